```python
import math
import jax, jax.numpy as jnp
from jax import lax
import numpy as np

D_MODEL = 1024
BATCH = 4
SEQ = 8192
DEPTH = 1
DEC_BATCH = 4
DEC_SEQ = 4096
PAST_LEN = 128

EPS = 1e-6
ROPE_THETA = 10000.0
BLOCK = 128
WINDOW = 128
HA = 8
KVA = 2
GA = HA // KVA
DA = 64
HB = 8
Q_RANK = 384
KV_RANK = 256
DN = 64
DR = 32
DV = 64
SPLIT_SIZES = (HA * DA, KVA * DA, KVA * DA, Q_RANK, KV_RANK, DR)
D_IN = sum(SPLIT_SIZES)
D_MIX = HA * DA + HB * DV
D_FF = int(math.ceil(8 * D_MODEL / 3 / 256) * 256)
NEG = -1e30

kernel_name = "hymba_swa_sink_mla_encoder"


def rms_norm(x, g):
    xf = x.astype(jnp.float32)
    y = xf * lax.rsqrt(jnp.mean(xf * xf, axis=-1, keepdims=True) + EPS)
    return (y * g.astype(jnp.float32)).astype(x.dtype)


def rope_tables(seq, dim):
    inv = 1.0 / (ROPE_THETA ** (jnp.arange(0, dim, 2, dtype=jnp.float32) / dim))
    ang = jnp.arange(seq, dtype=jnp.float32)[:, None] * inv[None, :]
    return jnp.cos(ang), jnp.sin(ang)


def apply_rope(x, cos, sin):
    xf = x.astype(jnp.float32)
    half = x.shape[-1] // 2
    x1, x2 = xf[..., :half], xf[..., half:]
    c = cos[None, :, None, :]
    s = sin[None, :, None, :]
    return jnp.concatenate([x1 * c - x2 * s, x2 * c + x1 * s], axis=-1).astype(x.dtype)


def window_gqa_sink(q, k, v, sink):
    B, S = q.shape[0], q.shape[1]
    nb = S // BLOCK
    qb = q.reshape(B, nb, BLOCK, KVA, GA, DA)
    pad = ((0, 0), (BLOCK, BLOCK), (0, 0), (0, 0))

    def bands(t):
        tb = jnp.pad(t, pad).reshape(B, nb + 2, BLOCK, KVA, DA)
        return jnp.concatenate([tb[:, :-2], tb[:, 1:-1], tb[:, 2:]], axis=2)

    kb, vb = bands(k), bands(v)
    s = jnp.einsum('bnqkgd,bnskd->bnkgqs', qb, kb).astype(jnp.float32) * (DA ** -0.5)
    blk = jnp.arange(nb)[:, None, None] * BLOCK
    qpos = blk + jnp.arange(BLOCK)[None, :, None]
    kpos = blk - BLOCK + jnp.arange(3 * BLOCK)[None, None, :]
    valid = (jnp.abs(qpos - kpos) <= WINDOW) & (kpos >= 0) & (kpos < S)
    s = jnp.where(valid[None, :, None, None], s, NEG)
    sink_l = sink.astype(jnp.float32).reshape(KVA, GA)[None, None, :, :, None, None]
    m = jnp.maximum(jnp.max(s, axis=-1, keepdims=True), sink_l)
    p = jnp.exp(s - m)
    denom = jnp.sum(p, axis=-1, keepdims=True) + jnp.exp(sink_l - m)
    o = jnp.einsum('bnkgqs,bnskd->bnqkgd', (p / denom).astype(v.dtype), vb)
    return o.reshape(B, S, HA * DA)


def mla(c_q, c_kv, k_rope, cq_g, w_uq, ckv_g, w_ukv, cos_r, sin_r):
    B, S = c_q.shape[0], c_q.shape[1]
    q = (rms_norm(c_q, cq_g) @ w_uq).reshape(B, S, HB, DN + DR)
    q_nope = q[..., :DN]
    q_rope = apply_rope(q[..., DN:], cos_r, sin_r)
    kv = (rms_norm(c_kv, ckv_g) @ w_ukv).reshape(B, S, HB, DN + DV)
    k_nope, v = kv[..., :DN], kv[..., DN:]
    k_r = apply_rope(k_rope[:, :, None, :], cos_r, sin_r)[:, :, 0]
    nb = S // BLOCK
    qn = q_nope.reshape(B, nb, BLOCK, HB, DN).transpose(1, 0, 2, 3, 4)
    qr = q_rope.reshape(B, nb, BLOCK, HB, DR).transpose(1, 0, 2, 3, 4)
    scale = (DN + DR) ** -0.5

    def block(args):
        qn_b, qr_b = args
        s = (jnp.einsum('bqhd,bshd->bhqs', qn_b, k_nope)
             + jnp.einsum('bqhr,bsr->bhqs', qr_b, k_r)).astype(jnp.float32) * scale
        p = jax.nn.softmax(s, axis=-1)
        return jnp.einsum('bhqs,bshd->bqhd', p.astype(v.dtype), v)

    o = lax.map(block, (qn, qr))
    return o.transpose(1, 0, 2, 3, 4).reshape(B, S, HB * DV)


def encoder_layer(x, cos_a, sin_a, cos_r, sin_r, g_mix, w_in, sink, cq_g, w_uq, ckv_g, w_ukv,
                  w_o, g_ffn, w_gate, w_up, w_down):
    B, S = x.shape[0], x.shape[1]
    h = rms_norm(x, g_mix)
    z = h @ w_in
    idx = list(np.cumsum(SPLIT_SIZES)[:-1])
    qa, ka, va, c_q, c_kv, k_rope = jnp.split(z, idx, axis=-1)
    qa = apply_rope(qa.reshape(B, S, HA, DA), cos_a, sin_a)
    ka = apply_rope(ka.reshape(B, S, KVA, DA), cos_a, sin_a)
    va = va.reshape(B, S, KVA, DA)
    o_a = window_gqa_sink(qa, ka, va, sink)
    o_b = mla(c_q, c_kv, k_rope, cq_g, w_uq, ckv_g, w_ukv, cos_r, sin_r)
    x = x + jnp.concatenate([o_a, o_b], axis=-1) @ w_o
    h = rms_norm(x, g_ffn)
    x = x + (jax.nn.silu(h @ w_gate) * (h @ w_up)) @ w_down
    return x


def trunk(x, g_mix, w_in, sink, cq_g, w_uq, ckv_g, w_ukv, w_o, g_ffn, w_gate, w_up, w_down, g_final):
    S = x.shape[1]
    cos_a, sin_a = rope_tables(S, DA)
    cos_r, sin_r = rope_tables(S, DR)
    for l in range(DEPTH):
        x = encoder_layer(x, cos_a, sin_a, cos_r, sin_r, g_mix[l], w_in[l], sink[l], cq_g[l], w_uq[l],
                          ckv_g[l], w_ukv[l], w_o[l], g_ffn[l], w_gate[l], w_up[l], w_down[l])
    return rms_norm(x, g_final)


def setup_inputs(seed: int = 0) -> dict:
    key = jax.random.key(seed)
    ks = jax.random.split(key, 16)
    f32 = jnp.float32

    def nrm(k, shape, scale):
        return jax.random.normal(k, shape, f32) * scale

    def gain(k, n):
        return 1.0 + 0.01 * jax.random.normal(k, (DEPTH, n), f32)

    return {
        "x_prompt": jax.random.normal(ks[0], (BATCH, SEQ, D_MODEL), f32),
        "x_sample": jax.random.normal(ks[1], (DEC_BATCH, DEC_SEQ, D_MODEL), f32),
        "g_mix": gain(ks[2], D_MODEL),
        "w_in": nrm(ks[3], (DEPTH, D_MODEL, D_IN), D_MODEL ** -0.5),
        "sink": nrm(ks[4], (DEPTH, HA), 0.5),
        "cq_g": gain(ks[5], Q_RANK),
        "w_uq": nrm(ks[6], (DEPTH, Q_RANK, HB * (DN + DR)), Q_RANK ** -0.5),
        "ckv_g": gain(ks[7], KV_RANK),
        "w_ukv": nrm(ks[8], (DEPTH, KV_RANK, HB * (DN + DV)), KV_RANK ** -0.5),
        "w_o": nrm(ks[9], (DEPTH, D_MIX, D_MODEL), D_MIX ** -0.5),
        "g_ffn": gain(ks[10], D_MODEL),
        "w_gate": nrm(ks[11], (DEPTH, D_MODEL, D_FF), D_MODEL ** -0.5),
        "w_up": nrm(ks[12], (DEPTH, D_MODEL, D_FF), D_MODEL ** -0.5),
        "w_down": nrm(ks[13], (DEPTH, D_FF, D_MODEL), D_FF ** -0.5),
        "g_final": 1.0 + 0.01 * jax.random.normal(ks[14], (D_MODEL,), f32),
    }


def reference(x_prompt, x_sample, g_mix, w_in, sink, cq_g, w_uq, ckv_g, w_ukv, w_o, g_ffn,
              w_gate, w_up, w_down, g_final):
    y_prompt = trunk(x_prompt, g_mix, w_in, sink, cq_g, w_uq, ckv_g, w_ukv, w_o, g_ffn,
                     w_gate, w_up, w_down, g_final)
    y_sample = trunk(x_sample, g_mix, w_in, sink, cq_g, w_uq, ckv_g, w_ukv, w_o, g_ffn,
                     w_gate, w_up, w_down, g_final)
    return (y_prompt, y_sample)
```

```python
import functools
import math

import jax
import jax.numpy as jnp
from jax import lax
from jax.experimental import pallas as pl
from jax.experimental.pallas import tpu as pltpu

D_MODEL = 1024
EPS = 1e-6
ROPE_THETA = 10000.0
WINDOW = 128
HA, KVA, DA = 8, 2, 64
GA = HA // KVA
HB, Q_RANK, KV_RANK, DN, DR, DV = 8, 384, 256, 64, 32, 64
D_IN = HA * DA + 2 * KVA * DA + Q_RANK + KV_RANK + DR
D_FF = int(math.ceil(8 * D_MODEL / 3 / 256) * 256)
NEG = -1e30

LANE = 128
OFF_KA = HA * DA
OFF_VA = OFF_KA + KVA * DA
OFF_CQ = OFF_VA + KVA * DA
OFF_CKV = OFF_CQ + Q_RANK
OFF_KR = OFF_CKV + KV_RANK

TOK_TILE = 512
SWA_TILE = 512
MLA_Q_TILE = 512
VMEM_LIMIT = 56 * 1024 * 1024

F32 = jnp.float32
BF16 = jnp.bfloat16


def _dot(a, b):
    return jnp.dot(a, b, preferred_element_type=F32)


def _dot_tn(a, b):
    return lax.dot_general(a, b, (((0,), (0,)), ((), ())), preferred_element_type=F32)


def _dot_nt(a, b):
    return lax.dot_general(a, b, (((1,), (1,)), ((), ())), preferred_element_type=F32)


def _rope_rows(x1, x2, c, s):
    return x1 * c - x2 * s, x2 * c + x1 * s


def _proj_kernel(x_ref, gmix_ref, winT_ref, cqg_ref, wuqT_ref, ckvg_ref, wk_ref, wvT_ref,
                 cosa_ref, sina_ref, cosr_ref, sinr_ref,
                 qaT_ref, ka_ref, vaT_ref, qT_ref, k_ref, vT_ref):
    x = x_ref[0]
    ms = jnp.mean(x * x, axis=-1, keepdims=True)
    h = (x * lax.rsqrt(ms + EPS) * gmix_ref[...]).astype(BF16)
    zT = _dot_nt(winT_ref[...], h)

    ca, sa = cosa_ref[...], sina_ref[...]
    cr, sr = cosr_ref[...], sinr_ref[...]
    half = DA // 2

    qaT_ref[0] = jnp.zeros(qaT_ref.shape[1:], BF16)
    for hq in range(HA):
        r = hq * DA
        o1, o2 = _rope_rows(zT[r:r + half], zT[r + half:r + DA], ca, sa)
        base = hq * LANE + (hq // GA) * DA
        qaT_ref[0, base:base + half, :] = (o1 * DA ** -0.5).astype(BF16)
        qaT_ref[0, base + half:base + DA, :] = (o2 * DA ** -0.5).astype(BF16)

    kaT = []
    for g in range(KVA):
        r = OFF_KA + g * DA
        kaT.extend(_rope_rows(zT[r:r + half], zT[r + half:r + DA], ca, sa))
    ka_ref[0] = jnp.concatenate(kaT, axis=0).T.astype(BF16)
    vaT_ref[0] = zT[OFF_VA:OFF_CQ].astype(BF16)

    cq = zT[OFF_CQ:OFF_CKV]
    cqn = (cq * lax.rsqrt(jnp.mean(cq * cq, axis=0, keepdims=True) + EPS) * cqg_ref[...]).astype(BF16)
    qT = _dot(wuqT_ref[...], cqn) * (DN + DR) ** -0.5
    qT_ref[0] = qT.astype(BF16)
    hr = DR // 2
    for hh in range(HB):
        r = hh * LANE + DN
        o1, o2 = _rope_rows(qT[r:r + hr], qT[r + hr:r + DR], cr, sr)
        qT_ref[0, r:r + hr, :] = o1.astype(BF16)
        qT_ref[0, r + hr:r + DR, :] = o2.astype(BF16)

    ckv = zT[OFF_CKV:OFF_KR]
    ckvn = (ckv * lax.rsqrt(jnp.mean(ckv * ckv, axis=0, keepdims=True) + EPS) * ckvg_ref[...]).astype(BF16)
    krT = zT[OFF_KR:D_IN]
    k1, k2 = _rope_rows(krT[:hr], krT[hr:], cr, sr)
    kin = jnp.concatenate([ckvn, k1.astype(BF16), k2.astype(BF16)], axis=0)
    k_ref[0] = _dot_tn(kin, wk_ref[...]).astype(BF16)
    vT_ref[0, :, 0] = _dot(wvT_ref[...], ckvn).astype(BF16).reshape(HB, DV, -1)


def _proj(x, p, tabs):
    B, S, _ = x.shape
    tm = TOK_TILE
    nt = S // tm
    full = lambda a: pl.BlockSpec(a.shape, lambda b, i: (0,) * a.ndim)
    tab = lambda a: pl.BlockSpec((a.shape[0], tm), lambda b, i: (0, i))
    weights = (p["g_mix"], p["w_inT"], p["cq_g"], p["w_uqT"], p["ckv_g"], p["w_k"], p["w_vT"])
    out_shape = (
        jax.ShapeDtypeStruct((B, HA * LANE, S), BF16),
        jax.ShapeDtypeStruct((B, S, KVA * DA), BF16),
        jax.ShapeDtypeStruct((B, KVA * DA, S), BF16),
        jax.ShapeDtypeStruct((B, HB * LANE, S), BF16),
        jax.ShapeDtypeStruct((B, S, HB * LANE), BF16),
        jax.ShapeDtypeStruct((B, HB, nt, DV, tm), BF16),
    )
    out_specs = (
        pl.BlockSpec((1, HA * LANE, tm), lambda b, i: (b, 0, i)),
        pl.BlockSpec((1, tm, KVA * DA), lambda b, i: (b, i, 0)),
        pl.BlockSpec((1, KVA * DA, tm), lambda b, i: (b, 0, i)),
        pl.BlockSpec((1, HB * LANE, tm), lambda b, i: (b, 0, i)),
        pl.BlockSpec((1, tm, HB * LANE), lambda b, i: (b, i, 0)),
        pl.BlockSpec((1, HB, 1, DV, tm), lambda b, i: (b, 0, i, 0, 0)),
    )
    return pl.pallas_call(
        _proj_kernel,
        grid=(B, nt),
        in_specs=[pl.BlockSpec((1, tm, D_MODEL), lambda b, i: (b, i, 0))]
        + [full(w) for w in weights] + [tab(t) for t in tabs],
        out_specs=out_specs,
        out_shape=out_shape,
        compiler_params=pltpu.CompilerParams(
            dimension_semantics=("parallel", "parallel"), vmem_limit_bytes=VMEM_LIMIT),
        name="proj",
    )(x, *weights, *tabs)


def _swa_kernel(sink_ref, qaT_ref, kp_ref, kc_ref, kn_ref, vp_ref, vc_ref, vn_ref, oT_ref, *, seq):
    tq = qaT_ref.shape[2]
    q0 = pl.program_id(1) * tq
    kcat = jnp.concatenate([kp_ref[0], kc_ref[0], kn_ref[0]], axis=0)
    vcat = jnp.concatenate([vp_ref[0], vc_ref[0], vn_ref[0]], axis=1)
    nk = 3 * WINDOW
    r = lax.broadcasted_iota(jnp.int32, (nk, WINDOW), 0)
    l = lax.broadcasted_iota(jnp.int32, (nk, WINDOW), 1)
    band = (r - l >= 0) & (r - l <= 2 * WINDOW)
    for c in range(tq // WINDOW):
        kpos = q0 + (c - 1) * WINDOW + r
        valid = band & (kpos >= 0) & (kpos < seq)
        kblk = kcat[c * WINDOW:c * WINDOW + nk]
        for hq in range(HA):
            g = hq // GA
            q = qaT_ref[0, hq * LANE:(hq + 1) * LANE, c * WINDOW:(c + 1) * WINDOW]
            s = jnp.where(valid, _dot(kblk, q), NEG)
            sink = sink_ref[hq]
            m = jnp.maximum(jnp.max(s, axis=0, keepdims=True), sink)
            p = jnp.exp(s - m)
            denom = jnp.sum(p, axis=0, keepdims=True) + jnp.exp(sink - m)
            vblk = vcat[g * DA:(g + 1) * DA, c * WINDOW:c * WINDOW + nk]
            o = _dot(vblk, p.astype(BF16)) / denom
            oT_ref[0, hq * DA:(hq + 1) * DA, c * WINDOW:(c + 1) * WINDOW] = o.astype(BF16)


def _swa(sink, qaT, ka, vaT):
    B, _, S = qaT.shape
    tq = SWA_TILE
    r = tq // WINDOW
    nb = S // WINDOW
    prev = lambda b, i, *_: (b, jnp.maximum(i * r - 1, 0), 0)
    nxt = lambda b, i, *_: (b, jnp.minimum(i * r + r, nb - 1), 0)
    prev_t = lambda b, i, *_: (b, 0, jnp.maximum(i * r - 1, 0))
    nxt_t = lambda b, i, *_: (b, 0, jnp.minimum(i * r + r, nb - 1))
    grid_spec = pltpu.PrefetchScalarGridSpec(
        num_scalar_prefetch=1,
        grid=(B, S // tq),
        in_specs=[
            pl.BlockSpec((1, HA * LANE, tq), lambda b, i, *_: (b, 0, i)),
            pl.BlockSpec((1, WINDOW, KVA * DA), prev),
            pl.BlockSpec((1, tq, KVA * DA), lambda b, i, *_: (b, i, 0)),
            pl.BlockSpec((1, WINDOW, KVA * DA), nxt),
            pl.BlockSpec((1, KVA * DA, WINDOW), prev_t),
            pl.BlockSpec((1, KVA * DA, tq), lambda b, i, *_: (b, 0, i)),
            pl.BlockSpec((1, KVA * DA, WINDOW), nxt_t),
        ],
        out_specs=pl.BlockSpec((1, HA * DA, tq), lambda b, i, *_: (b, 0, i)),
    )
    return pl.pallas_call(
        functools.partial(_swa_kernel, seq=S),
        grid_spec=grid_spec,
        out_shape=jax.ShapeDtypeStruct((B, HA * DA, S), BF16),
        compiler_params=pltpu.CompilerParams(
            dimension_semantics=("parallel", "parallel"), vmem_limit_bytes=VMEM_LIMIT),
        name="swa",
    )(sink, qaT, ka, ka, ka, vaT, vaT, vaT)


def _mla_kernel(qT_ref, k_ref, vT_ref, oT_ref):
    q = qT_ref[0]
    tq = q.shape[1]
    nkb, _, tk = vT_ref.shape[2:]

    def body(j, carry):
        m, l, acc = carry
        kb = k_ref[0, pl.ds(pl.multiple_of(j * tk, tk), tk), :]
        s = _dot(kb, q)
        m_new = jnp.maximum(m, jnp.max(s, axis=0, keepdims=True))
        alpha = jnp.exp(m - m_new)
        p = jnp.exp(s - m_new)
        l = alpha * l + jnp.sum(p, axis=0, keepdims=True)
        acc = alpha * acc + _dot(vT_ref[0, 0, j], p.astype(BF16))
        return m_new, l, acc

    init = (jnp.full((1, tq), -jnp.inf, F32), jnp.zeros((1, tq), F32), jnp.zeros((DV, tq), F32))
    _, l, acc = lax.fori_loop(0, nkb, body, init)
    oT_ref[0] = (acc / l).astype(BF16)


def _mla(qT, k, vT):
    B, _, S = qT.shape
    nkb, _, tk = vT.shape[2:]
    tq = MLA_Q_TILE
    return pl.pallas_call(
        _mla_kernel,
        grid=(B, HB, S // tq),
        in_specs=[
            pl.BlockSpec((1, LANE, tq), lambda b, h, i: (b, h, i)),
            pl.BlockSpec((1, S, LANE), lambda b, h, i: (b, 0, h)),
            pl.BlockSpec((1, 1, nkb, DV, tk), lambda b, h, i: (b, h, 0, 0, 0)),
        ],
        out_specs=pl.BlockSpec((1, DV, tq), lambda b, h, i: (b, h, i)),
        out_shape=jax.ShapeDtypeStruct((B, HB * DV, S), BF16),
        compiler_params=pltpu.CompilerParams(
            dimension_semantics=("parallel", "parallel", "arbitrary"), vmem_limit_bytes=VMEM_LIMIT),
        name="mla",
    )(qT, k, vT)


def _ffn_kernel(x_ref, oaT_ref, obT_ref, woa_ref, wob_ref, gffn_ref, wg_ref, wu_ref, wd_ref, gfin_ref, y_ref):
    x = x_ref[0]
    x = x + _dot_tn(oaT_ref[0], woa_ref[...]) + _dot_tn(obT_ref[0], wob_ref[...])
    h = (x * lax.rsqrt(jnp.mean(x * x, axis=-1, keepdims=True) + EPS) * gffn_ref[...]).astype(BF16)
    g = _dot(h, wg_ref[...])
    u = _dot(h, wu_ref[...])
    a = (g / (1.0 + jnp.exp(-g)) * u).astype(BF16)
    x = x + _dot(a, wd_ref[...])
    y_ref[0] = x * lax.rsqrt(jnp.mean(x * x, axis=-1, keepdims=True) + EPS) * gfin_ref[...]


def _ffn(x, oaT, obT, p):
    B, S, _ = x.shape
    tm = TOK_TILE
    full = lambda a: pl.BlockSpec(a.shape, lambda b, i: (0,) * a.ndim, pipeline_mode=pl.Buffered(1))
    weights = (p["w_oa"], p["w_ob"], p["g_ffn"], p["w_gate"], p["w_up"], p["w_down"], p["g_final"])
    return pl.pallas_call(
        _ffn_kernel,
        grid=(B, S // tm),
        in_specs=[
            pl.BlockSpec((1, tm, D_MODEL), lambda b, i: (b, i, 0)),
            pl.BlockSpec((1, HA * DA, tm), lambda b, i: (b, 0, i)),
            pl.BlockSpec((1, HB * DV, tm), lambda b, i: (b, 0, i)),
        ] + [full(w) for w in weights],
        out_specs=pl.BlockSpec((1, tm, D_MODEL), lambda b, i: (b, i, 0)),
        out_shape=jax.ShapeDtypeStruct(x.shape, F32),
        compiler_params=pltpu.CompilerParams(
            dimension_semantics=("parallel", "parallel"), vmem_limit_bytes=VMEM_LIMIT),
        name="ffn",
    )(x, oaT, obT, *weights)


def _rope_tables_t(seq, dim):
    inv = 1.0 / (ROPE_THETA ** (jnp.arange(0, dim, 2, dtype=F32) / dim))
    ang = jnp.arange(seq, dtype=F32)[:, None] * inv[None, :]
    return jnp.cos(ang).T, jnp.sin(ang).T


def _prep_params(g_mix, w_in, sink, cq_g, w_uq, ckv_g, w_ukv, w_o, g_ffn, w_gate, w_up, w_down, g_final):
    w_uq3 = w_uq.reshape(Q_RANK, HB, DN + DR)
    w_uq_pad = jnp.pad(w_uq3, ((0, 0), (0, 0), (0, LANE - DN - DR))).reshape(Q_RANK, HB * LANE)
    w_ukv3 = w_ukv.reshape(KV_RANK, HB, DN + DV)
    w_uk, w_uv = w_ukv3[..., :DN], w_ukv3[..., DN:]
    top = jnp.pad(w_uk, ((0, 0), (0, 0), (0, LANE - DN)))
    eye = jnp.pad(jnp.eye(DR, dtype=F32)[:, None, :], ((0, 0), (0, 0), (DN, LANE - DN - DR)))
    bot = jnp.broadcast_to(eye, (DR, HB, LANE))
    w_k = jnp.concatenate([top, bot], axis=0).reshape(KV_RANK + DR, HB * LANE)
    return {
        "g_mix": g_mix.reshape(1, D_MODEL),
        "w_inT": w_in.T.astype(BF16),
        "cq_g": cq_g.reshape(Q_RANK, 1),
        "w_uqT": w_uq_pad.T.astype(BF16),
        "ckv_g": ckv_g.reshape(KV_RANK, 1),
        "w_k": w_k.astype(BF16),
        "w_vT": w_uv.reshape(KV_RANK, HB * DV).T.astype(BF16),
        "sink": sink,
        "w_oa": w_o[:HA * DA].astype(BF16),
        "w_ob": w_o[HA * DA:].astype(BF16),
        "g_ffn": g_ffn.reshape(1, D_MODEL),
        "w_gate": w_gate.astype(BF16),
        "w_up": w_up.astype(BF16),
        "w_down": w_down.astype(BF16),
        "g_final": g_final.reshape(1, D_MODEL),
    }


def _trunk(x, p):
    S = x.shape[1]
    tabs = _rope_tables_t(S, DA) + _rope_tables_t(S, DR)
    qaT, ka, vaT, qT, k, vT = _proj(x, p, tabs)
    oaT = _swa(p["sink"], qaT, ka, vaT)
    obT = _mla(qT, k, vT)
    return _ffn(x, oaT, obT, p)


def kernel(x_prompt, x_sample, g_mix, w_in, sink, cq_g, w_uq, ckv_g, w_ukv, w_o, g_ffn,
           w_gate, w_up, w_down, g_final):
    p = _prep_params(g_mix[0], w_in[0], sink[0], cq_g[0], w_uq[0], ckv_g[0], w_ukv[0], w_o[0],
                     g_ffn[0], w_gate[0], w_up[0], w_down[0], g_final)
    return _trunk(x_prompt, p), _trunk(x_sample, p)
```

```python
import functools
import math

import jax
import jax.numpy as jnp
from jax import lax
from jax.experimental import pallas as pl
from jax.experimental.pallas import tpu as pltpu

D_MODEL = 1024
EPS = 1e-6
ROPE_THETA = 10000.0
WINDOW = 128
HA, KVA, DA = 8, 2, 64
GA = HA // KVA
HB, Q_RANK, KV_RANK, DN, DR, DV = 8, 384, 256, 64, 32, 64
D_IN = HA * DA + 2 * KVA * DA + Q_RANK + KV_RANK + DR
D_FF = int(math.ceil(8 * D_MODEL / 3 / 256) * 256)
NEG = -1e30

LANE = 128
BF16_ROWS = 16
V_ROWS = DV + BF16_ROWS
MLA_Q_SCALE = (DN + DR) ** -0.5 * math.log2(math.e)
OFF_KA = HA * DA
OFF_VA = OFF_KA + KVA * DA
OFF_CQ = OFF_VA + KVA * DA
OFF_CKV = OFF_CQ + Q_RANK
OFF_KR = OFF_CKV + KV_RANK

TOK_TILE = 512
SWA_TILE = 512
MLA_KEY_BLOCK = 1024
VMEM_LIMIT = 56 * 1024 * 1024

F32 = jnp.float32
BF16 = jnp.bfloat16


def _dot(a, b):
    return jnp.dot(a, b, preferred_element_type=F32)


def _dot_tn(a, b):
    return lax.dot_general(a, b, (((0,), (0,)), ((), ())), preferred_element_type=F32)


def _dot_nt(a, b):
    return lax.dot_general(a, b, (((1,), (1,)), ((), ())), preferred_element_type=F32)


def _rope_rows(x1, x2, c, s):
    return x1 * c - x2 * s, x2 * c + x1 * s


def _proj_kernel(x_ref, gmix_ref, winT_ref, cqg_ref, wuqT_ref, ckvg_ref, wk_ref, wvT_ref,
                 cosa_ref, sina_ref, cosr_ref, sinr_ref,
                 qaT_ref, ka_ref, vaT_ref, qT_ref, k_ref, vT_ref):
    x = x_ref[0]
    ms = jnp.mean(x * x, axis=-1, keepdims=True)
    h = (x * lax.rsqrt(ms + EPS) * gmix_ref[...]).astype(BF16)
    zT = _dot_nt(winT_ref[...], h)

    ca, sa = cosa_ref[...], sina_ref[...]
    cr, sr = cosr_ref[...], sinr_ref[...]
    half = DA // 2

    qaT_ref[0] = jnp.zeros(qaT_ref.shape[1:], BF16)
    for hq in range(HA):
        r = hq * DA
        o1, o2 = _rope_rows(zT[r:r + half], zT[r + half:r + DA], ca, sa)
        base = hq * LANE + (hq // GA) * DA
        qaT_ref[0, base:base + half, :] = (o1 * DA ** -0.5).astype(BF16)
        qaT_ref[0, base + half:base + DA, :] = (o2 * DA ** -0.5).astype(BF16)

    kaT = []
    for g in range(KVA):
        r = OFF_KA + g * DA
        kaT.extend(_rope_rows(zT[r:r + half], zT[r + half:r + DA], ca, sa))
    ka_ref[0] = jnp.concatenate(kaT, axis=0).T.astype(BF16)
    vaT_ref[0] = zT[OFF_VA:OFF_CQ].astype(BF16)

    cq = zT[OFF_CQ:OFF_CKV]
    cqn = (cq * lax.rsqrt(jnp.mean(cq * cq, axis=0, keepdims=True) + EPS) * cqg_ref[...]).astype(BF16)
    qT = _dot(wuqT_ref[...], cqn) * MLA_Q_SCALE
    tm = qT.shape[1]
    qT_ref[0, :, 0] = qT.astype(BF16).reshape(HB, LANE, tm)
    hr = DR // 2
    for hh in range(HB):
        r = hh * LANE + DN
        o1, o2 = _rope_rows(qT[r:r + hr], qT[r + hr:r + DR], cr, sr)
        qT_ref[0, hh, 0, DN:DN + hr, :] = o1.astype(BF16)
        qT_ref[0, hh, 0, DN + hr:DN + DR, :] = o2.astype(BF16)

    ckv = zT[OFF_CKV:OFF_KR]
    ckvn = (ckv * lax.rsqrt(jnp.mean(ckv * ckv, axis=0, keepdims=True) + EPS) * ckvg_ref[...]).astype(BF16)
    krT = zT[OFF_KR:D_IN]
    k1, k2 = _rope_rows(krT[:hr], krT[hr:], cr, sr)
    kin = jnp.concatenate([ckvn, k1.astype(BF16), k2.astype(BF16)], axis=0)
    k_ref[0] = _dot_tn(kin, wk_ref[...]).astype(BF16)
    vT_ref[0, :, 0, :DV] = _dot(wvT_ref[...], ckvn).astype(BF16).reshape(HB, DV, tm)
    ones_row = lax.broadcasted_iota(jnp.int32, (HB, V_ROWS - DV, tm), 1) == 0
    vT_ref[0, :, 0, DV:] = ones_row.astype(BF16)


def _proj(x, p, tabs):
    B, S, _ = x.shape
    tm = TOK_TILE
    nt = S // tm
    full = lambda a: pl.BlockSpec(a.shape, lambda b, i: (0,) * a.ndim)
    tab = lambda a: pl.BlockSpec((a.shape[0], tm), lambda b, i: (0, i))
    weights = (p["g_mix"], p["w_inT"], p["cq_g"], p["w_uqT"], p["ckv_g"], p["w_k"], p["w_vT"])
    out_shape = (
        jax.ShapeDtypeStruct((B, HA * LANE, S), BF16),
        jax.ShapeDtypeStruct((B, S, KVA * DA), BF16),
        jax.ShapeDtypeStruct((B, KVA * DA, S), BF16),
        jax.ShapeDtypeStruct((B, HB, nt, LANE, tm), BF16),
        jax.ShapeDtypeStruct((B, S, HB * LANE), BF16),
        jax.ShapeDtypeStruct((B, HB, nt, V_ROWS, tm), BF16),
    )
    out_specs = (
        pl.BlockSpec((1, HA * LANE, tm), lambda b, i: (b, 0, i)),
        pl.BlockSpec((1, tm, KVA * DA), lambda b, i: (b, i, 0)),
        pl.BlockSpec((1, KVA * DA, tm), lambda b, i: (b, 0, i)),
        pl.BlockSpec((1, HB, 1, LANE, tm), lambda b, i: (b, 0, i, 0, 0)),
        pl.BlockSpec((1, tm, HB * LANE), lambda b, i: (b, i, 0)),
        pl.BlockSpec((1, HB, 1, V_ROWS, tm), lambda b, i: (b, 0, i, 0, 0)),
    )
    return pl.pallas_call(
        _proj_kernel,
        grid=(B, nt),
        in_specs=[pl.BlockSpec((1, tm, D_MODEL), lambda b, i: (b, i, 0))]
        + [full(w) for w in weights] + [tab(t) for t in tabs],
        out_specs=out_specs,
        out_shape=out_shape,
        compiler_params=pltpu.CompilerParams(
            dimension_semantics=("parallel", "parallel"), vmem_limit_bytes=VMEM_LIMIT),
        name="proj",
    )(x, *weights, *tabs)


def _swa_kernel(sink_ref, qaT_ref, kp_ref, kc_ref, kn_ref, vp_ref, vc_ref, vn_ref, oT_ref, *, seq):
    tq = qaT_ref.shape[2]
    q0 = pl.program_id(1) * tq
    kcat = jnp.concatenate([kp_ref[0], kc_ref[0], kn_ref[0]], axis=0)
    vcat = jnp.concatenate([vp_ref[0], vc_ref[0], vn_ref[0]], axis=1)
    nk = 3 * WINDOW
    r = lax.broadcasted_iota(jnp.int32, (nk, WINDOW), 0)
    l = lax.broadcasted_iota(jnp.int32, (nk, WINDOW), 1)
    band = (r - l >= 0) & (r - l <= 2 * WINDOW)
    for c in range(tq // WINDOW):
        kpos = q0 + (c - 1) * WINDOW + r
        valid = band & (kpos >= 0) & (kpos < seq)
        kblk = kcat[c * WINDOW:c * WINDOW + nk]
        for hq in range(HA):
            g = hq // GA
            q = qaT_ref[0, hq * LANE:(hq + 1) * LANE, c * WINDOW:(c + 1) * WINDOW]
            s = jnp.where(valid, _dot(kblk, q), NEG)
            sink = sink_ref[hq]
            m = jnp.maximum(jnp.max(s, axis=0, keepdims=True), sink)
            p = jnp.exp(s - m)
            denom = jnp.sum(p, axis=0, keepdims=True) + jnp.exp(sink - m)
            vblk = vcat[g * DA:(g + 1) * DA, c * WINDOW:c * WINDOW + nk]
            o = _dot(vblk, p.astype(BF16)) / denom
            oT_ref[0, hq * DA:(hq + 1) * DA, c * WINDOW:(c + 1) * WINDOW] = o.astype(BF16)


def _swa(sink, qaT, ka, vaT):
    B, _, S = qaT.shape
    tq = SWA_TILE
    r = tq // WINDOW
    nb = S // WINDOW
    prev = lambda b, i, *_: (b, jnp.maximum(i * r - 1, 0), 0)
    nxt = lambda b, i, *_: (b, jnp.minimum(i * r + r, nb - 1), 0)
    prev_t = lambda b, i, *_: (b, 0, jnp.maximum(i * r - 1, 0))
    nxt_t = lambda b, i, *_: (b, 0, jnp.minimum(i * r + r, nb - 1))
    grid_spec = pltpu.PrefetchScalarGridSpec(
        num_scalar_prefetch=1,
        grid=(B, S // tq),
        in_specs=[
            pl.BlockSpec((1, HA * LANE, tq), lambda b, i, *_: (b, 0, i)),
            pl.BlockSpec((1, WINDOW, KVA * DA), prev),
            pl.BlockSpec((1, tq, KVA * DA), lambda b, i, *_: (b, i, 0)),
            pl.BlockSpec((1, WINDOW, KVA * DA), nxt),
            pl.BlockSpec((1, KVA * DA, WINDOW), prev_t),
            pl.BlockSpec((1, KVA * DA, tq), lambda b, i, *_: (b, 0, i)),
            pl.BlockSpec((1, KVA * DA, WINDOW), nxt_t),
        ],
        out_specs=pl.BlockSpec((1, HA * DA, tq), lambda b, i, *_: (b, 0, i)),
    )
    return pl.pallas_call(
        functools.partial(_swa_kernel, seq=S),
        grid_spec=grid_spec,
        out_shape=jax.ShapeDtypeStruct((B, HA * DA, S), BF16),
        compiler_params=pltpu.CompilerParams(
            dimension_semantics=("parallel", "parallel"), vmem_limit_bytes=VMEM_LIMIT),
        name="swa",
    )(sink, qaT, ka, ka, ka, vaT, vaT, vaT)


def _mla_kernel(qT_ref, k_ref, vT_ref, oT_ref, s_ref):
    nq, _, tq = qT_ref.shape[2:]
    nvb, nv, tv = vT_ref.shape[2:]
    tk = s_ref.shape[1]
    nkb = nvb * tv // tk
    vper = tk // tv

    def scores(qi, j):
        kb = k_ref[0, pl.ds(pl.multiple_of(j * tk, tk), tk), :]
        return _dot(kb, qT_ref[0, 0, qi])

    def softmax_pv(slot, j, m, acc):
        s = s_ref[slot]
        m_new = jnp.maximum(m, jnp.max(s, axis=0, keepdims=True))
        alpha = jnp.exp2(m - m_new)
        p = jnp.exp2(s - m_new).astype(BF16)
        pv = _dot(vT_ref[0, 0, j * vper], p[:tv])
        for c in range(1, vper):
            pv += _dot(vT_ref[0, 0, j * vper + c], p[c * tv:(c + 1) * tv])
        return m_new, alpha * acc + pv

    def q_tile(qi, _):
        def pair(jj, carry):
            j = 2 * jj
            s_ref[1] = scores(qi, j + 1)
            m, acc = softmax_pv(0, j, *carry)
            s_ref[0] = scores(qi, j + 2)
            return softmax_pv(1, j + 1, m, acc)

        init = (jnp.full((1, tq), -jnp.inf, F32), jnp.zeros((nv, tq), F32))
        m, acc = lax.fori_loop(0, nkb // 2 - 1, pair, init)
        s_ref[1] = scores(qi, nkb - 1)
        m, acc = softmax_pv(0, nkb - 2, m, acc)
        s_ref[0] = scores(jnp.minimum(qi + 1, nq - 1), 0)
        m, acc = softmax_pv(1, nkb - 1, m, acc)
        oT_ref[0, 0, qi] = (acc[:DV] / acc[DV:DV + 1]).astype(BF16)
        return 0

    s_ref[0] = scores(0, 0)
    lax.fori_loop(0, nq, q_tile, 0)


def _mla(qT, k, vT):
    B, _, nq, _, tq = qT.shape
    nvb, nv, tv = vT.shape[2:]
    S = nq * tq
    tk = MLA_KEY_BLOCK
    assert tk % tv == 0 and S % (2 * tk) == 0
    return pl.pallas_call(
        _mla_kernel,
        grid=(B, HB),
        in_specs=[
            pl.BlockSpec((1, 1, nq, LANE, tq), lambda b, h: (b, h, 0, 0, 0)),
            pl.BlockSpec((1, S, LANE), lambda b, h: (b, 0, h)),
            pl.BlockSpec((1, 1, nvb, nv, tv), lambda b, h: (b, h, 0, 0, 0)),
        ],
        out_specs=pl.BlockSpec((1, 1, nq, DV, tq), lambda b, h: (b, h, 0, 0, 0)),
        out_shape=jax.ShapeDtypeStruct((B, HB, nq, DV, tq), BF16),
        scratch_shapes=[pltpu.VMEM((2, tk, tq), F32)],
        compiler_params=pltpu.CompilerParams(
            dimension_semantics=("parallel", "parallel"), vmem_limit_bytes=VMEM_LIMIT),
        name="mla",
    )(qT, k, vT)


def _ffn_kernel(x_ref, oaT_ref, obT_ref, woa_ref, wob_ref, gffn_ref, wg_ref, wu_ref, wd_ref, gfin_ref, y_ref):
    x = x_ref[0]
    obT = obT_ref[0, :, 0].reshape(HB * DV, -1)
    x = x + _dot_tn(oaT_ref[0], woa_ref[...]) + _dot_tn(obT, wob_ref[...])
    h = (x * lax.rsqrt(jnp.mean(x * x, axis=-1, keepdims=True) + EPS) * gffn_ref[...]).astype(BF16)
    g = _dot(h, wg_ref[...])
    u = _dot(h, wu_ref[...])
    a = (g / (1.0 + jnp.exp(-g)) * u).astype(BF16)
    x = x + _dot(a, wd_ref[...])
    y_ref[0] = x * lax.rsqrt(jnp.mean(x * x, axis=-1, keepdims=True) + EPS) * gfin_ref[...]


def _ffn(x, oaT, obT, p):
    B, S, _ = x.shape
    tm = TOK_TILE
    full = lambda a: pl.BlockSpec(a.shape, lambda b, i: (0,) * a.ndim, pipeline_mode=pl.Buffered(1))
    weights = (p["w_oa"], p["w_ob"], p["g_ffn"], p["w_gate"], p["w_up"], p["w_down"], p["g_final"])
    return pl.pallas_call(
        _ffn_kernel,
        grid=(B, S // tm),
        in_specs=[
            pl.BlockSpec((1, tm, D_MODEL), lambda b, i: (b, i, 0)),
            pl.BlockSpec((1, HA * DA, tm), lambda b, i: (b, 0, i)),
            pl.BlockSpec((1, HB, 1, DV, tm), lambda b, i: (b, 0, i, 0, 0)),
        ] + [full(w) for w in weights],
        out_specs=pl.BlockSpec((1, tm, D_MODEL), lambda b, i: (b, i, 0)),
        out_shape=jax.ShapeDtypeStruct(x.shape, F32),
        compiler_params=pltpu.CompilerParams(
            dimension_semantics=("parallel", "parallel"), vmem_limit_bytes=VMEM_LIMIT),
        name="ffn",
    )(x, oaT, obT, *weights)


def _rope_tables_t(seq, dim):
    inv = 1.0 / (ROPE_THETA ** (jnp.arange(0, dim, 2, dtype=F32) / dim))
    ang = jnp.arange(seq, dtype=F32)[:, None] * inv[None, :]
    return jnp.cos(ang).T, jnp.sin(ang).T


def _prep_params(g_mix, w_in, sink, cq_g, w_uq, ckv_g, w_ukv, w_o, g_ffn, w_gate, w_up, w_down, g_final):
    w_uq3 = w_uq.reshape(Q_RANK, HB, DN + DR)
    w_uq_pad = jnp.pad(w_uq3, ((0, 0), (0, 0), (0, LANE - DN - DR))).reshape(Q_RANK, HB * LANE)
    w_ukv3 = w_ukv.reshape(KV_RANK, HB, DN + DV)
    w_uk, w_uv = w_ukv3[..., :DN], w_ukv3[..., DN:]
    top = jnp.pad(w_uk, ((0, 0), (0, 0), (0, LANE - DN)))
    eye = jnp.pad(jnp.eye(DR, dtype=F32)[:, None, :], ((0, 0), (0, 0), (DN, LANE - DN - DR)))
    bot = jnp.broadcast_to(eye, (DR, HB, LANE))
    w_k = jnp.concatenate([top, bot], axis=0).reshape(KV_RANK + DR, HB * LANE)
    return {
        "g_mix": g_mix.reshape(1, D_MODEL),
        "w_inT": w_in.T.astype(BF16),
        "cq_g": cq_g.reshape(Q_RANK, 1),
        "w_uqT": w_uq_pad.T.astype(BF16),
        "ckv_g": ckv_g.reshape(KV_RANK, 1),
        "w_k": w_k.astype(BF16),
        "w_vT": w_uv.reshape(KV_RANK, HB * DV).T.astype(BF16),
        "sink": sink,
        "w_oa": w_o[:HA * DA].astype(BF16),
        "w_ob": w_o[HA * DA:].astype(BF16),
        "g_ffn": g_ffn.reshape(1, D_MODEL),
        "w_gate": w_gate.astype(BF16),
        "w_up": w_up.astype(BF16),
        "w_down": w_down.astype(BF16),
        "g_final": g_final.reshape(1, D_MODEL),
    }


def _trunk(x, p):
    S = x.shape[1]
    tabs = _rope_tables_t(S, DA) + _rope_tables_t(S, DR)
    qaT, ka, vaT, qT, k, vT = _proj(x, p, tabs)
    oaT = _swa(p["sink"], qaT, ka, vaT)
    obT = _mla(qT, k, vT)
    return _ffn(x, oaT, obT, p)


def kernel(x_prompt, x_sample, g_mix, w_in, sink, cq_g, w_uq, ckv_g, w_ukv, w_o, g_ffn,
           w_gate, w_up, w_down, g_final):
    p = _prep_params(g_mix[0], w_in[0], sink[0], cq_g[0], w_uq[0], ckv_g[0], w_ukv[0], w_o[0],
                     g_ffn[0], w_gate[0], w_up[0], w_down[0], g_final)
    return _trunk(x_prompt, p), _trunk(x_sample, p)
```

```python
import functools
import math

import jax
import jax.numpy as jnp
from jax import lax
from jax.experimental import pallas as pl
from jax.experimental.pallas import tpu as pltpu

D_MODEL = 1024
EPS = 1e-6
ROPE_THETA = 10000.0
WINDOW = 128
HA, KVA, DA = 8, 2, 64
GA = HA // KVA
HB, Q_RANK, KV_RANK, DN, DR, DV = 8, 384, 256, 64, 32, 64
D_IN = HA * DA + 2 * KVA * DA + Q_RANK + KV_RANK + DR
D_FF = int(math.ceil(8 * D_MODEL / 3 / 256) * 256)
NEG = -1e30

LANE = 128
BF16_ROWS = 16
V_ROWS = DV + BF16_ROWS
LOG2E = math.log2(math.e)
SWA_Q_SCALE = DA ** -0.5 * LOG2E
MLA_Q_SCALE = (DN + DR) ** -0.5 * LOG2E
OFF_KA = HA * DA
OFF_VA = OFF_KA + KVA * DA
OFF_CQ = OFF_VA + KVA * DA
OFF_CKV = OFF_CQ + Q_RANK
OFF_KR = OFF_CKV + KV_RANK

TOK_TILE = 512
SWA_TILE = 512
MLA_KEY_BLOCK = 1024
VMEM_LIMIT = 56 * 1024 * 1024

F32 = jnp.float32
BF16 = jnp.bfloat16


def _dot(a, b):
    return jnp.dot(a, b, preferred_element_type=F32)


def _dot_tn(a, b):
    return lax.dot_general(a, b, (((0,), (0,)), ((), ())), preferred_element_type=F32)


def _dot_nt(a, b):
    return lax.dot_general(a, b, (((1,), (1,)), ((), ())), preferred_element_type=F32)


def _rope_rows(x1, x2, c, s):
    return x1 * c - x2 * s, x2 * c + x1 * s


def _proj_kernel(x_ref, gmix_ref, winT_ref, cqg_ref, wuqT_ref, ckvg_ref, wk_ref, wvT_ref,
                 cosa_ref, sina_ref, cosr_ref, sinr_ref,
                 qaT_ref, ka_ref, vaT_ref, qT_ref, k_ref, vT_ref):
    x = x_ref[0]
    ms = jnp.mean(x * x, axis=-1, keepdims=True)
    h = (x * lax.rsqrt(ms + EPS) * gmix_ref[...]).astype(BF16)
    zT = _dot_nt(winT_ref[...], h)

    ca, sa = cosa_ref[...], sina_ref[...]
    cr, sr = cosr_ref[...], sinr_ref[...]
    half = DA // 2

    qaT_ref[0] = jnp.zeros(qaT_ref.shape[1:], BF16)
    for hq in range(HA):
        r = hq * DA
        o1, o2 = _rope_rows(zT[r:r + half], zT[r + half:r + DA], ca, sa)
        base = hq * LANE + (hq // GA) * DA
        qaT_ref[0, base:base + half, :] = (o1 * SWA_Q_SCALE).astype(BF16)
        qaT_ref[0, base + half:base + DA, :] = (o2 * SWA_Q_SCALE).astype(BF16)

    kaT = []
    for g in range(KVA):
        r = OFF_KA + g * DA
        kaT.extend(_rope_rows(zT[r:r + half], zT[r + half:r + DA], ca, sa))
    ka_ref[0] = jnp.concatenate(kaT, axis=0).T.astype(BF16)
    tm = zT.shape[1]
    ones_row = lax.broadcasted_iota(jnp.int32, (V_ROWS - DV, tm), 0) == 0
    for g in range(KVA):
        r = OFF_VA + g * DA
        vaT_ref[0, g * V_ROWS:g * V_ROWS + DA, :] = zT[r:r + DA].astype(BF16)
        vaT_ref[0, g * V_ROWS + DA:(g + 1) * V_ROWS, :] = ones_row.astype(BF16)

    cq = zT[OFF_CQ:OFF_CKV]
    cqn = (cq * lax.rsqrt(jnp.mean(cq * cq, axis=0, keepdims=True) + EPS) * cqg_ref[...]).astype(BF16)
    qT = _dot(wuqT_ref[...], cqn) * MLA_Q_SCALE
    qT_ref[0, :, 0] = qT.astype(BF16).reshape(HB, LANE, tm)
    hr = DR // 2
    for hh in range(HB):
        r = hh * LANE + DN
        o1, o2 = _rope_rows(qT[r:r + hr], qT[r + hr:r + DR], cr, sr)
        qT_ref[0, hh, 0, DN:DN + hr, :] = o1.astype(BF16)
        qT_ref[0, hh, 0, DN + hr:DN + DR, :] = o2.astype(BF16)

    ckv = zT[OFF_CKV:OFF_KR]
    ckvn = (ckv * lax.rsqrt(jnp.mean(ckv * ckv, axis=0, keepdims=True) + EPS) * ckvg_ref[...]).astype(BF16)
    krT = zT[OFF_KR:D_IN]
    k1, k2 = _rope_rows(krT[:hr], krT[hr:], cr, sr)
    kin = jnp.concatenate([ckvn, k1.astype(BF16), k2.astype(BF16)], axis=0)
    k_ref[0] = _dot_tn(kin, wk_ref[...]).astype(BF16)
    vT_ref[0, :, 0, :DV] = _dot(wvT_ref[...], ckvn).astype(BF16).reshape(HB, DV, tm)
    vT_ref[0, :, 0, DV:] = jnp.broadcast_to(ones_row.astype(BF16), (HB, V_ROWS - DV, tm))


def _proj(x, p, tabs):
    B, S, _ = x.shape
    tm = TOK_TILE
    nt = S // tm
    full = lambda a: pl.BlockSpec(a.shape, lambda b, i: (0,) * a.ndim)
    tab = lambda a: pl.BlockSpec((a.shape[0], tm), lambda b, i: (0, i))
    weights = (p["g_mix"], p["w_inT"], p["cq_g"], p["w_uqT"], p["ckv_g"], p["w_k"], p["w_vT"])
    out_shape = (
        jax.ShapeDtypeStruct((B, HA * LANE, S), BF16),
        jax.ShapeDtypeStruct((B, S, KVA * DA), BF16),
        jax.ShapeDtypeStruct((B, KVA * V_ROWS, S), BF16),
        jax.ShapeDtypeStruct((B, HB, nt, LANE, tm), BF16),
        jax.ShapeDtypeStruct((B, S, HB * LANE), BF16),
        jax.ShapeDtypeStruct((B, HB, nt, V_ROWS, tm), BF16),
    )
    out_specs = (
        pl.BlockSpec((1, HA * LANE, tm), lambda b, i: (b, 0, i)),
        pl.BlockSpec((1, tm, KVA * DA), lambda b, i: (b, i, 0)),
        pl.BlockSpec((1, KVA * V_ROWS, tm), lambda b, i: (b, 0, i)),
        pl.BlockSpec((1, HB, 1, LANE, tm), lambda b, i: (b, 0, i, 0, 0)),
        pl.BlockSpec((1, tm, HB * LANE), lambda b, i: (b, i, 0)),
        pl.BlockSpec((1, HB, 1, V_ROWS, tm), lambda b, i: (b, 0, i, 0, 0)),
    )
    return pl.pallas_call(
        _proj_kernel,
        grid=(B, nt),
        in_specs=[pl.BlockSpec((1, tm, D_MODEL), lambda b, i: (b, i, 0))]
        + [full(w) for w in weights] + [tab(t) for t in tabs],
        out_specs=out_specs,
        out_shape=out_shape,
        compiler_params=pltpu.CompilerParams(
            dimension_semantics=("parallel", "parallel"), vmem_limit_bytes=VMEM_LIMIT),
        name="proj",
    )(x, *weights, *tabs)


def _swa_kernel(sink_ref, qaT_ref, kp_ref, kc_ref, kn_ref, vp_ref, vc_ref, vn_ref, oT_ref, s_ref, *, seq):
    tq = qaT_ref.shape[2]
    q0 = pl.program_id(1) * tq
    kcat = jnp.concatenate([kp_ref[0], kc_ref[0], kn_ref[0]], axis=0)
    vcat = jnp.concatenate([vp_ref[0], vc_ref[0], vn_ref[0]], axis=1)
    nk = 3 * WINDOW
    wide = GA * WINDOW
    r = lax.broadcasted_iota(jnp.int32, (nk, wide), 0)
    lane = lax.broadcasted_iota(jnp.int32, (nk, wide), 1)
    d = r - (lane & (WINDOW - 1))
    band = (d >= 0) & (d <= 2 * WINDOW)
    head = lax.broadcasted_iota(jnp.int32, (1, wide), 1) // WINDOW
    units = [(c, g) for c in range(tq // WINDOW) for g in range(KVA)]

    def scores(slot, c, g):
        q4 = jnp.concatenate(
            [qaT_ref[0, (g * GA + h) * LANE:(g * GA + h + 1) * LANE, c * WINDOW:(c + 1) * WINDOW]
             for h in range(GA)], axis=1)
        valid = band
        if c in (0, tq // WINDOW - 1):
            kpos = q0 + (c - 1) * WINDOW + r
            valid = band & (kpos >= 0) & (kpos < seq)
        s = jnp.where(valid, _dot(kcat[c * WINDOW:c * WINDOW + nk], q4), NEG)
        s_ref[slot] = s
        return jnp.max(s, axis=0, keepdims=True)

    def finish(slot, c, g, cmax):
        sink = jnp.zeros((1, wide), F32)
        for h in range(GA):
            sink = jnp.where(head == h, sink_ref[g * GA + h] * LOG2E, sink)
        m = jnp.maximum(cmax, sink)
        p = jnp.exp2(s_ref[slot] - m).astype(BF16)
        pv = _dot(vcat[g * V_ROWS:(g + 1) * V_ROWS, c * WINDOW:c * WINDOW + nk], p)
        o = (pv[:DA] / (pv[DA:DA + 1] + jnp.exp2(sink - m))).astype(BF16)
        for h in range(GA):
            hq = g * GA + h
            oT_ref[0, hq * DA:(hq + 1) * DA, c * WINDOW:(c + 1) * WINDOW] = o[:, h * WINDOW:(h + 1) * WINDOW]

    cmax = scores(0, *units[0])
    for u, (c, g) in enumerate(units):
        nxt = scores((u + 1) % 2, *units[u + 1]) if u + 1 < len(units) else None
        finish(u % 2, c, g, cmax)
        cmax = nxt


def _swa(sink, qaT, ka, vaT):
    B, _, S = qaT.shape
    tq = SWA_TILE
    r = tq // WINDOW
    nb = S // WINDOW
    prev = lambda b, i, *_: (b, jnp.maximum(i * r - 1, 0), 0)
    nxt = lambda b, i, *_: (b, jnp.minimum(i * r + r, nb - 1), 0)
    prev_t = lambda b, i, *_: (b, 0, jnp.maximum(i * r - 1, 0))
    nxt_t = lambda b, i, *_: (b, 0, jnp.minimum(i * r + r, nb - 1))
    grid_spec = pltpu.PrefetchScalarGridSpec(
        num_scalar_prefetch=1,
        grid=(B, S // tq),
        in_specs=[
            pl.BlockSpec((1, HA * LANE, tq), lambda b, i, *_: (b, 0, i)),
            pl.BlockSpec((1, WINDOW, KVA * DA), prev),
            pl.BlockSpec((1, tq, KVA * DA), lambda b, i, *_: (b, i, 0)),
            pl.BlockSpec((1, WINDOW, KVA * DA), nxt),
            pl.BlockSpec((1, KVA * V_ROWS, WINDOW), prev_t),
            pl.BlockSpec((1, KVA * V_ROWS, tq), lambda b, i, *_: (b, 0, i)),
            pl.BlockSpec((1, KVA * V_ROWS, WINDOW), nxt_t),
        ],
        out_specs=pl.BlockSpec((1, HA * DA, tq), lambda b, i, *_: (b, 0, i)),
        scratch_shapes=[pltpu.VMEM((2, 3 * WINDOW, GA * WINDOW), F32)],
    )
    return pl.pallas_call(
        functools.partial(_swa_kernel, seq=S),
        grid_spec=grid_spec,
        out_shape=jax.ShapeDtypeStruct((B, HA * DA, S), BF16),
        compiler_params=pltpu.CompilerParams(
            dimension_semantics=("parallel", "parallel"), vmem_limit_bytes=VMEM_LIMIT),
        name="swa",
    )(sink, qaT, ka, ka, ka, vaT, vaT, vaT)


def _mla_kernel(qT_ref, k_ref, vT_ref, oT_ref, s_ref):
    nq, _, tq = qT_ref.shape[2:]
    nvb, nv, tv = vT_ref.shape[2:]
    tk = s_ref.shape[1]
    nkb = nvb * tv // tk
    vper = tk // tv

    def scores(slot, qi, j):
        kb = k_ref[0, pl.ds(pl.multiple_of(j * tk, tk), tk), :]
        s = _dot(kb, qT_ref[0, 0, qi])
        s_ref[slot] = s
        return jnp.max(s, axis=0, keepdims=True)

    def softmax_pv(slot, j, cmax, m, acc):
        m_new = jnp.maximum(m, cmax)
        alpha = jnp.exp2(m - m_new)
        p = jnp.exp2(s_ref[slot] - m_new).astype(BF16)
        pv = _dot(vT_ref[0, 0, j * vper], p[:tv])
        for c in range(1, vper):
            pv += _dot(vT_ref[0, 0, j * vper + c], p[c * tv:(c + 1) * tv])
        return m_new, alpha * acc + pv

    def q_tile(qi, cmax0):
        def pair(jj, carry):
            cmax0, m, acc = carry
            j = 2 * jj
            cmax1 = scores(1, qi, j + 1)
            m, acc = softmax_pv(0, j, cmax0, m, acc)
            cmax0 = scores(0, qi, j + 2)
            m, acc = softmax_pv(1, j + 1, cmax1, m, acc)
            return cmax0, m, acc

        init = (cmax0, jnp.full((1, tq), -jnp.inf, F32), jnp.zeros((nv, tq), F32))
        cmax0, m, acc = lax.fori_loop(0, nkb // 2 - 1, pair, init)
        cmax1 = scores(1, qi, nkb - 1)
        m, acc = softmax_pv(0, nkb - 2, cmax0, m, acc)
        cmax0 = scores(0, jnp.minimum(qi + 1, nq - 1), 0)
        m, acc = softmax_pv(1, nkb - 1, cmax1, m, acc)
        oT_ref[0, 0, qi] = (acc[:DV] / acc[DV:DV + 1]).astype(BF16)
        return cmax0

    lax.fori_loop(0, nq, q_tile, scores(0, 0, 0))


def _mla(qT, k, vT):
    B, _, nq, _, tq = qT.shape
    nvb, nv, tv = vT.shape[2:]
    S = nq * tq
    tk = MLA_KEY_BLOCK
    assert tk % tv == 0 and S % (2 * tk) == 0
    return pl.pallas_call(
        _mla_kernel,
        grid=(B, HB),
        in_specs=[
            pl.BlockSpec((1, 1, nq, LANE, tq), lambda b, h: (b, h, 0, 0, 0)),
            pl.BlockSpec((1, S, LANE), lambda b, h: (b, 0, h)),
            pl.BlockSpec((1, 1, nvb, nv, tv), lambda b, h: (b, h, 0, 0, 0)),
        ],
        out_specs=pl.BlockSpec((1, 1, nq, DV, tq), lambda b, h: (b, h, 0, 0, 0)),
        out_shape=jax.ShapeDtypeStruct((B, HB, nq, DV, tq), BF16),
        scratch_shapes=[pltpu.VMEM((2, tk, tq), F32)],
        compiler_params=pltpu.CompilerParams(
            dimension_semantics=("parallel", "parallel"), vmem_limit_bytes=VMEM_LIMIT),
        name="mla",
    )(qT, k, vT)


def _ffn_kernel(x_ref, oaT_ref, obT_ref, woa_ref, wob_ref, gffn_ref, wg_ref, wu_ref, wd_ref, gfin_ref, y_ref):
    x = x_ref[0]
    obT = obT_ref[0, :, 0].reshape(HB * DV, -1)
    x = x + _dot_tn(oaT_ref[0], woa_ref[...]) + _dot_tn(obT, wob_ref[...])
    h = (x * lax.rsqrt(jnp.mean(x * x, axis=-1, keepdims=True) + EPS) * gffn_ref[...]).astype(BF16)
    g = _dot(h, wg_ref[...])
    u = _dot(h, wu_ref[...])
    a = (g / (1.0 + jnp.exp(-g)) * u).astype(BF16)
    x = x + _dot(a, wd_ref[...])
    y_ref[0] = x * lax.rsqrt(jnp.mean(x * x, axis=-1, keepdims=True) + EPS) * gfin_ref[...]


def _ffn(x, oaT, obT, p):
    B, S, _ = x.shape
    tm = TOK_TILE
    full = lambda a: pl.BlockSpec(a.shape, lambda b, i: (0,) * a.ndim, pipeline_mode=pl.Buffered(1))
    weights = (p["w_oa"], p["w_ob"], p["g_ffn"], p["w_gate"], p["w_up"], p["w_down"], p["g_final"])
    return pl.pallas_call(
        _ffn_kernel,
        grid=(B, S // tm),
        in_specs=[
            pl.BlockSpec((1, tm, D_MODEL), lambda b, i: (b, i, 0)),
            pl.BlockSpec((1, HA * DA, tm), lambda b, i: (b, 0, i)),
            pl.BlockSpec((1, HB, 1, DV, tm), lambda b, i: (b, 0, i, 0, 0)),
        ] + [full(w) for w in weights],
        out_specs=pl.BlockSpec((1, tm, D_MODEL), lambda b, i: (b, i, 0)),
        out_shape=jax.ShapeDtypeStruct(x.shape, F32),
        compiler_params=pltpu.CompilerParams(
            dimension_semantics=("parallel", "parallel"), vmem_limit_bytes=VMEM_LIMIT),
        name="ffn",
    )(x, oaT, obT, *weights)


def _rope_tables_t(seq, dim):
    inv = 1.0 / (ROPE_THETA ** (jnp.arange(0, dim, 2, dtype=F32) / dim))
    ang = jnp.arange(seq, dtype=F32)[:, None] * inv[None, :]
    return jnp.cos(ang).T, jnp.sin(ang).T


def _prep_params(g_mix, w_in, sink, cq_g, w_uq, ckv_g, w_ukv, w_o, g_ffn, w_gate, w_up, w_down, g_final):
    w_uq3 = w_uq.reshape(Q_RANK, HB, DN + DR)
    w_uq_pad = jnp.pad(w_uq3, ((0, 0), (0, 0), (0, LANE - DN - DR))).reshape(Q_RANK, HB * LANE)
    w_ukv3 = w_ukv.reshape(KV_RANK, HB, DN + DV)
    w_uk, w_uv = w_ukv3[..., :DN], w_ukv3[..., DN:]
    top = jnp.pad(w_uk, ((0, 0), (0, 0), (0, LANE - DN)))
    eye = jnp.pad(jnp.eye(DR, dtype=F32)[:, None, :], ((0, 0), (0, 0), (DN, LANE - DN - DR)))
    bot = jnp.broadcast_to(eye, (DR, HB, LANE))
    w_k = jnp.concatenate([top, bot], axis=0).reshape(KV_RANK + DR, HB * LANE)
    return {
        "g_mix": g_mix.reshape(1, D_MODEL),
        "w_inT": w_in.T.astype(BF16),
        "cq_g": cq_g.reshape(Q_RANK, 1),
        "w_uqT": w_uq_pad.T.astype(BF16),
        "ckv_g": ckv_g.reshape(KV_RANK, 1),
        "w_k": w_k.astype(BF16),
        "w_vT": w_uv.reshape(KV_RANK, HB * DV).T.astype(BF16),
        "sink": sink,
        "w_oa": w_o[:HA * DA].astype(BF16),
        "w_ob": w_o[HA * DA:].astype(BF16),
        "g_ffn": g_ffn.reshape(1, D_MODEL),
        "w_gate": w_gate.astype(BF16),
        "w_up": w_up.astype(BF16),
        "w_down": w_down.astype(BF16),
        "g_final": g_final.reshape(1, D_MODEL),
    }


def _trunk(x, p):
    S = x.shape[1]
    tabs = _rope_tables_t(S, DA) + _rope_tables_t(S, DR)
    qaT, ka, vaT, qT, k, vT = _proj(x, p, tabs)
    oaT = _swa(p["sink"], qaT, ka, vaT)
    obT = _mla(qT, k, vT)
    return _ffn(x, oaT, obT, p)


def kernel(x_prompt, x_sample, g_mix, w_in, sink, cq_g, w_uq, ckv_g, w_ukv, w_o, g_ffn,
           w_gate, w_up, w_down, g_final):
    p = _prep_params(g_mix[0], w_in[0], sink[0], cq_g[0], w_uq[0], ckv_g[0], w_ukv[0], w_o[0],
                     g_ffn[0], w_gate[0], w_up[0], w_down[0], g_final)
    return _trunk(x_prompt, p), _trunk(x_sample, p)
```

```python
import functools
import math

import jax
import jax.numpy as jnp
from jax import lax
from jax.experimental import pallas as pl
from jax.experimental.pallas import tpu as pltpu

D_MODEL = 1024
EPS = 1e-6
ROPE_THETA = 10000.0
WINDOW = 128
HA, KVA, DA = 8, 2, 64
GA = HA // KVA
HB, Q_RANK, KV_RANK, DN, DR, DV = 8, 384, 256, 64, 32, 64
D_IN = HA * DA + 2 * KVA * DA + Q_RANK + KV_RANK + DR
D_FF = int(math.ceil(8 * D_MODEL / 3 / 256) * 256)
NEG = -1e30

LANE = 128
BF16_ROWS = 16
V_ROWS = DV + BF16_ROWS
LOG2E = math.log2(math.e)
SWA_Q_SCALE = DA ** -0.5 * LOG2E
MLA_Q_SCALE = (DN + DR) ** -0.5 * LOG2E
OFF_KA = HA * DA
OFF_VA = OFF_KA + KVA * DA
OFF_CQ = OFF_VA + KVA * DA
OFF_CKV = OFF_CQ + Q_RANK
OFF_KR = OFF_CKV + KV_RANK

TOK_TILE = 512
SWA_TILE = 512
MLA_KEY_BLOCK = 1024
VMEM_LIMIT = 56 * 1024 * 1024

F32 = jnp.float32
BF16 = jnp.bfloat16


def _dot(a, b):
    return jnp.dot(a, b, preferred_element_type=F32)


def _dot_tn(a, b):
    return lax.dot_general(a, b, (((0,), (0,)), ((), ())), preferred_element_type=F32)


def _dot_nt(a, b):
    return lax.dot_general(a, b, (((1,), (1,)), ((), ())), preferred_element_type=F32)


def _rope_rows(x1, x2, c, s):
    return x1 * c - x2 * s, x2 * c + x1 * s


def _proj_kernel(x_ref, gmix_ref, winT_ref, cqg_ref, wuqT_ref, ckvg_ref, wk_ref, wvT_ref,
                 cosa_ref, sina_ref, cosr_ref, sinr_ref,
                 qaT_ref, ka_ref, vaT_ref, qT_ref, k_ref, vT_ref):
    x = x_ref[0]
    ms = jnp.mean(x * x, axis=-1, keepdims=True)
    h = (x * lax.rsqrt(ms + EPS) * gmix_ref[...]).astype(BF16)
    zT = _dot_nt(winT_ref[...], h)

    ca, sa = cosa_ref[...], sina_ref[...]
    cr, sr = cosr_ref[...], sinr_ref[...]
    half = DA // 2

    qaT_ref[0] = jnp.zeros(qaT_ref.shape[1:], BF16)
    for hq in range(HA):
        r = hq * DA
        o1, o2 = _rope_rows(zT[r:r + half], zT[r + half:r + DA], ca, sa)
        base = hq * LANE + (hq // GA) * DA
        qaT_ref[0, base:base + half, :] = (o1 * SWA_Q_SCALE).astype(BF16)
        qaT_ref[0, base + half:base + DA, :] = (o2 * SWA_Q_SCALE).astype(BF16)

    kaT = []
    for g in range(KVA):
        r = OFF_KA + g * DA
        kaT.extend(_rope_rows(zT[r:r + half], zT[r + half:r + DA], ca, sa))
    ka_ref[0] = jnp.concatenate(kaT, axis=0).T.astype(BF16)
    tm = zT.shape[1]
    ones_row = lax.broadcasted_iota(jnp.int32, (V_ROWS - DV, tm), 0) == 0
    for g in range(KVA):
        r = OFF_VA + g * DA
        vaT_ref[0, g * V_ROWS:g * V_ROWS + DA, :] = zT[r:r + DA].astype(BF16)
        vaT_ref[0, g * V_ROWS + DA:(g + 1) * V_ROWS, :] = ones_row.astype(BF16)

    cq = zT[OFF_CQ:OFF_CKV]
    cqn = (cq * lax.rsqrt(jnp.mean(cq * cq, axis=0, keepdims=True) + EPS) * cqg_ref[...]).astype(BF16)
    qT = _dot(wuqT_ref[...], cqn) * MLA_Q_SCALE
    qT_ref[0, :, 0] = qT.astype(BF16).reshape(HB, LANE, tm)
    hr = DR // 2
    for hh in range(HB):
        r = hh * LANE + DN
        o1, o2 = _rope_rows(qT[r:r + hr], qT[r + hr:r + DR], cr, sr)
        qT_ref[0, hh, 0, DN:DN + hr, :] = o1.astype(BF16)
        qT_ref[0, hh, 0, DN + hr:DN + DR, :] = o2.astype(BF16)

    ckv = zT[OFF_CKV:OFF_KR]
    ckvn = (ckv * lax.rsqrt(jnp.mean(ckv * ckv, axis=0, keepdims=True) + EPS) * ckvg_ref[...]).astype(BF16)
    krT = zT[OFF_KR:D_IN]
    k1, k2 = _rope_rows(krT[:hr], krT[hr:], cr, sr)
    kin = jnp.concatenate([ckvn, k1.astype(BF16), k2.astype(BF16)], axis=0)
    k_ref[0] = _dot_tn(kin, wk_ref[...]).astype(BF16)
    vT_ref[0, :, 0, :DV] = _dot(wvT_ref[...], ckvn).astype(BF16).reshape(HB, DV, tm)
    vT_ref[0, :, 0, DV:] = jnp.broadcast_to(ones_row.astype(BF16), (HB, V_ROWS - DV, tm))


def _proj(x, p, tabs):
    B, S, _ = x.shape
    tm = TOK_TILE
    nt = S // tm
    full = lambda a: pl.BlockSpec(a.shape, lambda b, i: (0,) * a.ndim)
    tab = lambda a: pl.BlockSpec((a.shape[0], tm), lambda b, i: (0, i))
    weights = (p["g_mix"], p["w_inT"], p["cq_g"], p["w_uqT"], p["ckv_g"], p["w_k"], p["w_vT"])
    out_shape = (
        jax.ShapeDtypeStruct((B, HA * LANE, S), BF16),
        jax.ShapeDtypeStruct((B, S, KVA * DA), BF16),
        jax.ShapeDtypeStruct((B, KVA * V_ROWS, S), BF16),
        jax.ShapeDtypeStruct((B, HB, nt, LANE, tm), BF16),
        jax.ShapeDtypeStruct((B, S, HB * LANE), BF16),
        jax.ShapeDtypeStruct((B, HB, nt, V_ROWS, tm), BF16),
    )
    out_specs = (
        pl.BlockSpec((1, HA * LANE, tm), lambda b, i: (b, 0, i)),
        pl.BlockSpec((1, tm, KVA * DA), lambda b, i: (b, i, 0)),
        pl.BlockSpec((1, KVA * V_ROWS, tm), lambda b, i: (b, 0, i)),
        pl.BlockSpec((1, HB, 1, LANE, tm), lambda b, i: (b, 0, i, 0, 0)),
        pl.BlockSpec((1, tm, HB * LANE), lambda b, i: (b, i, 0)),
        pl.BlockSpec((1, HB, 1, V_ROWS, tm), lambda b, i: (b, 0, i, 0, 0)),
    )
    return pl.pallas_call(
        _proj_kernel,
        grid=(B, nt),
        in_specs=[pl.BlockSpec((1, tm, D_MODEL), lambda b, i: (b, i, 0))]
        + [full(w) for w in weights] + [tab(t) for t in tabs],
        out_specs=out_specs,
        out_shape=out_shape,
        compiler_params=pltpu.CompilerParams(
            dimension_semantics=("parallel", "parallel"), vmem_limit_bytes=VMEM_LIMIT),
        name="proj",
    )(x, *weights, *tabs)


def _swa_kernel(sink_ref, qaT_ref, kp_ref, kc_ref, kn_ref, vp_ref, vc_ref, vn_ref, oT_ref, s_ref, *, seq):
    tq = qaT_ref.shape[2]
    q0 = pl.program_id(1) * tq
    kcat = jnp.concatenate([kp_ref[0], kc_ref[0], kn_ref[0]], axis=0)
    vcat = jnp.concatenate([vp_ref[0], vc_ref[0], vn_ref[0]], axis=1)
    nk = 3 * WINDOW
    wide = GA * WINDOW
    r = lax.broadcasted_iota(jnp.int32, (nk, wide), 0)
    lane = lax.broadcasted_iota(jnp.int32, (nk, wide), 1)
    d = r - (lane & (WINDOW - 1))
    band = (d >= 0) & (d <= 2 * WINDOW)
    head = lax.broadcasted_iota(jnp.int32, (1, wide), 1) // WINDOW
    units = [(c, g) for c in range(tq // WINDOW) for g in range(KVA)]

    def scores(slot, c, g):
        q4 = jnp.concatenate(
            [qaT_ref[0, (g * GA + h) * LANE:(g * GA + h + 1) * LANE, c * WINDOW:(c + 1) * WINDOW]
             for h in range(GA)], axis=1)
        valid = band
        if c in (0, tq // WINDOW - 1):
            kpos = q0 + (c - 1) * WINDOW + r
            valid = band & (kpos >= 0) & (kpos < seq)
        s = jnp.where(valid, _dot(kcat[c * WINDOW:c * WINDOW + nk], q4), NEG)
        s_ref[slot] = s
        return jnp.max(s, axis=0, keepdims=True)

    def finish(slot, c, g, cmax):
        sink = jnp.zeros((1, wide), F32)
        for h in range(GA):
            sink = jnp.where(head == h, sink_ref[g * GA + h] * LOG2E, sink)
        m = jnp.maximum(cmax, sink)
        p = jnp.exp2(s_ref[slot] - m).astype(BF16)
        pv = _dot(vcat[g * V_ROWS:(g + 1) * V_ROWS, c * WINDOW:c * WINDOW + nk], p)
        o = (pv[:DA] / (pv[DA:DA + 1] + jnp.exp2(sink - m))).astype(BF16)
        for h in range(GA):
            hq = g * GA + h
            oT_ref[0, hq * DA:(hq + 1) * DA, c * WINDOW:(c + 1) * WINDOW] = o[:, h * WINDOW:(h + 1) * WINDOW]

    cmax = scores(0, *units[0])
    for u, (c, g) in enumerate(units):
        nxt = scores((u + 1) % 2, *units[u + 1]) if u + 1 < len(units) else None
        finish(u % 2, c, g, cmax)
        cmax = nxt


def _swa(sink, qaT, ka, vaT):
    B, _, S = qaT.shape
    tq = SWA_TILE
    r = tq // WINDOW
    nb = S // WINDOW
    prev = lambda b, i, *_: (b, jnp.maximum(i * r - 1, 0), 0)
    nxt = lambda b, i, *_: (b, jnp.minimum(i * r + r, nb - 1), 0)
    prev_t = lambda b, i, *_: (b, 0, jnp.maximum(i * r - 1, 0))
    nxt_t = lambda b, i, *_: (b, 0, jnp.minimum(i * r + r, nb - 1))
    grid_spec = pltpu.PrefetchScalarGridSpec(
        num_scalar_prefetch=1,
        grid=(B, S // tq),
        in_specs=[
            pl.BlockSpec((1, HA * LANE, tq), lambda b, i, *_: (b, 0, i)),
            pl.BlockSpec((1, WINDOW, KVA * DA), prev),
            pl.BlockSpec((1, tq, KVA * DA), lambda b, i, *_: (b, i, 0)),
            pl.BlockSpec((1, WINDOW, KVA * DA), nxt),
            pl.BlockSpec((1, KVA * V_ROWS, WINDOW), prev_t),
            pl.BlockSpec((1, KVA * V_ROWS, tq), lambda b, i, *_: (b, 0, i)),
            pl.BlockSpec((1, KVA * V_ROWS, WINDOW), nxt_t),
        ],
        out_specs=pl.BlockSpec((1, HA * DA, tq), lambda b, i, *_: (b, 0, i)),
        scratch_shapes=[pltpu.VMEM((2, 3 * WINDOW, GA * WINDOW), F32)],
    )
    return pl.pallas_call(
        functools.partial(_swa_kernel, seq=S),
        grid_spec=grid_spec,
        out_shape=jax.ShapeDtypeStruct((B, HA * DA, S), BF16),
        compiler_params=pltpu.CompilerParams(
            dimension_semantics=("parallel", "parallel"), vmem_limit_bytes=VMEM_LIMIT),
        name="swa",
    )(sink, qaT, ka, ka, ka, vaT, vaT, vaT)


def _mla_kernel(qT_ref, k_ref, vT_ref, oT_ref, s_ref):
    nq, _, tq = qT_ref.shape[2:]
    nvb, nv, tv = vT_ref.shape[2:]
    tk = s_ref.shape[1]
    nkb = nvb * tv // tk
    vper = tk // tv

    def scores(slot, qi, j):
        kb = k_ref[0, j * tk:(j + 1) * tk, :]
        s = _dot(kb, qT_ref[0, 0, qi])
        s_ref[slot] = s
        return jnp.max(s, axis=0, keepdims=True)

    def softmax_pv(slot, j, cmax, m, acc):
        m_new = jnp.maximum(m, cmax)
        alpha = jnp.exp2(m - m_new)
        p = jnp.exp2(s_ref[slot] - m_new).astype(BF16)
        pv = _dot(vT_ref[0, 0, j * vper], p[:tv])
        for c in range(1, vper):
            pv += _dot(vT_ref[0, 0, j * vper + c], p[c * tv:(c + 1) * tv])
        return m_new, alpha * acc + pv

    def q_tile(qi, cmax):
        m, acc = jnp.full((1, tq), -jnp.inf, F32), jnp.zeros((nv, tq), F32)
        for j in range(nkb):
            if j + 1 < nkb:
                nxt = scores((j + 1) % 2, qi, j + 1)
            else:
                nxt = scores(0, jnp.minimum(qi + 1, nq - 1), 0)
            m, acc = softmax_pv(j % 2, j, cmax, m, acc)
            cmax = nxt
        oT_ref[0, 0, qi] = (acc[:DV] / acc[DV:DV + 1]).astype(BF16)
        return cmax

    lax.fori_loop(0, nq, q_tile, scores(0, 0, 0))


def _mla(qT, k, vT):
    B, _, nq, _, tq = qT.shape
    nvb, nv, tv = vT.shape[2:]
    S = nq * tq
    tk = MLA_KEY_BLOCK
    assert tk % tv == 0 and S % (2 * tk) == 0
    return pl.pallas_call(
        _mla_kernel,
        grid=(B, HB),
        in_specs=[
            pl.BlockSpec((1, 1, nq, LANE, tq), lambda b, h: (b, h, 0, 0, 0)),
            pl.BlockSpec((1, S, LANE), lambda b, h: (b, 0, h)),
            pl.BlockSpec((1, 1, nvb, nv, tv), lambda b, h: (b, h, 0, 0, 0)),
        ],
        out_specs=pl.BlockSpec((1, 1, nq, DV, tq), lambda b, h: (b, h, 0, 0, 0)),
        out_shape=jax.ShapeDtypeStruct((B, HB, nq, DV, tq), BF16),
        scratch_shapes=[pltpu.VMEM((2, tk, tq), F32)],
        compiler_params=pltpu.CompilerParams(
            dimension_semantics=("parallel", "parallel"), vmem_limit_bytes=VMEM_LIMIT),
        name="mla",
    )(qT, k, vT)


def _ffn_kernel(x_ref, oaT_ref, obT_ref, woa_ref, wob_ref, gffn_ref, wg_ref, wu_ref, wd_ref, gfin_ref, y_ref):
    x = x_ref[0]
    obT = obT_ref[0, :, 0].reshape(HB * DV, -1)
    x = x + _dot_tn(oaT_ref[0], woa_ref[...]) + _dot_tn(obT, wob_ref[...])
    h = (x * lax.rsqrt(jnp.mean(x * x, axis=-1, keepdims=True) + EPS) * gffn_ref[...]).astype(BF16)
    g = _dot(h, wg_ref[...])
    u = _dot(h, wu_ref[...])
    a = (g / (1.0 + jnp.exp(-g)) * u).astype(BF16)
    x = x + _dot(a, wd_ref[...])
    y_ref[0] = x * lax.rsqrt(jnp.mean(x * x, axis=-1, keepdims=True) + EPS) * gfin_ref[...]


def _ffn(x, oaT, obT, p):
    B, S, _ = x.shape
    tm = TOK_TILE
    full = lambda a: pl.BlockSpec(a.shape, lambda b, i: (0,) * a.ndim, pipeline_mode=pl.Buffered(1))
    weights = (p["w_oa"], p["w_ob"], p["g_ffn"], p["w_gate"], p["w_up"], p["w_down"], p["g_final"])
    return pl.pallas_call(
        _ffn_kernel,
        grid=(B, S // tm),
        in_specs=[
            pl.BlockSpec((1, tm, D_MODEL), lambda b, i: (b, i, 0)),
            pl.BlockSpec((1, HA * DA, tm), lambda b, i: (b, 0, i)),
            pl.BlockSpec((1, HB, 1, DV, tm), lambda b, i: (b, 0, i, 0, 0)),
        ] + [full(w) for w in weights],
        out_specs=pl.BlockSpec((1, tm, D_MODEL), lambda b, i: (b, i, 0)),
        out_shape=jax.ShapeDtypeStruct(x.shape, F32),
        compiler_params=pltpu.CompilerParams(
            dimension_semantics=("parallel", "parallel"), vmem_limit_bytes=VMEM_LIMIT),
        name="ffn",
    )(x, oaT, obT, *weights)


def _rope_tables_t(seq, dim):
    inv = 1.0 / (ROPE_THETA ** (jnp.arange(0, dim, 2, dtype=F32) / dim))
    ang = jnp.arange(seq, dtype=F32)[:, None] * inv[None, :]
    return jnp.cos(ang).T, jnp.sin(ang).T


def _prep_params(g_mix, w_in, sink, cq_g, w_uq, ckv_g, w_ukv, w_o, g_ffn, w_gate, w_up, w_down, g_final):
    w_uq3 = w_uq.reshape(Q_RANK, HB, DN + DR)
    w_uq_pad = jnp.pad(w_uq3, ((0, 0), (0, 0), (0, LANE - DN - DR))).reshape(Q_RANK, HB * LANE)
    w_ukv3 = w_ukv.reshape(KV_RANK, HB, DN + DV)
    w_uk, w_uv = w_ukv3[..., :DN], w_ukv3[..., DN:]
    top = jnp.pad(w_uk, ((0, 0), (0, 0), (0, LANE - DN)))
    eye = jnp.pad(jnp.eye(DR, dtype=F32)[:, None, :], ((0, 0), (0, 0), (DN, LANE - DN - DR)))
    bot = jnp.broadcast_to(eye, (DR, HB, LANE))
    w_k = jnp.concatenate([top, bot], axis=0).reshape(KV_RANK + DR, HB * LANE)
    return {
        "g_mix": g_mix.reshape(1, D_MODEL),
        "w_inT": w_in.T.astype(BF16),
        "cq_g": cq_g.reshape(Q_RANK, 1),
        "w_uqT": w_uq_pad.T.astype(BF16),
        "ckv_g": ckv_g.reshape(KV_RANK, 1),
        "w_k": w_k.astype(BF16),
        "w_vT": w_uv.reshape(KV_RANK, HB * DV).T.astype(BF16),
        "sink": sink,
        "w_oa": w_o[:HA * DA].astype(BF16),
        "w_ob": w_o[HA * DA:].astype(BF16),
        "g_ffn": g_ffn.reshape(1, D_MODEL),
        "w_gate": w_gate.astype(BF16),
        "w_up": w_up.astype(BF16),
        "w_down": w_down.astype(BF16),
        "g_final": g_final.reshape(1, D_MODEL),
    }


def _trunk(x, p):
    S = x.shape[1]
    tabs = _rope_tables_t(S, DA) + _rope_tables_t(S, DR)
    qaT, ka, vaT, qT, k, vT = _proj(x, p, tabs)
    oaT = _swa(p["sink"], qaT, ka, vaT)
    obT = _mla(qT, k, vT)
    return _ffn(x, oaT, obT, p)


def kernel(x_prompt, x_sample, g_mix, w_in, sink, cq_g, w_uq, ckv_g, w_ukv, w_o, g_ffn,
           w_gate, w_up, w_down, g_final):
    p = _prep_params(g_mix[0], w_in[0], sink[0], cq_g[0], w_uq[0], ckv_g[0], w_ukv[0], w_o[0],
                     g_ffn[0], w_gate[0], w_up[0], w_down[0], g_final)
    return _trunk(x_prompt, p), _trunk(x_sample, p)
```

```python
import functools
import math

import jax
import jax.numpy as jnp
from jax import lax
from jax.experimental import pallas as pl
from jax.experimental.pallas import tpu as pltpu

D_MODEL = 1024
EPS = 1e-6
ROPE_THETA = 10000.0
WINDOW = 128
HA, KVA, DA = 8, 2, 64
GA = HA // KVA
HB, Q_RANK, KV_RANK, DN, DR, DV = 8, 384, 256, 64, 32, 64
D_IN = HA * DA + 2 * KVA * DA + Q_RANK + KV_RANK + DR
D_FF = int(math.ceil(8 * D_MODEL / 3 / 256) * 256)
NEG = -1e30

LANE = 128
V_ROWS = 128
LOG2E = math.log2(math.e)
SWA_Q_SCALE = DA ** -0.5 * LOG2E
MLA_Q_SCALE = (DN + DR) ** -0.5 * LOG2E
OFF_KA = HA * DA
OFF_VA = OFF_KA + KVA * DA
OFF_CQ = OFF_VA + KVA * DA
OFF_CKV = OFF_CQ + Q_RANK
OFF_KR = OFF_CKV + KV_RANK

TOK_TILE = 512
SWA_TILE = 512
MLA_KEY_BLOCK = 1024
VMEM_LIMIT = 56 * 1024 * 1024

F32 = jnp.float32
BF16 = jnp.bfloat16


def _dot(a, b):
    return jnp.dot(a, b, preferred_element_type=F32)


def _dot_tn(a, b):
    return lax.dot_general(a, b, (((0,), (0,)), ((), ())), preferred_element_type=F32)


def _dot_nt(a, b):
    return lax.dot_general(a, b, (((1,), (1,)), ((), ())), preferred_element_type=F32)


def _rope_rows(x1, x2, c, s):
    return x1 * c - x2 * s, x2 * c + x1 * s


def _proj_kernel(x_ref, gmix_ref, winT_ref, cqg_ref, wuqT_ref, ckvg_ref, wk_ref, wvT_ref,
                 cosa_ref, sina_ref, cosr_ref, sinr_ref,
                 qaT_ref, ka_ref, vaT_ref, qT_ref, k_ref, vT_ref):
    x = x_ref[0]
    ms = jnp.mean(x * x, axis=-1, keepdims=True)
    h = (x * lax.rsqrt(ms + EPS) * gmix_ref[...]).astype(BF16)
    zT = _dot_nt(winT_ref[...], h)

    ca, sa = cosa_ref[...], sina_ref[...]
    cr, sr = cosr_ref[...], sinr_ref[...]
    half = DA // 2

    qaT_ref[0] = jnp.zeros(qaT_ref.shape[1:], BF16)
    for hq in range(HA):
        r = hq * DA
        o1, o2 = _rope_rows(zT[r:r + half], zT[r + half:r + DA], ca, sa)
        base = hq * LANE + (hq // GA) * DA
        qaT_ref[0, base:base + half, :] = (o1 * SWA_Q_SCALE).astype(BF16)
        qaT_ref[0, base + half:base + DA, :] = (o2 * SWA_Q_SCALE).astype(BF16)

    kaT = []
    for g in range(KVA):
        r = OFF_KA + g * DA
        kaT.extend(_rope_rows(zT[r:r + half], zT[r + half:r + DA], ca, sa))
    ka_ref[0] = jnp.concatenate(kaT, axis=0).T.astype(BF16)
    tm = zT.shape[1]
    ones_row = lax.broadcasted_iota(jnp.int32, (V_ROWS - DV, tm), 0) == 0
    for g in range(KVA):
        r = OFF_VA + g * DA
        vaT_ref[0, g * V_ROWS:g * V_ROWS + DA, :] = zT[r:r + DA].astype(BF16)
        vaT_ref[0, g * V_ROWS + DA:(g + 1) * V_ROWS, :] = ones_row.astype(BF16)

    cq = zT[OFF_CQ:OFF_CKV]
    cqn = (cq * lax.rsqrt(jnp.mean(cq * cq, axis=0, keepdims=True) + EPS) * cqg_ref[...]).astype(BF16)
    qT = _dot(wuqT_ref[...], cqn) * MLA_Q_SCALE
    qT_ref[0, :, 0] = qT.astype(BF16).reshape(HB, LANE, tm)
    hr = DR // 2
    for hh in range(HB):
        r = hh * LANE + DN
        o1, o2 = _rope_rows(qT[r:r + hr], qT[r + hr:r + DR], cr, sr)
        qT_ref[0, hh, 0, DN:DN + hr, :] = o1.astype(BF16)
        qT_ref[0, hh, 0, DN + hr:DN + DR, :] = o2.astype(BF16)

    ckv = zT[OFF_CKV:OFF_KR]
    ckvn = (ckv * lax.rsqrt(jnp.mean(ckv * ckv, axis=0, keepdims=True) + EPS) * ckvg_ref[...]).astype(BF16)
    krT = zT[OFF_KR:D_IN]
    k1, k2 = _rope_rows(krT[:hr], krT[hr:], cr, sr)
    kin = jnp.concatenate([ckvn, k1.astype(BF16), k2.astype(BF16)], axis=0)
    k_ref[0] = _dot_tn(kin, wk_ref[...]).astype(BF16)
    vT_ref[0, :, 0, :DV] = _dot(wvT_ref[...], ckvn).astype(BF16).reshape(HB, DV, tm)
    vT_ref[0, :, 0, DV:] = jnp.broadcast_to(ones_row.astype(BF16), (HB, V_ROWS - DV, tm))


def _proj(x, p, tabs):
    B, S, _ = x.shape
    tm = TOK_TILE
    nt = S // tm
    full = lambda a: pl.BlockSpec(a.shape, lambda b, i: (0,) * a.ndim)
    tab = lambda a: pl.BlockSpec((a.shape[0], tm), lambda b, i: (0, i))
    weights = (p["g_mix"], p["w_inT"], p["cq_g"], p["w_uqT"], p["ckv_g"], p["w_k"], p["w_vT"])
    out_shape = (
        jax.ShapeDtypeStruct((B, HA * LANE, S), BF16),
        jax.ShapeDtypeStruct((B, S, KVA * DA), BF16),
        jax.ShapeDtypeStruct((B, KVA * V_ROWS, S), BF16),
        jax.ShapeDtypeStruct((B, HB, nt, LANE, tm), BF16),
        jax.ShapeDtypeStruct((B, S, HB * LANE), BF16),
        jax.ShapeDtypeStruct((B, HB, nt, V_ROWS, tm), BF16),
    )
    out_specs = (
        pl.BlockSpec((1, HA * LANE, tm), lambda b, i: (b, 0, i)),
        pl.BlockSpec((1, tm, KVA * DA), lambda b, i: (b, i, 0)),
        pl.BlockSpec((1, KVA * V_ROWS, tm), lambda b, i: (b, 0, i)),
        pl.BlockSpec((1, HB, 1, LANE, tm), lambda b, i: (b, 0, i, 0, 0)),
        pl.BlockSpec((1, tm, HB * LANE), lambda b, i: (b, i, 0)),
        pl.BlockSpec((1, HB, 1, V_ROWS, tm), lambda b, i: (b, 0, i, 0, 0)),
    )
    return pl.pallas_call(
        _proj_kernel,
        grid=(B, nt),
        in_specs=[pl.BlockSpec((1, tm, D_MODEL), lambda b, i: (b, i, 0))]
        + [full(w) for w in weights] + [tab(t) for t in tabs],
        out_specs=out_specs,
        out_shape=out_shape,
        compiler_params=pltpu.CompilerParams(
            dimension_semantics=("parallel", "parallel"), vmem_limit_bytes=VMEM_LIMIT),
        name="proj",
    )(x, *weights, *tabs)


def _swa_kernel(sink_ref, qaT_ref, kp_ref, kc_ref, kn_ref, vp_ref, vc_ref, vn_ref, oT_ref, s_ref, *, seq):
    tq = qaT_ref.shape[2]
    q0 = pl.program_id(1) * tq
    kcat = jnp.concatenate([kp_ref[0], kc_ref[0], kn_ref[0]], axis=0)
    vcat = jnp.concatenate([vp_ref[0], vc_ref[0], vn_ref[0]], axis=1)
    nk = 3 * WINDOW
    wide = GA * WINDOW
    r = lax.broadcasted_iota(jnp.int32, (nk, wide), 0)
    lane = lax.broadcasted_iota(jnp.int32, (nk, wide), 1)
    d = r - (lane & (WINDOW - 1))
    band = (d >= 0) & (d <= 2 * WINDOW)
    head = lax.broadcasted_iota(jnp.int32, (1, wide), 1) // WINDOW
    units = [(c, g) for c in range(tq // WINDOW) for g in range(KVA)]

    def scores(slot, c, g):
        q4 = jnp.concatenate(
            [qaT_ref[0, (g * GA + h) * LANE:(g * GA + h + 1) * LANE, c * WINDOW:(c + 1) * WINDOW]
             for h in range(GA)], axis=1)
        valid = band
        if c in (0, tq // WINDOW - 1):
            kpos = q0 + (c - 1) * WINDOW + r
            valid = band & (kpos >= 0) & (kpos < seq)
        s = jnp.where(valid, _dot(kcat[c * WINDOW:c * WINDOW + nk], q4), NEG)
        s_ref[slot] = s
        return jnp.max(s, axis=0, keepdims=True)

    def finish(slot, c, g, cmax):
        sink = jnp.zeros((1, wide), F32)
        for h in range(GA):
            sink = jnp.where(head == h, sink_ref[g * GA + h] * LOG2E, sink)
        m = jnp.maximum(cmax, sink)
        p = jnp.exp2(s_ref[slot] - m).astype(BF16)
        pv = _dot(vcat[g * V_ROWS:(g + 1) * V_ROWS, c * WINDOW:c * WINDOW + nk], p)
        o = (pv[:DA] / (pv[DA:DA + 1] + jnp.exp2(sink - m))).astype(BF16)
        for h in range(GA):
            hq = g * GA + h
            oT_ref[0, hq * DA:(hq + 1) * DA, c * WINDOW:(c + 1) * WINDOW] = o[:, h * WINDOW:(h + 1) * WINDOW]

    cmax = scores(0, *units[0])
    for u, (c, g) in enumerate(units):
        nxt = scores((u + 1) % 2, *units[u + 1]) if u + 1 < len(units) else None
        finish(u % 2, c, g, cmax)
        cmax = nxt


def _swa(sink, qaT, ka, vaT):
    B, _, S = qaT.shape
    tq = SWA_TILE
    r = tq // WINDOW
    nb = S // WINDOW
    prev = lambda b, i, *_: (b, jnp.maximum(i * r - 1, 0), 0)
    nxt = lambda b, i, *_: (b, jnp.minimum(i * r + r, nb - 1), 0)
    prev_t = lambda b, i, *_: (b, 0, jnp.maximum(i * r - 1, 0))
    nxt_t = lambda b, i, *_: (b, 0, jnp.minimum(i * r + r, nb - 1))
    grid_spec = pltpu.PrefetchScalarGridSpec(
        num_scalar_prefetch=1,
        grid=(B, S // tq),
        in_specs=[
            pl.BlockSpec((1, HA * LANE, tq), lambda b, i, *_: (b, 0, i)),
            pl.BlockSpec((1, WINDOW, KVA * DA), prev),
            pl.BlockSpec((1, tq, KVA * DA), lambda b, i, *_: (b, i, 0)),
            pl.BlockSpec((1, WINDOW, KVA * DA), nxt),
            pl.BlockSpec((1, KVA * V_ROWS, WINDOW), prev_t),
            pl.BlockSpec((1, KVA * V_ROWS, tq), lambda b, i, *_: (b, 0, i)),
            pl.BlockSpec((1, KVA * V_ROWS, WINDOW), nxt_t),
        ],
        out_specs=pl.BlockSpec((1, HA * DA, tq), lambda b, i, *_: (b, 0, i)),
        scratch_shapes=[pltpu.VMEM((2, 3 * WINDOW, GA * WINDOW), F32)],
    )
    return pl.pallas_call(
        functools.partial(_swa_kernel, seq=S),
        grid_spec=grid_spec,
        out_shape=jax.ShapeDtypeStruct((B, HA * DA, S), BF16),
        compiler_params=pltpu.CompilerParams(
            dimension_semantics=("parallel", "parallel"), vmem_limit_bytes=VMEM_LIMIT),
        name="swa",
    )(sink, qaT, ka, ka, ka, vaT, vaT, vaT)


def _mla_kernel(qT_ref, k_ref, vT_ref, oT_ref, s_ref):
    nq, _, tq = qT_ref.shape[2:]
    nvb, nv, tv = vT_ref.shape[2:]
    tk = s_ref.shape[1]
    nkb = nvb * tv // tk
    vper = tk // tv

    def scores(slot, qi, j):
        kb = k_ref[0, j * tk:(j + 1) * tk, :]
        s = _dot(kb, qT_ref[0, 0, qi])
        s_ref[slot] = s
        return jnp.max(s, axis=0, keepdims=True)

    def softmax_pv(slot, j, cmax, m, acc):
        m_new = jnp.maximum(m, cmax)
        alpha = jnp.exp2(m - m_new)
        p = jnp.exp2(s_ref[slot] - m_new).astype(BF16)
        pv = _dot(vT_ref[0, 0, j * vper], p[:tv])
        for c in range(1, vper):
            pv += _dot(vT_ref[0, 0, j * vper + c], p[c * tv:(c + 1) * tv])
        return m_new, alpha * acc + pv

    def q_tile(qi, cmax):
        m, acc = jnp.full((1, tq), -jnp.inf, F32), jnp.zeros((nv, tq), F32)
        for j in range(nkb):
            if j + 1 < nkb:
                nxt = scores((j + 1) % 2, qi, j + 1)
            else:
                nxt = scores(0, jnp.minimum(qi + 1, nq - 1), 0)
            m, acc = softmax_pv(j % 2, j, cmax, m, acc)
            cmax = nxt
        oT_ref[0, 0, qi] = (acc[:DV] / acc[DV:DV + 1]).astype(BF16)
        return cmax

    lax.fori_loop(0, nq, q_tile, scores(0, 0, 0))


def _mla(qT, k, vT):
    B, _, nq, _, tq = qT.shape
    nvb, nv, tv = vT.shape[2:]
    S = nq * tq
    tk = MLA_KEY_BLOCK
    assert tk % tv == 0 and S % (2 * tk) == 0
    return pl.pallas_call(
        _mla_kernel,
        grid=(B, HB),
        in_specs=[
            pl.BlockSpec((1, 1, nq, LANE, tq), lambda b, h: (b, h, 0, 0, 0)),
            pl.BlockSpec((1, S, LANE), lambda b, h: (b, 0, h)),
            pl.BlockSpec((1, 1, nvb, nv, tv), lambda b, h: (b, h, 0, 0, 0)),
        ],
        out_specs=pl.BlockSpec((1, 1, nq, DV, tq), lambda b, h: (b, h, 0, 0, 0)),
        out_shape=jax.ShapeDtypeStruct((B, HB, nq, DV, tq), BF16),
        scratch_shapes=[pltpu.VMEM((2, tk, tq), F32)],
        compiler_params=pltpu.CompilerParams(
            dimension_semantics=("parallel", "parallel"), vmem_limit_bytes=VMEM_LIMIT),
        name="mla",
    )(qT, k, vT)


def _ffn_kernel(x_ref, oaT_ref, obT_ref, woa_ref, wob_ref, gffn_ref, wg_ref, wu_ref, wd_ref, gfin_ref, y_ref):
    x = x_ref[0]
    obT = obT_ref[0, :, 0].reshape(HB * DV, -1)
    x = x + _dot_tn(oaT_ref[0], woa_ref[...]) + _dot_tn(obT, wob_ref[...])
    h = (x * lax.rsqrt(jnp.mean(x * x, axis=-1, keepdims=True) + EPS) * gffn_ref[...]).astype(BF16)
    g = _dot(h, wg_ref[...])
    u = _dot(h, wu_ref[...])
    a = (g / (1.0 + jnp.exp(-g)) * u).astype(BF16)
    x = x + _dot(a, wd_ref[...])
    y_ref[0] = x * lax.rsqrt(jnp.mean(x * x, axis=-1, keepdims=True) + EPS) * gfin_ref[...]


def _ffn(x, oaT, obT, p):
    B, S, _ = x.shape
    tm = TOK_TILE
    full = lambda a: pl.BlockSpec(a.shape, lambda b, i: (0,) * a.ndim, pipeline_mode=pl.Buffered(1))
    weights = (p["w_oa"], p["w_ob"], p["g_ffn"], p["w_gate"], p["w_up"], p["w_down"], p["g_final"])
    return pl.pallas_call(
        _ffn_kernel,
        grid=(B, S // tm),
        in_specs=[
            pl.BlockSpec((1, tm, D_MODEL), lambda b, i: (b, i, 0)),
            pl.BlockSpec((1, HA * DA, tm), lambda b, i: (b, 0, i)),
            pl.BlockSpec((1, HB, 1, DV, tm), lambda b, i: (b, 0, i, 0, 0)),
        ] + [full(w) for w in weights],
        out_specs=pl.BlockSpec((1, tm, D_MODEL), lambda b, i: (b, i, 0)),
        out_shape=jax.ShapeDtypeStruct(x.shape, F32),
        compiler_params=pltpu.CompilerParams(
            dimension_semantics=("parallel", "parallel"), vmem_limit_bytes=VMEM_LIMIT),
        name="ffn",
    )(x, oaT, obT, *weights)


def _rope_tables_t(seq, dim):
    inv = 1.0 / (ROPE_THETA ** (jnp.arange(0, dim, 2, dtype=F32) / dim))
    ang = jnp.arange(seq, dtype=F32)[:, None] * inv[None, :]
    return jnp.cos(ang).T, jnp.sin(ang).T


def _prep_params(g_mix, w_in, sink, cq_g, w_uq, ckv_g, w_ukv, w_o, g_ffn, w_gate, w_up, w_down, g_final):
    w_uq3 = w_uq.reshape(Q_RANK, HB, DN + DR)
    w_uq_pad = jnp.pad(w_uq3, ((0, 0), (0, 0), (0, LANE - DN - DR))).reshape(Q_RANK, HB * LANE)
    w_ukv3 = w_ukv.reshape(KV_RANK, HB, DN + DV)
    w_uk, w_uv = w_ukv3[..., :DN], w_ukv3[..., DN:]
    top = jnp.pad(w_uk, ((0, 0), (0, 0), (0, LANE - DN)))
    eye = jnp.pad(jnp.eye(DR, dtype=F32)[:, None, :], ((0, 0), (0, 0), (DN, LANE - DN - DR)))
    bot = jnp.broadcast_to(eye, (DR, HB, LANE))
    w_k = jnp.concatenate([top, bot], axis=0).reshape(KV_RANK + DR, HB * LANE)
    return {
        "g_mix": g_mix.reshape(1, D_MODEL),
        "w_inT": w_in.T.astype(BF16),
        "cq_g": cq_g.reshape(Q_RANK, 1),
        "w_uqT": w_uq_pad.T.astype(BF16),
        "ckv_g": ckv_g.reshape(KV_RANK, 1),
        "w_k": w_k.astype(BF16),
        "w_vT": w_uv.reshape(KV_RANK, HB * DV).T.astype(BF16),
        "sink": sink,
        "w_oa": w_o[:HA * DA].astype(BF16),
        "w_ob": w_o[HA * DA:].astype(BF16),
        "g_ffn": g_ffn.reshape(1, D_MODEL),
        "w_gate": w_gate.astype(BF16),
        "w_up": w_up.astype(BF16),
        "w_down": w_down.astype(BF16),
        "g_final": g_final.reshape(1, D_MODEL),
    }


def _trunk(x, p):
    S = x.shape[1]
    tabs = _rope_tables_t(S, DA) + _rope_tables_t(S, DR)
    qaT, ka, vaT, qT, k, vT = _proj(x, p, tabs)
    oaT = _swa(p["sink"], qaT, ka, vaT)
    obT = _mla(qT, k, vT)
    return _ffn(x, oaT, obT, p)


def kernel(x_prompt, x_sample, g_mix, w_in, sink, cq_g, w_uq, ckv_g, w_ukv, w_o, g_ffn,
           w_gate, w_up, w_down, g_final):
    p = _prep_params(g_mix[0], w_in[0], sink[0], cq_g[0], w_uq[0], ckv_g[0], w_ukv[0], w_o[0],
                     g_ffn[0], w_gate[0], w_up[0], w_down[0], g_final)
    return _trunk(x_prompt, p), _trunk(x_sample, p)
```

```python
import functools
import math

import jax
import jax.numpy as jnp
from jax import lax
from jax.experimental import pallas as pl
from jax.experimental.pallas import tpu as pltpu

D_MODEL = 1024
EPS = 1e-6
ROPE_THETA = 10000.0
WINDOW = 128
HA, KVA, DA = 8, 2, 64
GA = HA // KVA
HB, Q_RANK, KV_RANK, DN, DR, DV = 8, 384, 256, 64, 32, 64
D_IN = HA * DA + 2 * KVA * DA + Q_RANK + KV_RANK + DR
D_FF = int(math.ceil(8 * D_MODEL / 3 / 256) * 256)
NEG = -1e30

LANE = 128
V_ROWS = 128
LOG2E = math.log2(math.e)
SWA_Q_SCALE = DA ** -0.5 * LOG2E
MLA_Q_SCALE = (DN + DR) ** -0.5 * LOG2E
OFF_KA = HA * DA
OFF_VA = OFF_KA + KVA * DA
OFF_CQ = OFF_VA + KVA * DA
OFF_CKV = OFF_CQ + Q_RANK
OFF_KR = OFF_CKV + KV_RANK

TOK_TILE = 512
SWA_TILE = 512
MLA_KEY_BLOCK = 1024
MLA_Q_UNROLL = 2
VMEM_LIMIT = 56 * 1024 * 1024

F32 = jnp.float32
BF16 = jnp.bfloat16


def _dot(a, b):
    return jnp.dot(a, b, preferred_element_type=F32)


def _dot_tn(a, b):
    return lax.dot_general(a, b, (((0,), (0,)), ((), ())), preferred_element_type=F32)


def _dot_nt(a, b):
    return lax.dot_general(a, b, (((1,), (1,)), ((), ())), preferred_element_type=F32)


def _rope_rows(x1, x2, c, s):
    return x1 * c - x2 * s, x2 * c + x1 * s


def _proj_kernel(x_ref, gmix_ref, winT_ref, cqg_ref, wuqT_ref, ckvg_ref, wk_ref, wvT_ref,
                 cosa_ref, sina_ref, cosr_ref, sinr_ref,
                 qaT_ref, ka_ref, vaT_ref, qT_ref, k_ref, vT_ref):
    x = x_ref[0]
    ms = jnp.mean(x * x, axis=-1, keepdims=True)
    h = (x * lax.rsqrt(ms + EPS) * gmix_ref[...]).astype(BF16)
    zT = _dot_nt(winT_ref[...], h)

    ca, sa = cosa_ref[...], sina_ref[...]
    cr, sr = cosr_ref[...], sinr_ref[...]
    half = DA // 2

    qaT_ref[0] = jnp.zeros(qaT_ref.shape[1:], BF16)
    for hq in range(HA):
        r = hq * DA
        o1, o2 = _rope_rows(zT[r:r + half], zT[r + half:r + DA], ca, sa)
        base = hq * LANE + (hq // GA) * DA
        qaT_ref[0, base:base + half, :] = (o1 * SWA_Q_SCALE).astype(BF16)
        qaT_ref[0, base + half:base + DA, :] = (o2 * SWA_Q_SCALE).astype(BF16)

    kaT = []
    for g in range(KVA):
        r = OFF_KA + g * DA
        kaT.extend(_rope_rows(zT[r:r + half], zT[r + half:r + DA], ca, sa))
    ka_ref[0] = jnp.concatenate(kaT, axis=0).T.astype(BF16)
    tm = zT.shape[1]
    ones_row = lax.broadcasted_iota(jnp.int32, (V_ROWS - DV, tm), 0) == 0
    for g in range(KVA):
        r = OFF_VA + g * DA
        vaT_ref[0, g * V_ROWS:g * V_ROWS + DA, :] = zT[r:r + DA].astype(BF16)
        vaT_ref[0, g * V_ROWS + DA:(g + 1) * V_ROWS, :] = ones_row.astype(BF16)

    cq = zT[OFF_CQ:OFF_CKV]
    cqn = (cq * lax.rsqrt(jnp.mean(cq * cq, axis=0, keepdims=True) + EPS) * cqg_ref[...]).astype(BF16)
    qT = _dot(wuqT_ref[...], cqn) * MLA_Q_SCALE
    qT_ref[0, :, 0] = qT.astype(BF16).reshape(HB, LANE, tm)
    hr = DR // 2
    for hh in range(HB):
        r = hh * LANE + DN
        o1, o2 = _rope_rows(qT[r:r + hr], qT[r + hr:r + DR], cr, sr)
        qT_ref[0, hh, 0, DN:DN + hr, :] = o1.astype(BF16)
        qT_ref[0, hh, 0, DN + hr:DN + DR, :] = o2.astype(BF16)

    ckv = zT[OFF_CKV:OFF_KR]
    ckvn = (ckv * lax.rsqrt(jnp.mean(ckv * ckv, axis=0, keepdims=True) + EPS) * ckvg_ref[...]).astype(BF16)
    krT = zT[OFF_KR:D_IN]
    k1, k2 = _rope_rows(krT[:hr], krT[hr:], cr, sr)
    kin = jnp.concatenate([ckvn, k1.astype(BF16), k2.astype(BF16)], axis=0)
    k_ref[0] = _dot_tn(kin, wk_ref[...]).astype(BF16)
    vT_ref[0, :, 0, :DV] = _dot(wvT_ref[...], ckvn).astype(BF16).reshape(HB, DV, tm)
    vT_ref[0, :, 0, DV:] = jnp.broadcast_to(ones_row.astype(BF16), (HB, V_ROWS - DV, tm))


def _proj(x, p, tabs):
    B, S, _ = x.shape
    tm = TOK_TILE
    nt = S // tm
    full = lambda a: pl.BlockSpec(a.shape, lambda b, i: (0,) * a.ndim)
    tab = lambda a: pl.BlockSpec((a.shape[0], tm), lambda b, i: (0, i))
    weights = (p["g_mix"], p["w_inT"], p["cq_g"], p["w_uqT"], p["ckv_g"], p["w_k"], p["w_vT"])
    out_shape = (
        jax.ShapeDtypeStruct((B, HA * LANE, S), BF16),
        jax.ShapeDtypeStruct((B, S, KVA * DA), BF16),
        jax.ShapeDtypeStruct((B, KVA * V_ROWS, S), BF16),
        jax.ShapeDtypeStruct((B, HB, nt, LANE, tm), BF16),
        jax.ShapeDtypeStruct((B, S, HB * LANE), BF16),
        jax.ShapeDtypeStruct((B, HB, nt, V_ROWS, tm), BF16),
    )
    out_specs = (
        pl.BlockSpec((1, HA * LANE, tm), lambda b, i: (b, 0, i)),
        pl.BlockSpec((1, tm, KVA * DA), lambda b, i: (b, i, 0)),
        pl.BlockSpec((1, KVA * V_ROWS, tm), lambda b, i: (b, 0, i)),
        pl.BlockSpec((1, HB, 1, LANE, tm), lambda b, i: (b, 0, i, 0, 0)),
        pl.BlockSpec((1, tm, HB * LANE), lambda b, i: (b, i, 0)),
        pl.BlockSpec((1, HB, 1, V_ROWS, tm), lambda b, i: (b, 0, i, 0, 0)),
    )
    return pl.pallas_call(
        _proj_kernel,
        grid=(B, nt),
        in_specs=[pl.BlockSpec((1, tm, D_MODEL), lambda b, i: (b, i, 0))]
        + [full(w) for w in weights] + [tab(t) for t in tabs],
        out_specs=out_specs,
        out_shape=out_shape,
        compiler_params=pltpu.CompilerParams(
            dimension_semantics=("parallel", "parallel"), vmem_limit_bytes=VMEM_LIMIT),
        name="proj",
    )(x, *weights, *tabs)


def _swa_kernel(sink_ref, qaT_ref, kp_ref, kc_ref, kn_ref, vp_ref, vc_ref, vn_ref, oT_ref, s_ref, *, seq):
    tq = qaT_ref.shape[2]
    q0 = pl.program_id(1) * tq
    kcat = jnp.concatenate([kp_ref[0], kc_ref[0], kn_ref[0]], axis=0)
    vcat = jnp.concatenate([vp_ref[0], vc_ref[0], vn_ref[0]], axis=1)
    nk = 3 * WINDOW
    wide = GA * WINDOW
    r = lax.broadcasted_iota(jnp.int32, (nk, wide), 0)
    lane = lax.broadcasted_iota(jnp.int32, (nk, wide), 1)
    d = r - (lane & (WINDOW - 1))
    band = (d >= 0) & (d <= 2 * WINDOW)
    head = lax.broadcasted_iota(jnp.int32, (1, wide), 1) // WINDOW
    units = [(c, g) for c in range(tq // WINDOW) for g in range(KVA)]

    def scores(slot, c, g):
        q4 = jnp.concatenate(
            [qaT_ref[0, (g * GA + h) * LANE:(g * GA + h + 1) * LANE, c * WINDOW:(c + 1) * WINDOW]
             for h in range(GA)], axis=1)
        valid = band
        if c in (0, tq // WINDOW - 1):
            kpos = q0 + (c - 1) * WINDOW + r
            valid = band & (kpos >= 0) & (kpos < seq)
        s = jnp.where(valid, _dot(kcat[c * WINDOW:c * WINDOW + nk], q4), NEG)
        s_ref[slot] = s
        return jnp.max(s, axis=0, keepdims=True)

    def finish(slot, c, g, cmax):
        sink = jnp.zeros((1, wide), F32)
        for h in range(GA):
            sink = jnp.where(head == h, sink_ref[g * GA + h] * LOG2E, sink)
        m = jnp.maximum(cmax, sink)
        p = jnp.exp2(s_ref[slot] - m).astype(BF16)
        pv = _dot(vcat[g * V_ROWS:(g + 1) * V_ROWS, c * WINDOW:c * WINDOW + nk], p)
        o = (pv[:DA] / (pv[DA:DA + 1] + jnp.exp2(sink - m))).astype(BF16)
        for h in range(GA):
            hq = g * GA + h
            oT_ref[0, hq * DA:(hq + 1) * DA, c * WINDOW:(c + 1) * WINDOW] = o[:, h * WINDOW:(h + 1) * WINDOW]

    cmax = scores(0, *units[0])
    for u, (c, g) in enumerate(units):
        nxt = scores((u + 1) % 2, *units[u + 1]) if u + 1 < len(units) else None
        finish(u % 2, c, g, cmax)
        cmax = nxt


def _swa(sink, qaT, ka, vaT):
    B, _, S = qaT.shape
    tq = SWA_TILE
    r = tq // WINDOW
    nb = S // WINDOW
    prev = lambda b, i, *_: (b, jnp.maximum(i * r - 1, 0), 0)
    nxt = lambda b, i, *_: (b, jnp.minimum(i * r + r, nb - 1), 0)
    prev_t = lambda b, i, *_: (b, 0, jnp.maximum(i * r - 1, 0))
    nxt_t = lambda b, i, *_: (b, 0, jnp.minimum(i * r + r, nb - 1))
    grid_spec = pltpu.PrefetchScalarGridSpec(
        num_scalar_prefetch=1,
        grid=(B, S // tq),
        in_specs=[
            pl.BlockSpec((1, HA * LANE, tq), lambda b, i, *_: (b, 0, i)),
            pl.BlockSpec((1, WINDOW, KVA * DA), prev),
            pl.BlockSpec((1, tq, KVA * DA), lambda b, i, *_: (b, i, 0)),
            pl.BlockSpec((1, WINDOW, KVA * DA), nxt),
            pl.BlockSpec((1, KVA * V_ROWS, WINDOW), prev_t),
            pl.BlockSpec((1, KVA * V_ROWS, tq), lambda b, i, *_: (b, 0, i)),
            pl.BlockSpec((1, KVA * V_ROWS, WINDOW), nxt_t),
        ],
        out_specs=pl.BlockSpec((1, HA * DA, tq), lambda b, i, *_: (b, 0, i)),
        scratch_shapes=[pltpu.VMEM((2, 3 * WINDOW, GA * WINDOW), F32)],
    )
    return pl.pallas_call(
        functools.partial(_swa_kernel, seq=S),
        grid_spec=grid_spec,
        out_shape=jax.ShapeDtypeStruct((B, HA * DA, S), BF16),
        compiler_params=pltpu.CompilerParams(
            dimension_semantics=("parallel", "parallel"), vmem_limit_bytes=VMEM_LIMIT),
        name="swa",
    )(sink, qaT, ka, ka, ka, vaT, vaT, vaT)


def _mla_kernel(qT_ref, k_ref, vT_ref, oT_ref, s_ref):
    nq, _, tq = qT_ref.shape[2:]
    nvb, nv, tv = vT_ref.shape[2:]
    tk = s_ref.shape[1]
    nkb = nvb * tv // tk
    vper = tk // tv

    def scores(slot, qi, j):
        kb = k_ref[0, j * tk:(j + 1) * tk, :]
        s = _dot(kb, qT_ref[0, 0, qi])
        s_ref[slot] = s
        return jnp.max(s, axis=0, keepdims=True)

    def softmax_pv(slot, j, cmax, m, acc):
        m_new = jnp.maximum(m, cmax)
        alpha = jnp.exp2(m - m_new)
        p = jnp.exp2(s_ref[slot] - m_new).astype(BF16)
        pv = _dot(vT_ref[0, 0, j * vper], p[:tv])
        for c in range(1, vper):
            pv += _dot(vT_ref[0, 0, j * vper + c], p[c * tv:(c + 1) * tv])
        return m_new, alpha * acc + pv

    def q_tile(qi, cmax):
        m, acc = jnp.full((1, tq), -jnp.inf, F32), jnp.zeros((nv, tq), F32)
        for j in range(nkb):
            if j + 1 < nkb:
                nxt = scores((j + 1) % 2, qi, j + 1)
            else:
                nxt = scores(0, jnp.minimum(qi + 1, nq - 1), 0)
            m, acc = softmax_pv(j % 2, j, cmax, m, acc)
            cmax = nxt
        oT_ref[0, 0, qi] = (acc[:DV] / acc[DV:DV + 1]).astype(BF16)
        return cmax

    def q_tiles(t, cmax):
        for u in range(MLA_Q_UNROLL):
            cmax = q_tile(t * MLA_Q_UNROLL + u, cmax)
        return cmax

    lax.fori_loop(0, nq // MLA_Q_UNROLL, q_tiles, scores(0, 0, 0))


def _mla(qT, k, vT):
    B, _, nq, _, tq = qT.shape
    nvb, nv, tv = vT.shape[2:]
    S = nq * tq
    tk = MLA_KEY_BLOCK
    assert tk % tv == 0 and S % (2 * tk) == 0
    return pl.pallas_call(
        _mla_kernel,
        grid=(B, HB),
        in_specs=[
            pl.BlockSpec((1, 1, nq, LANE, tq), lambda b, h: (b, h, 0, 0, 0)),
            pl.BlockSpec((1, S, LANE), lambda b, h: (b, 0, h)),
            pl.BlockSpec((1, 1, nvb, nv, tv), lambda b, h: (b, h, 0, 0, 0)),
        ],
        out_specs=pl.BlockSpec((1, 1, nq, DV, tq), lambda b, h: (b, h, 0, 0, 0)),
        out_shape=jax.ShapeDtypeStruct((B, HB, nq, DV, tq), BF16),
        scratch_shapes=[pltpu.VMEM((2, tk, tq), F32)],
        compiler_params=pltpu.CompilerParams(
            dimension_semantics=("parallel", "parallel"), vmem_limit_bytes=VMEM_LIMIT),
        name="mla",
    )(qT, k, vT)


def _ffn_kernel(x_ref, oaT_ref, obT_ref, woa_ref, wob_ref, gffn_ref, wg_ref, wu_ref, wd_ref, gfin_ref, y_ref):
    x = x_ref[0]
    obT = obT_ref[0, :, 0].reshape(HB * DV, -1)
    x = x + _dot_tn(oaT_ref[0], woa_ref[...]) + _dot_tn(obT, wob_ref[...])
    h = (x * lax.rsqrt(jnp.mean(x * x, axis=-1, keepdims=True) + EPS) * gffn_ref[...]).astype(BF16)
    g = _dot(h, wg_ref[...])
    u = _dot(h, wu_ref[...])
    a = (g / (1.0 + jnp.exp(-g)) * u).astype(BF16)
    x = x + _dot(a, wd_ref[...])
    y_ref[0] = x * lax.rsqrt(jnp.mean(x * x, axis=-1, keepdims=True) + EPS) * gfin_ref[...]


def _ffn(x, oaT, obT, p):
    B, S, _ = x.shape
    tm = TOK_TILE
    full = lambda a: pl.BlockSpec(a.shape, lambda b, i: (0,) * a.ndim, pipeline_mode=pl.Buffered(1))
    weights = (p["w_oa"], p["w_ob"], p["g_ffn"], p["w_gate"], p["w_up"], p["w_down"], p["g_final"])
    return pl.pallas_call(
        _ffn_kernel,
        grid=(B, S // tm),
        in_specs=[
            pl.BlockSpec((1, tm, D_MODEL), lambda b, i: (b, i, 0)),
            pl.BlockSpec((1, HA * DA, tm), lambda b, i: (b, 0, i)),
            pl.BlockSpec((1, HB, 1, DV, tm), lambda b, i: (b, 0, i, 0, 0)),
        ] + [full(w) for w in weights],
        out_specs=pl.BlockSpec((1, tm, D_MODEL), lambda b, i: (b, i, 0)),
        out_shape=jax.ShapeDtypeStruct(x.shape, F32),
        compiler_params=pltpu.CompilerParams(
            dimension_semantics=("parallel", "parallel"), vmem_limit_bytes=VMEM_LIMIT),
        name="ffn",
    )(x, oaT, obT, *weights)


def _rope_tables_t(seq, dim):
    inv = 1.0 / (ROPE_THETA ** (jnp.arange(0, dim, 2, dtype=F32) / dim))
    ang = jnp.arange(seq, dtype=F32)[:, None] * inv[None, :]
    return jnp.cos(ang).T, jnp.sin(ang).T


def _prep_params(g_mix, w_in, sink, cq_g, w_uq, ckv_g, w_ukv, w_o, g_ffn, w_gate, w_up, w_down, g_final):
    w_uq3 = w_uq.reshape(Q_RANK, HB, DN + DR)
    w_uq_pad = jnp.pad(w_uq3, ((0, 0), (0, 0), (0, LANE - DN - DR))).reshape(Q_RANK, HB * LANE)
    w_ukv3 = w_ukv.reshape(KV_RANK, HB, DN + DV)
    w_uk, w_uv = w_ukv3[..., :DN], w_ukv3[..., DN:]
    top = jnp.pad(w_uk, ((0, 0), (0, 0), (0, LANE - DN)))
    eye = jnp.pad(jnp.eye(DR, dtype=F32)[:, None, :], ((0, 0), (0, 0), (DN, LANE - DN - DR)))
    bot = jnp.broadcast_to(eye, (DR, HB, LANE))
    w_k = jnp.concatenate([top, bot], axis=0).reshape(KV_RANK + DR, HB * LANE)
    return {
        "g_mix": g_mix.reshape(1, D_MODEL),
        "w_inT": w_in.T.astype(BF16),
        "cq_g": cq_g.reshape(Q_RANK, 1),
        "w_uqT": w_uq_pad.T.astype(BF16),
        "ckv_g": ckv_g.reshape(KV_RANK, 1),
        "w_k": w_k.astype(BF16),
        "w_vT": w_uv.reshape(KV_RANK, HB * DV).T.astype(BF16),
        "sink": sink,
        "w_oa": w_o[:HA * DA].astype(BF16),
        "w_ob": w_o[HA * DA:].astype(BF16),
        "g_ffn": g_ffn.reshape(1, D_MODEL),
        "w_gate": w_gate.astype(BF16),
        "w_up": w_up.astype(BF16),
        "w_down": w_down.astype(BF16),
        "g_final": g_final.reshape(1, D_MODEL),
    }


def _trunk(x, p):
    S = x.shape[1]
    tabs = _rope_tables_t(S, DA) + _rope_tables_t(S, DR)
    qaT, ka, vaT, qT, k, vT = _proj(x, p, tabs)
    oaT = _swa(p["sink"], qaT, ka, vaT)
    obT = _mla(qT, k, vT)
    return _ffn(x, oaT, obT, p)


def kernel(x_prompt, x_sample, g_mix, w_in, sink, cq_g, w_uq, ckv_g, w_ukv, w_o, g_ffn,
           w_gate, w_up, w_down, g_final):
    p = _prep_params(g_mix[0], w_in[0], sink[0], cq_g[0], w_uq[0], ckv_g[0], w_ukv[0], w_o[0],
                     g_ffn[0], w_gate[0], w_up[0], w_down[0], g_final)
    return _trunk(x_prompt, p), _trunk(x_sample, p)
```

```python
import functools
import math

import jax
import jax.numpy as jnp
from jax import lax
from jax.experimental import pallas as pl
from jax.experimental.pallas import tpu as pltpu

D_MODEL = 1024
EPS = 1e-6
ROPE_THETA = 10000.0
WINDOW = 128
HA, KVA, DA = 8, 2, 64
GA = HA // KVA
HB, Q_RANK, KV_RANK, DN, DR, DV = 8, 384, 256, 64, 32, 64
D_IN = HA * DA + 2 * KVA * DA + Q_RANK + KV_RANK + DR
D_FF = int(math.ceil(8 * D_MODEL / 3 / 256) * 256)
NEG = -1e30

LANE = 128
V_ROWS = 128
LOG2E = math.log2(math.e)
SWA_Q_SCALE = DA ** -0.5 * LOG2E
MLA_Q_SCALE = (DN + DR) ** -0.5 * LOG2E
OFF_KA = HA * DA
OFF_VA = OFF_KA + KVA * DA
OFF_CQ = OFF_VA + KVA * DA
OFF_CKV = OFF_CQ + Q_RANK
OFF_KR = OFF_CKV + KV_RANK

TOK_TILE = 512
SWA_TILE = 512
MLA_KEY_BLOCK = 1024
MLA_Q_UNROLL = 2
VMEM_LIMIT = 56 * 1024 * 1024

F32 = jnp.float32
BF16 = jnp.bfloat16


def _dot(a, b):
    return jnp.dot(a, b, preferred_element_type=F32)


def _dot_tn(a, b):
    return lax.dot_general(a, b, (((0,), (0,)), ((), ())), preferred_element_type=F32)


def _dot_nt(a, b):
    return lax.dot_general(a, b, (((1,), (1,)), ((), ())), preferred_element_type=F32)


def _rope_rows(x1, x2, c, s):
    return x1 * c - x2 * s, x2 * c + x1 * s


def _proj_kernel(x_ref, gmix_ref, winT_ref, cqg_ref, wuqT_ref, ckvg_ref, wk_ref, wvT_ref,
                 cosa_ref, sina_ref, cosr_ref, sinr_ref,
                 qaT_ref, ka_ref, vaT_ref, qT_ref, k_ref, vT_ref):
    x = x_ref[0]
    ms = jnp.mean(x * x, axis=-1, keepdims=True)
    h = (x * lax.rsqrt(ms + EPS) * gmix_ref[...]).astype(BF16)
    zT = _dot_nt(winT_ref[...], h)

    ca, sa = cosa_ref[...], sina_ref[...]
    cr, sr = cosr_ref[...], sinr_ref[...]
    half = DA // 2

    qaT_ref[0] = jnp.zeros(qaT_ref.shape[1:], BF16)
    for hq in range(HA):
        r = hq * DA
        o1, o2 = _rope_rows(zT[r:r + half], zT[r + half:r + DA], ca, sa)
        base = hq * LANE + (hq // GA) * DA
        qaT_ref[0, base:base + half, :] = (o1 * SWA_Q_SCALE).astype(BF16)
        qaT_ref[0, base + half:base + DA, :] = (o2 * SWA_Q_SCALE).astype(BF16)

    kaT = []
    for g in range(KVA):
        r = OFF_KA + g * DA
        kaT.extend(_rope_rows(zT[r:r + half], zT[r + half:r + DA], ca, sa))
    ka_ref[0] = jnp.concatenate(kaT, axis=0).T.astype(BF16)
    tm = zT.shape[1]
    ones_row = lax.broadcasted_iota(jnp.int32, (V_ROWS - DV, tm), 0) == 0
    for g in range(KVA):
        r = OFF_VA + g * DA
        vaT_ref[0, g * V_ROWS:g * V_ROWS + DA, :] = zT[r:r + DA].astype(BF16)
        vaT_ref[0, g * V_ROWS + DA:(g + 1) * V_ROWS, :] = ones_row.astype(BF16)

    cq = zT[OFF_CQ:OFF_CKV]
    cqn = (cq * lax.rsqrt(jnp.mean(cq * cq, axis=0, keepdims=True) + EPS) * cqg_ref[...]).astype(BF16)
    qT = _dot(wuqT_ref[...], cqn) * MLA_Q_SCALE
    qT_ref[0, :, 0] = qT.astype(BF16).reshape(HB, LANE, tm)
    hr = DR // 2
    for hh in range(HB):
        r = hh * LANE + DN
        o1, o2 = _rope_rows(qT[r:r + hr], qT[r + hr:r + DR], cr, sr)
        qT_ref[0, hh, 0, DN:DN + hr, :] = o1.astype(BF16)
        qT_ref[0, hh, 0, DN + hr:DN + DR, :] = o2.astype(BF16)

    ckv = zT[OFF_CKV:OFF_KR]
    ckvn = (ckv * lax.rsqrt(jnp.mean(ckv * ckv, axis=0, keepdims=True) + EPS) * ckvg_ref[...]).astype(BF16)
    krT = zT[OFF_KR:D_IN]
    k1, k2 = _rope_rows(krT[:hr], krT[hr:], cr, sr)
    kin = jnp.concatenate([ckvn, k1.astype(BF16), k2.astype(BF16)], axis=0)
    k_ref[0] = _dot_tn(kin, wk_ref[...]).astype(BF16)
    vT_ref[0, :, 0, :DV] = _dot(wvT_ref[...], ckvn).astype(BF16).reshape(HB, DV, tm)
    vT_ref[0, :, 0, DV:] = jnp.broadcast_to(ones_row.astype(BF16), (HB, V_ROWS - DV, tm))


def _proj(x, p, tabs):
    B, S, _ = x.shape
    tm = TOK_TILE
    nt = S // tm
    full = lambda a: pl.BlockSpec(a.shape, lambda b, i: (0,) * a.ndim)
    tab = lambda a: pl.BlockSpec((a.shape[0], tm), lambda b, i: (0, i))
    weights = (p["g_mix"], p["w_inT"], p["cq_g"], p["w_uqT"], p["ckv_g"], p["w_k"], p["w_vT"])
    out_shape = (
        jax.ShapeDtypeStruct((B, HA * LANE, S), BF16),
        jax.ShapeDtypeStruct((B, S, KVA * DA), BF16),
        jax.ShapeDtypeStruct((B, KVA * V_ROWS, S), BF16),
        jax.ShapeDtypeStruct((B, HB, nt, LANE, tm), BF16),
        jax.ShapeDtypeStruct((B, S, HB * LANE), BF16),
        jax.ShapeDtypeStruct((B, HB, nt, V_ROWS, tm), BF16),
    )
    out_specs = (
        pl.BlockSpec((1, HA * LANE, tm), lambda b, i: (b, 0, i)),
        pl.BlockSpec((1, tm, KVA * DA), lambda b, i: (b, i, 0)),
        pl.BlockSpec((1, KVA * V_ROWS, tm), lambda b, i: (b, 0, i)),
        pl.BlockSpec((1, HB, 1, LANE, tm), lambda b, i: (b, 0, i, 0, 0)),
        pl.BlockSpec((1, tm, HB * LANE), lambda b, i: (b, i, 0)),
        pl.BlockSpec((1, HB, 1, V_ROWS, tm), lambda b, i: (b, 0, i, 0, 0)),
    )
    return pl.pallas_call(
        _proj_kernel,
        grid=(B, nt),
        in_specs=[pl.BlockSpec((1, tm, D_MODEL), lambda b, i: (b, i, 0))]
        + [full(w) for w in weights] + [tab(t) for t in tabs],
        out_specs=out_specs,
        out_shape=out_shape,
        compiler_params=pltpu.CompilerParams(
            dimension_semantics=("parallel", "parallel"), vmem_limit_bytes=VMEM_LIMIT),
        name="proj",
    )(x, *weights, *tabs)


def _swa_kernel(sink_ref, qaT_ref, kp_ref, kc_ref, kn_ref, vp_ref, vc_ref, vn_ref, oT_ref, s_ref, *, seq):
    tq = qaT_ref.shape[2]
    q0 = pl.program_id(1) * tq
    kcat = jnp.concatenate([kp_ref[0], kc_ref[0], kn_ref[0]], axis=0)
    vcat = jnp.concatenate([vp_ref[0], vc_ref[0], vn_ref[0]], axis=1)
    nk = 3 * WINDOW
    wide = GA * WINDOW
    r = lax.broadcasted_iota(jnp.int32, (nk, wide), 0)
    lane = lax.broadcasted_iota(jnp.int32, (nk, wide), 1)
    d = r - (lane & (WINDOW - 1))
    band = (d >= 0) & (d <= 2 * WINDOW)
    head = lax.broadcasted_iota(jnp.int32, (1, wide), 1) // WINDOW
    units = [(c, g) for c in range(tq // WINDOW) for g in range(KVA)]

    def scores(slot, c, g):
        q4 = jnp.concatenate(
            [qaT_ref[0, (g * GA + h) * LANE:(g * GA + h + 1) * LANE, c * WINDOW:(c + 1) * WINDOW]
             for h in range(GA)], axis=1)
        valid = band
        if c in (0, tq // WINDOW - 1):
            kpos = q0 + (c - 1) * WINDOW + r
            valid = band & (kpos >= 0) & (kpos < seq)
        s = jnp.where(valid, _dot(kcat[c * WINDOW:c * WINDOW + nk], q4), NEG)
        s_ref[slot] = s
        return jnp.max(s, axis=0, keepdims=True)

    def finish(slot, c, g, cmax):
        sink = jnp.zeros((1, wide), F32)
        for h in range(GA):
            sink = jnp.where(head == h, sink_ref[g * GA + h] * LOG2E, sink)
        m = jnp.maximum(cmax, sink)
        p = jnp.exp2(s_ref[slot] - m).astype(BF16)
        pv = _dot(vcat[g * V_ROWS:(g + 1) * V_ROWS, c * WINDOW:c * WINDOW + nk], p)
        o = (pv[:DA] / (pv[DA:DA + 1] + jnp.exp2(sink - m))).astype(BF16)
        for h in range(GA):
            hq = g * GA + h
            oT_ref[0, hq * DA:(hq + 1) * DA, c * WINDOW:(c + 1) * WINDOW] = o[:, h * WINDOW:(h + 1) * WINDOW]

    cmax = scores(0, *units[0])
    for u, (c, g) in enumerate(units):
        nxt = scores((u + 1) % 2, *units[u + 1]) if u + 1 < len(units) else None
        finish(u % 2, c, g, cmax)
        cmax = nxt


def _swa(sink, qaT, ka, vaT):
    B, _, S = qaT.shape
    tq = SWA_TILE
    r = tq // WINDOW
    nb = S // WINDOW
    prev = lambda b, i, *_: (b, jnp.maximum(i * r - 1, 0), 0)
    nxt = lambda b, i, *_: (b, jnp.minimum(i * r + r, nb - 1), 0)
    prev_t = lambda b, i, *_: (b, 0, jnp.maximum(i * r - 1, 0))
    nxt_t = lambda b, i, *_: (b, 0, jnp.minimum(i * r + r, nb - 1))
    grid_spec = pltpu.PrefetchScalarGridSpec(
        num_scalar_prefetch=1,
        grid=(B, S // tq),
        in_specs=[
            pl.BlockSpec((1, HA * LANE, tq), lambda b, i, *_: (b, 0, i)),
            pl.BlockSpec((1, WINDOW, KVA * DA), prev),
            pl.BlockSpec((1, tq, KVA * DA), lambda b, i, *_: (b, i, 0)),
            pl.BlockSpec((1, WINDOW, KVA * DA), nxt),
            pl.BlockSpec((1, KVA * V_ROWS, WINDOW), prev_t),
            pl.BlockSpec((1, KVA * V_ROWS, tq), lambda b, i, *_: (b, 0, i)),
            pl.BlockSpec((1, KVA * V_ROWS, WINDOW), nxt_t),
        ],
        out_specs=pl.BlockSpec((1, HA * DA, tq), lambda b, i, *_: (b, 0, i)),
        scratch_shapes=[pltpu.VMEM((2, 3 * WINDOW, GA * WINDOW), F32)],
    )
    return pl.pallas_call(
        functools.partial(_swa_kernel, seq=S),
        grid_spec=grid_spec,
        out_shape=jax.ShapeDtypeStruct((B, HA * DA, S), BF16),
        compiler_params=pltpu.CompilerParams(
            dimension_semantics=("parallel", "parallel"), vmem_limit_bytes=VMEM_LIMIT),
        name="swa",
    )(sink, qaT, ka, ka, ka, vaT, vaT, vaT)


def _mla_kernel(qT_ref, k_ref, vT_ref, oT_ref, s_ref):
    nq, _, tq = qT_ref.shape[2:]
    nvb, nv, tv = vT_ref.shape[2:]
    tk = s_ref.shape[1]
    nkb = nvb * tv // tk
    vper = tk // tv

    def scores(slot, qi, j):
        kb = k_ref[0, j * tk:(j + 1) * tk, :]
        s = _dot(kb, qT_ref[0, 0, qi])
        s_ref[slot] = s
        return jnp.max(s, axis=0, keepdims=True)

    def softmax_pv(slot, j, cmax, m, acc):
        m_new = jnp.maximum(m, cmax)
        alpha = jnp.exp2(m - m_new)
        p = jnp.exp2(s_ref[slot] - m_new).astype(BF16)
        pv = _dot(vT_ref[0, 0, j * vper, :, :LANE], p[:LANE])
        pv += _dot(vT_ref[0, 0, j * vper, :, LANE:], p[LANE:tv])
        for c in range(1, vper):
            pv += _dot(vT_ref[0, 0, j * vper + c], p[c * tv:(c + 1) * tv])
        return m_new, alpha * acc + pv

    def q_tile(qi, cmax):
        m, acc = jnp.full((1, tq), -jnp.inf, F32), jnp.zeros((nv, tq), F32)
        for j in range(nkb):
            if j + 1 < nkb:
                nxt = scores((j + 1) % 2, qi, j + 1)
            else:
                nxt = scores(0, jnp.minimum(qi + 1, nq - 1), 0)
            m, acc = softmax_pv(j % 2, j, cmax, m, acc)
            cmax = nxt
        oT_ref[0, 0, qi] = (acc[:DV] / acc[DV:DV + 1]).astype(BF16)
        return cmax

    def q_tiles(t, cmax):
        for u in range(MLA_Q_UNROLL):
            cmax = q_tile(t * MLA_Q_UNROLL + u, cmax)
        return cmax

    lax.fori_loop(0, nq // MLA_Q_UNROLL, q_tiles, scores(0, 0, 0))


def _mla(qT, k, vT):
    B, _, nq, _, tq = qT.shape
    nvb, nv, tv = vT.shape[2:]
    S = nq * tq
    tk = MLA_KEY_BLOCK
    assert tk % tv == 0 and S % (2 * tk) == 0
    return pl.pallas_call(
        _mla_kernel,
        grid=(B, HB),
        in_specs=[
            pl.BlockSpec((1, 1, nq, LANE, tq), lambda b, h: (b, h, 0, 0, 0)),
            pl.BlockSpec((1, S, LANE), lambda b, h: (b, 0, h)),
            pl.BlockSpec((1, 1, nvb, nv, tv), lambda b, h: (b, h, 0, 0, 0)),
        ],
        out_specs=pl.BlockSpec((1, 1, nq, DV, tq), lambda b, h: (b, h, 0, 0, 0)),
        out_shape=jax.ShapeDtypeStruct((B, HB, nq, DV, tq), BF16),
        scratch_shapes=[pltpu.VMEM((2, tk, tq), F32)],
        compiler_params=pltpu.CompilerParams(
            dimension_semantics=("parallel", "parallel"), vmem_limit_bytes=VMEM_LIMIT),
        name="mla",
    )(qT, k, vT)


def _ffn_kernel(x_ref, oaT_ref, obT_ref, woa_ref, wob_ref, gffn_ref, wg_ref, wu_ref, wd_ref, gfin_ref, y_ref):
    x = x_ref[0]
    obT = obT_ref[0, :, 0].reshape(HB * DV, -1)
    x = x + _dot_tn(oaT_ref[0], woa_ref[...]) + _dot_tn(obT, wob_ref[...])
    h = (x * lax.rsqrt(jnp.mean(x * x, axis=-1, keepdims=True) + EPS) * gffn_ref[...]).astype(BF16)
    g = _dot(h, wg_ref[...])
    u = _dot(h, wu_ref[...])
    a = (g / (1.0 + jnp.exp(-g)) * u).astype(BF16)
    x = x + _dot(a, wd_ref[...])
    y_ref[0] = x * lax.rsqrt(jnp.mean(x * x, axis=-1, keepdims=True) + EPS) * gfin_ref[...]


def _ffn(x, oaT, obT, p):
    B, S, _ = x.shape
    tm = TOK_TILE
    full = lambda a: pl.BlockSpec(a.shape, lambda b, i: (0,) * a.ndim, pipeline_mode=pl.Buffered(1))
    weights = (p["w_oa"], p["w_ob"], p["g_ffn"], p["w_gate"], p["w_up"], p["w_down"], p["g_final"])
    return pl.pallas_call(
        _ffn_kernel,
        grid=(B, S // tm),
        in_specs=[
            pl.BlockSpec((1, tm, D_MODEL), lambda b, i: (b, i, 0)),
            pl.BlockSpec((1, HA * DA, tm), lambda b, i: (b, 0, i)),
            pl.BlockSpec((1, HB, 1, DV, tm), lambda b, i: (b, 0, i, 0, 0)),
        ] + [full(w) for w in weights],
        out_specs=pl.BlockSpec((1, tm, D_MODEL), lambda b, i: (b, i, 0)),
        out_shape=jax.ShapeDtypeStruct(x.shape, F32),
        compiler_params=pltpu.CompilerParams(
            dimension_semantics=("parallel", "parallel"), vmem_limit_bytes=VMEM_LIMIT),
        name="ffn",
    )(x, oaT, obT, *weights)


def _rope_tables_t(seq, dim):
    inv = 1.0 / (ROPE_THETA ** (jnp.arange(0, dim, 2, dtype=F32) / dim))
    ang = jnp.arange(seq, dtype=F32)[:, None] * inv[None, :]
    return jnp.cos(ang).T, jnp.sin(ang).T


def _prep_params(g_mix, w_in, sink, cq_g, w_uq, ckv_g, w_ukv, w_o, g_ffn, w_gate, w_up, w_down, g_final):
    w_uq3 = w_uq.reshape(Q_RANK, HB, DN + DR)
    w_uq_pad = jnp.pad(w_uq3, ((0, 0), (0, 0), (0, LANE - DN - DR))).reshape(Q_RANK, HB * LANE)
    w_ukv3 = w_ukv.reshape(KV_RANK, HB, DN + DV)
    w_uk, w_uv = w_ukv3[..., :DN], w_ukv3[..., DN:]
    top = jnp.pad(w_uk, ((0, 0), (0, 0), (0, LANE - DN)))
    eye = jnp.pad(jnp.eye(DR, dtype=F32)[:, None, :], ((0, 0), (0, 0), (DN, LANE - DN - DR)))
    bot = jnp.broadcast_to(eye, (DR, HB, LANE))
    w_k = jnp.concatenate([top, bot], axis=0).reshape(KV_RANK + DR, HB * LANE)
    return {
        "g_mix": g_mix.reshape(1, D_MODEL),
        "w_inT": w_in.T.astype(BF16),
        "cq_g": cq_g.reshape(Q_RANK, 1),
        "w_uqT": w_uq_pad.T.astype(BF16),
        "ckv_g": ckv_g.reshape(KV_RANK, 1),
        "w_k": w_k.astype(BF16),
        "w_vT": w_uv.reshape(KV_RANK, HB * DV).T.astype(BF16),
        "sink": sink,
        "w_oa": w_o[:HA * DA].astype(BF16),
        "w_ob": w_o[HA * DA:].astype(BF16),
        "g_ffn": g_ffn.reshape(1, D_MODEL),
        "w_gate": w_gate.astype(BF16),
        "w_up": w_up.astype(BF16),
        "w_down": w_down.astype(BF16),
        "g_final": g_final.reshape(1, D_MODEL),
    }


def _trunk(x, p):
    S = x.shape[1]
    tabs = _rope_tables_t(S, DA) + _rope_tables_t(S, DR)
    qaT, ka, vaT, qT, k, vT = _proj(x, p, tabs)
    oaT = _swa(p["sink"], qaT, ka, vaT)
    obT = _mla(qT, k, vT)
    return _ffn(x, oaT, obT, p)


def kernel(x_prompt, x_sample, g_mix, w_in, sink, cq_g, w_uq, ckv_g, w_ukv, w_o, g_ffn,
           w_gate, w_up, w_down, g_final):
    p = _prep_params(g_mix[0], w_in[0], sink[0], cq_g[0], w_uq[0], ckv_g[0], w_ukv[0], w_o[0],
                     g_ffn[0], w_gate[0], w_up[0], w_down[0], g_final)
    return _trunk(x_prompt, p), _trunk(x_sample, p)
```

```python
import functools
import math

import jax
import jax.numpy as jnp
from jax import lax
from jax.experimental import pallas as pl
from jax.experimental.pallas import tpu as pltpu

D_MODEL = 1024
EPS = 1e-6
ROPE_THETA = 10000.0
WINDOW = 128
HA, KVA, DA = 8, 2, 64
GA = HA // KVA
HB, Q_RANK, KV_RANK, DN, DR, DV = 8, 384, 256, 64, 32, 64
D_IN = HA * DA + 2 * KVA * DA + Q_RANK + KV_RANK + DR
D_FF = int(math.ceil(8 * D_MODEL / 3 / 256) * 256)
NEG = -1e30

LANE = 128
V_ROWS = 128
LOG2E = math.log2(math.e)
SWA_Q_SCALE = DA ** -0.5 * LOG2E
MLA_Q_SCALE = (DN + DR) ** -0.5 * LOG2E
OFF_KA = HA * DA
OFF_VA = OFF_KA + KVA * DA
OFF_CQ = OFF_VA + KVA * DA
OFF_CKV = OFF_CQ + Q_RANK
OFF_KR = OFF_CKV + KV_RANK

TOK_TILE = 512
SWA_TILE = 1024
MLA_KEY_BLOCK = 1024
MLA_BLOCKS_PER_TRIP = 16
VMEM_LIMIT = 56 * 1024 * 1024

F32 = jnp.float32
BF16 = jnp.bfloat16


def _dot(a, b):
    return jnp.dot(a, b, preferred_element_type=F32)


def _dot_tn(a, b):
    return lax.dot_general(a, b, (((0,), (0,)), ((), ())), preferred_element_type=F32)


def _dot_nt(a, b):
    return lax.dot_general(a, b, (((1,), (1,)), ((), ())), preferred_element_type=F32)


def _rope_rows(x1, x2, c, s):
    return x1 * c - x2 * s, x2 * c + x1 * s


def _proj_kernel(x_ref, gmix_ref, winT_ref, cqg_ref, wuqT_ref, ckvg_ref, wk_ref, wvT_ref,
                 cosa_ref, sina_ref, cosr_ref, sinr_ref,
                 qaT_ref, ka_ref, vaT_ref, qT_ref, k_ref, vT_ref):
    x = x_ref[0]
    ms = jnp.mean(x * x, axis=-1, keepdims=True)
    h = (x * lax.rsqrt(ms + EPS) * gmix_ref[...]).astype(BF16)
    zT = _dot_nt(winT_ref[...], h)

    ca, sa = cosa_ref[...], sina_ref[...]
    cr, sr = cosr_ref[...], sinr_ref[...]
    half = DA // 2

    qaT_ref[0] = jnp.zeros(qaT_ref.shape[1:], BF16)
    for hq in range(HA):
        r = hq * DA
        o1, o2 = _rope_rows(zT[r:r + half], zT[r + half:r + DA], ca, sa)
        base = hq * LANE + (hq // GA) * DA
        qaT_ref[0, base:base + half, :] = (o1 * SWA_Q_SCALE).astype(BF16)
        qaT_ref[0, base + half:base + DA, :] = (o2 * SWA_Q_SCALE).astype(BF16)

    kaT = []
    for g in range(KVA):
        r = OFF_KA + g * DA
        kaT.extend(_rope_rows(zT[r:r + half], zT[r + half:r + DA], ca, sa))
    ka_ref[0] = jnp.concatenate(kaT, axis=0).T.astype(BF16)
    tm = zT.shape[1]
    ones_row = lax.broadcasted_iota(jnp.int32, (V_ROWS - DV, tm), 0) == 0
    for g in range(KVA):
        r = OFF_VA + g * DA
        vaT_ref[0, g * V_ROWS:g * V_ROWS + DA, :] = zT[r:r + DA].astype(BF16)
        vaT_ref[0, g * V_ROWS + DA:(g + 1) * V_ROWS, :] = ones_row.astype(BF16)

    cq = zT[OFF_CQ:OFF_CKV]
    cqn = (cq * lax.rsqrt(jnp.mean(cq * cq, axis=0, keepdims=True) + EPS) * cqg_ref[...]).astype(BF16)
    qT = _dot(wuqT_ref[...], cqn) * MLA_Q_SCALE
    qT_ref[0, :, 0] = qT.astype(BF16).reshape(HB, LANE, tm)
    hr = DR // 2
    for hh in range(HB):
        r = hh * LANE + DN
        o1, o2 = _rope_rows(qT[r:r + hr], qT[r + hr:r + DR], cr, sr)
        qT_ref[0, hh, 0, DN:DN + hr, :] = o1.astype(BF16)
        qT_ref[0, hh, 0, DN + hr:DN + DR, :] = o2.astype(BF16)

    ckv = zT[OFF_CKV:OFF_KR]
    ckvn = (ckv * lax.rsqrt(jnp.mean(ckv * ckv, axis=0, keepdims=True) + EPS) * ckvg_ref[...]).astype(BF16)
    krT = zT[OFF_KR:D_IN]
    k1, k2 = _rope_rows(krT[:hr], krT[hr:], cr, sr)
    kin = jnp.concatenate([ckvn, k1.astype(BF16), k2.astype(BF16)], axis=0)
    k_ref[0] = _dot_tn(kin, wk_ref[...]).astype(BF16)
    vT_ref[0, :, 0, :DV] = _dot(wvT_ref[...], ckvn).astype(BF16).reshape(HB, DV, tm)
    vT_ref[0, :, 0, DV:] = jnp.broadcast_to(ones_row.astype(BF16), (HB, V_ROWS - DV, tm))


def _proj(x, p, tabs):
    B, S, _ = x.shape
    tm = TOK_TILE
    nt = S // tm
    full = lambda a: pl.BlockSpec(a.shape, lambda b, i: (0,) * a.ndim)
    tab = lambda a: pl.BlockSpec((a.shape[0], tm), lambda b, i: (0, i))
    weights = (p["g_mix"], p["w_inT"], p["cq_g"], p["w_uqT"], p["ckv_g"], p["w_k"], p["w_vT"])
    out_shape = (
        jax.ShapeDtypeStruct((B, HA * LANE, S), BF16),
        jax.ShapeDtypeStruct((B, S, KVA * DA), BF16),
        jax.ShapeDtypeStruct((B, KVA * V_ROWS, S), BF16),
        jax.ShapeDtypeStruct((B, HB, nt, LANE, tm), BF16),
        jax.ShapeDtypeStruct((B, S, HB * LANE), BF16),
        jax.ShapeDtypeStruct((B, HB, nt, V_ROWS, tm), BF16),
    )
    out_specs = (
        pl.BlockSpec((1, HA * LANE, tm), lambda b, i: (b, 0, i)),
        pl.BlockSpec((1, tm, KVA * DA), lambda b, i: (b, i, 0)),
        pl.BlockSpec((1, KVA * V_ROWS, tm), lambda b, i: (b, 0, i)),
        pl.BlockSpec((1, HB, 1, LANE, tm), lambda b, i: (b, 0, i, 0, 0)),
        pl.BlockSpec((1, tm, HB * LANE), lambda b, i: (b, i, 0)),
        pl.BlockSpec((1, HB, 1, V_ROWS, tm), lambda b, i: (b, 0, i, 0, 0)),
    )
    return pl.pallas_call(
        _proj_kernel,
        grid=(B, nt),
        in_specs=[pl.BlockSpec((1, tm, D_MODEL), lambda b, i: (b, i, 0))]
        + [full(w) for w in weights] + [tab(t) for t in tabs],
        out_specs=out_specs,
        out_shape=out_shape,
        compiler_params=pltpu.CompilerParams(
            dimension_semantics=("parallel", "parallel"), vmem_limit_bytes=VMEM_LIMIT),
        name="proj",
    )(x, *weights, *tabs)


def _swa_kernel(sink_ref, qaT_ref, kp_ref, kc_ref, kn_ref, vp_ref, vc_ref, vn_ref, oT_ref, s_ref, *, seq):
    tq = qaT_ref.shape[2]
    q0 = pl.program_id(1) * tq
    kcat = jnp.concatenate([kp_ref[0], kc_ref[0], kn_ref[0]], axis=0)
    vcat = jnp.concatenate([vp_ref[0], vc_ref[0], vn_ref[0]], axis=1)
    nk = 3 * WINDOW
    wide = GA * WINDOW
    r = lax.broadcasted_iota(jnp.int32, (nk, wide), 0)
    lane = lax.broadcasted_iota(jnp.int32, (nk, wide), 1)
    d = r - (lane & (WINDOW - 1))
    band = (d >= 0) & (d <= 2 * WINDOW)
    head = lax.broadcasted_iota(jnp.int32, (1, wide), 1) // WINDOW
    units = [(c, g) for c in range(tq // WINDOW) for g in range(KVA)]

    def scores(slot, c, g):
        q4 = jnp.concatenate(
            [qaT_ref[0, (g * GA + h) * LANE:(g * GA + h + 1) * LANE, c * WINDOW:(c + 1) * WINDOW]
             for h in range(GA)], axis=1)
        valid = band
        if c in (0, tq // WINDOW - 1):
            kpos = q0 + (c - 1) * WINDOW + r
            valid = band & (kpos >= 0) & (kpos < seq)
        s = jnp.where(valid, _dot(kcat[c * WINDOW:c * WINDOW + nk], q4), NEG)
        s_ref[slot] = s
        return jnp.max(s, axis=0, keepdims=True)

    def finish(slot, c, g, cmax):
        sink = jnp.zeros((1, wide), F32)
        for h in range(GA):
            sink = jnp.where(head == h, sink_ref[g * GA + h] * LOG2E, sink)
        m = jnp.maximum(cmax, sink)
        p = jnp.exp2(s_ref[slot] - m).astype(BF16)
        pv = _dot(vcat[g * V_ROWS:(g + 1) * V_ROWS, c * WINDOW:c * WINDOW + nk], p)
        o = (pv[:DA] / (pv[DA:DA + 1] + jnp.exp2(sink - m))).astype(BF16)
        for h in range(GA):
            hq = g * GA + h
            oT_ref[0, hq * DA:(hq + 1) * DA, c * WINDOW:(c + 1) * WINDOW] = o[:, h * WINDOW:(h + 1) * WINDOW]

    cmax = scores(0, *units[0])
    for u, (c, g) in enumerate(units):
        nxt = scores((u + 1) % 2, *units[u + 1]) if u + 1 < len(units) else None
        finish(u % 2, c, g, cmax)
        cmax = nxt


def _swa(sink, qaT, ka, vaT):
    B, _, S = qaT.shape
    tq = SWA_TILE
    r = tq // WINDOW
    nb = S // WINDOW
    prev = lambda b, i, *_: (b, jnp.maximum(i * r - 1, 0), 0)
    nxt = lambda b, i, *_: (b, jnp.minimum(i * r + r, nb - 1), 0)
    prev_t = lambda b, i, *_: (b, 0, jnp.maximum(i * r - 1, 0))
    nxt_t = lambda b, i, *_: (b, 0, jnp.minimum(i * r + r, nb - 1))
    grid_spec = pltpu.PrefetchScalarGridSpec(
        num_scalar_prefetch=1,
        grid=(B, S // tq),
        in_specs=[
            pl.BlockSpec((1, HA * LANE, tq), lambda b, i, *_: (b, 0, i)),
            pl.BlockSpec((1, WINDOW, KVA * DA), prev),
            pl.BlockSpec((1, tq, KVA * DA), lambda b, i, *_: (b, i, 0)),
            pl.BlockSpec((1, WINDOW, KVA * DA), nxt),
            pl.BlockSpec((1, KVA * V_ROWS, WINDOW), prev_t),
            pl.BlockSpec((1, KVA * V_ROWS, tq), lambda b, i, *_: (b, 0, i)),
            pl.BlockSpec((1, KVA * V_ROWS, WINDOW), nxt_t),
        ],
        out_specs=pl.BlockSpec((1, HA * DA, tq), lambda b, i, *_: (b, 0, i)),
        scratch_shapes=[pltpu.VMEM((2, 3 * WINDOW, GA * WINDOW), F32)],
    )
    return pl.pallas_call(
        functools.partial(_swa_kernel, seq=S),
        grid_spec=grid_spec,
        out_shape=jax.ShapeDtypeStruct((B, HA * DA, S), BF16),
        compiler_params=pltpu.CompilerParams(
            dimension_semantics=("parallel", "parallel"), vmem_limit_bytes=VMEM_LIMIT),
        name="swa",
    )(sink, qaT, ka, ka, ka, vaT, vaT, vaT)


def _mla_kernel(qT_ref, k_ref, vT_ref, oT_ref, s_ref):
    nq, _, tq = qT_ref.shape[2:]
    nvb, nv, tv = vT_ref.shape[2:]
    tk = s_ref.shape[1]
    nkb = nvb * tv // tk
    vper = tk // tv

    def scores(slot, qi, j):
        kb = k_ref[0, j * tk:(j + 1) * tk, :]
        s = _dot(kb, qT_ref[0, 0, qi])
        s_ref[slot] = s
        return jnp.max(s, axis=0, keepdims=True)

    def softmax_pv(slot, j, cmax, m, acc):
        m_new = jnp.maximum(m, cmax)
        alpha = jnp.exp2(m - m_new)
        p = jnp.exp2(s_ref[slot] - m_new).astype(BF16)
        pv = _dot(vT_ref[0, 0, j * vper, :, :LANE], p[:LANE])
        pv += _dot(vT_ref[0, 0, j * vper, :, LANE:], p[LANE:tv])
        for c in range(1, vper):
            pv += _dot(vT_ref[0, 0, j * vper + c], p[c * tv:(c + 1) * tv])
        return m_new, alpha * acc + pv

    def q_tile(qi, cmax):
        m, acc = jnp.full((1, tq), -jnp.inf, F32), jnp.zeros((nv, tq), F32)
        for j in range(nkb):
            if j + 1 < nkb:
                nxt = scores((j + 1) % 2, qi, j + 1)
            else:
                nxt = scores(0, jnp.minimum(qi + 1, nq - 1), 0)
            m, acc = softmax_pv(j % 2, j, cmax, m, acc)
            cmax = nxt
        oT_ref[0, 0, qi] = (acc[:DV] / acc[DV:DV + 1]).astype(BF16)
        return cmax

    unroll = min(nq, max(1, MLA_BLOCKS_PER_TRIP // nkb))

    def q_tiles(t, cmax):
        for u in range(unroll):
            cmax = q_tile(t * unroll + u, cmax)
        return cmax

    assert nq % unroll == 0
    lax.fori_loop(0, nq // unroll, q_tiles, scores(0, 0, 0))


def _mla(qT, k, vT):
    B, _, nq, _, tq = qT.shape
    nvb, nv, tv = vT.shape[2:]
    S = nq * tq
    tk = MLA_KEY_BLOCK
    assert tk % tv == 0 and S % (2 * tk) == 0
    return pl.pallas_call(
        _mla_kernel,
        grid=(B, HB),
        in_specs=[
            pl.BlockSpec((1, 1, nq, LANE, tq), lambda b, h: (b, h, 0, 0, 0)),
            pl.BlockSpec((1, S, LANE), lambda b, h: (b, 0, h)),
            pl.BlockSpec((1, 1, nvb, nv, tv), lambda b, h: (b, h, 0, 0, 0)),
        ],
        out_specs=pl.BlockSpec((1, 1, nq, DV, tq), lambda b, h: (b, h, 0, 0, 0)),
        out_shape=jax.ShapeDtypeStruct((B, HB, nq, DV, tq), BF16),
        scratch_shapes=[pltpu.VMEM((2, tk, tq), F32)],
        compiler_params=pltpu.CompilerParams(
            dimension_semantics=("parallel", "parallel"), vmem_limit_bytes=VMEM_LIMIT),
        name="mla",
    )(qT, k, vT)


def _ffn_kernel(x_ref, oaT_ref, obT_ref, woa_ref, wob_ref, gffn_ref, wg_ref, wu_ref, wd_ref, gfin_ref, y_ref):
    x = x_ref[0]
    obT = obT_ref[0, :, 0].reshape(HB * DV, -1)
    x = x + _dot_tn(oaT_ref[0], woa_ref[...]) + _dot_tn(obT, wob_ref[...])
    h = (x * lax.rsqrt(jnp.mean(x * x, axis=-1, keepdims=True) + EPS) * gffn_ref[...]).astype(BF16)
    g = _dot(h, wg_ref[...])
    u = _dot(h, wu_ref[...])
    a = (g / (1.0 + jnp.exp(-g)) * u).astype(BF16)
    x = x + _dot(a, wd_ref[...])
    y_ref[0] = x * lax.rsqrt(jnp.mean(x * x, axis=-1, keepdims=True) + EPS) * gfin_ref[...]


def _ffn(x, oaT, obT, p):
    B, S, _ = x.shape
    tm = TOK_TILE
    full = lambda a: pl.BlockSpec(a.shape, lambda b, i: (0,) * a.ndim, pipeline_mode=pl.Buffered(1))
    weights = (p["w_oa"], p["w_ob"], p["g_ffn"], p["w_gate"], p["w_up"], p["w_down"], p["g_final"])
    return pl.pallas_call(
        _ffn_kernel,
        grid=(B, S // tm),
        in_specs=[
            pl.BlockSpec((1, tm, D_MODEL), lambda b, i: (b, i, 0)),
            pl.BlockSpec((1, HA * DA, tm), lambda b, i: (b, 0, i)),
            pl.BlockSpec((1, HB, 1, DV, tm), lambda b, i: (b, 0, i, 0, 0)),
        ] + [full(w) for w in weights],
        out_specs=pl.BlockSpec((1, tm, D_MODEL), lambda b, i: (b, i, 0)),
        out_shape=jax.ShapeDtypeStruct(x.shape, F32),
        compiler_params=pltpu.CompilerParams(
            dimension_semantics=("parallel", "parallel"), vmem_limit_bytes=VMEM_LIMIT),
        name="ffn",
    )(x, oaT, obT, *weights)


def _rope_tables_t(seq, dim):
    inv = 1.0 / (ROPE_THETA ** (jnp.arange(0, dim, 2, dtype=F32) / dim))
    ang = jnp.arange(seq, dtype=F32)[:, None] * inv[None, :]
    return jnp.cos(ang).T, jnp.sin(ang).T


def _prep_params(g_mix, w_in, sink, cq_g, w_uq, ckv_g, w_ukv, w_o, g_ffn, w_gate, w_up, w_down, g_final):
    w_uq3 = w_uq.reshape(Q_RANK, HB, DN + DR)
    w_uq_pad = jnp.pad(w_uq3, ((0, 0), (0, 0), (0, LANE - DN - DR))).reshape(Q_RANK, HB * LANE)
    w_ukv3 = w_ukv.reshape(KV_RANK, HB, DN + DV)
    w_uk, w_uv = w_ukv3[..., :DN], w_ukv3[..., DN:]
    top = jnp.pad(w_uk, ((0, 0), (0, 0), (0, LANE - DN)))
    eye = jnp.pad(jnp.eye(DR, dtype=F32)[:, None, :], ((0, 0), (0, 0), (DN, LANE - DN - DR)))
    bot = jnp.broadcast_to(eye, (DR, HB, LANE))
    w_k = jnp.concatenate([top, bot], axis=0).reshape(KV_RANK + DR, HB * LANE)
    return {
        "g_mix": g_mix.reshape(1, D_MODEL),
        "w_inT": w_in.T.astype(BF16),
        "cq_g": cq_g.reshape(Q_RANK, 1),
        "w_uqT": w_uq_pad.T.astype(BF16),
        "ckv_g": ckv_g.reshape(KV_RANK, 1),
        "w_k": w_k.astype(BF16),
        "w_vT": w_uv.reshape(KV_RANK, HB * DV).T.astype(BF16),
        "sink": sink,
        "w_oa": w_o[:HA * DA].astype(BF16),
        "w_ob": w_o[HA * DA:].astype(BF16),
        "g_ffn": g_ffn.reshape(1, D_MODEL),
        "w_gate": w_gate.astype(BF16),
        "w_up": w_up.astype(BF16),
        "w_down": w_down.astype(BF16),
        "g_final": g_final.reshape(1, D_MODEL),
    }


def _trunk(x, p):
    S = x.shape[1]
    tabs = _rope_tables_t(S, DA) + _rope_tables_t(S, DR)
    qaT, ka, vaT, qT, k, vT = _proj(x, p, tabs)
    oaT = _swa(p["sink"], qaT, ka, vaT)
    obT = _mla(qT, k, vT)
    return _ffn(x, oaT, obT, p)


def kernel(x_prompt, x_sample, g_mix, w_in, sink, cq_g, w_uq, ckv_g, w_ukv, w_o, g_ffn,
           w_gate, w_up, w_down, g_final):
    p = _prep_params(g_mix[0], w_in[0], sink[0], cq_g[0], w_uq[0], ckv_g[0], w_ukv[0], w_o[0],
                     g_ffn[0], w_gate[0], w_up[0], w_down[0], g_final)
    return _trunk(x_prompt, p), _trunk(x_sample, p)
```

```python
import functools
import math

import jax
import jax.numpy as jnp
from jax import lax
from jax.experimental import pallas as pl
from jax.experimental.pallas import tpu as pltpu

D_MODEL = 1024
EPS = 1e-6
ROPE_THETA = 10000.0
WINDOW = 128
HA, KVA, DA = 8, 2, 64
GA = HA // KVA
HB, Q_RANK, KV_RANK, DN, DR, DV = 8, 384, 256, 64, 32, 64
D_IN = HA * DA + 2 * KVA * DA + Q_RANK + KV_RANK + DR
D_FF = int(math.ceil(8 * D_MODEL / 3 / 256) * 256)
NEG = -1e30

LANE = 128
V_ROWS = 128
LOG2E = math.log2(math.e)
SWA_Q_SCALE = DA ** -0.5 * LOG2E
MLA_Q_SCALE = (DN + DR) ** -0.5 * LOG2E
OFF_KA = HA * DA
OFF_VA = OFF_KA + KVA * DA
OFF_CQ = OFF_VA + KVA * DA
OFF_CKV = OFF_CQ + Q_RANK
OFF_KR = OFF_CKV + KV_RANK

TOK_TILE = 512
SWA_TILE = 1024
MLA_Q_TILE = 256
MLA_KEY_BLOCK = 2048
MLA_SCORE_ROWS = 1024
MLA_BLOCKS_PER_TRIP = 16
VMEM_LIMIT = 56 * 1024 * 1024

F32 = jnp.float32
BF16 = jnp.bfloat16


def _dot(a, b):
    return jnp.dot(a, b, preferred_element_type=F32)


def _dot_tn(a, b):
    return lax.dot_general(a, b, (((0,), (0,)), ((), ())), preferred_element_type=F32)


def _dot_nt(a, b):
    return lax.dot_general(a, b, (((1,), (1,)), ((), ())), preferred_element_type=F32)


def _rope_rows(x1, x2, c, s):
    return x1 * c - x2 * s, x2 * c + x1 * s


def _proj_kernel(x_ref, gmix_ref, winT_ref, cqg_ref, wuqT_ref, ckvg_ref, wk_ref, wvT_ref,
                 cosa_ref, sina_ref, cosr_ref, sinr_ref,
                 qaT_ref, ka_ref, vaT_ref, qT_ref, k_ref, vT_ref):
    x = x_ref[0]
    ms = jnp.mean(x * x, axis=-1, keepdims=True)
    h = (x * lax.rsqrt(ms + EPS) * gmix_ref[...]).astype(BF16)
    zT = _dot_nt(winT_ref[...], h)

    ca, sa = cosa_ref[...], sina_ref[...]
    cr, sr = cosr_ref[...], sinr_ref[...]
    half = DA // 2

    qaT_ref[0] = jnp.zeros(qaT_ref.shape[1:], BF16)
    for hq in range(HA):
        r = hq * DA
        o1, o2 = _rope_rows(zT[r:r + half], zT[r + half:r + DA], ca, sa)
        base = hq * LANE + (hq // GA) * DA
        qaT_ref[0, base:base + half, :] = (o1 * SWA_Q_SCALE).astype(BF16)
        qaT_ref[0, base + half:base + DA, :] = (o2 * SWA_Q_SCALE).astype(BF16)

    kaT = []
    for g in range(KVA):
        r = OFF_KA + g * DA
        kaT.extend(_rope_rows(zT[r:r + half], zT[r + half:r + DA], ca, sa))
    ka_ref[0] = jnp.concatenate(kaT, axis=0).T.astype(BF16)
    tm = zT.shape[1]
    ones_row = lax.broadcasted_iota(jnp.int32, (V_ROWS - DV, tm), 0) == 0
    for g in range(KVA):
        r = OFF_VA + g * DA
        vaT_ref[0, g * V_ROWS:g * V_ROWS + DA, :] = zT[r:r + DA].astype(BF16)
        vaT_ref[0, g * V_ROWS + DA:(g + 1) * V_ROWS, :] = ones_row.astype(BF16)

    cq = zT[OFF_CQ:OFF_CKV]
    cqn = (cq * lax.rsqrt(jnp.mean(cq * cq, axis=0, keepdims=True) + EPS) * cqg_ref[...]).astype(BF16)
    qT = _dot(wuqT_ref[...], cqn) * MLA_Q_SCALE
    qT_ref[0, :, 0] = qT.astype(BF16).reshape(HB, LANE, tm)
    hr = DR // 2
    for hh in range(HB):
        r = hh * LANE + DN
        o1, o2 = _rope_rows(qT[r:r + hr], qT[r + hr:r + DR], cr, sr)
        qT_ref[0, hh, 0, DN:DN + hr, :] = o1.astype(BF16)
        qT_ref[0, hh, 0, DN + hr:DN + DR, :] = o2.astype(BF16)

    ckv = zT[OFF_CKV:OFF_KR]
    ckvn = (ckv * lax.rsqrt(jnp.mean(ckv * ckv, axis=0, keepdims=True) + EPS) * ckvg_ref[...]).astype(BF16)
    krT = zT[OFF_KR:D_IN]
    k1, k2 = _rope_rows(krT[:hr], krT[hr:], cr, sr)
    kin = jnp.concatenate([ckvn, k1.astype(BF16), k2.astype(BF16)], axis=0)
    k_ref[0] = _dot_tn(kin, wk_ref[...]).astype(BF16)
    vT_ref[0, :, 0, :DV] = _dot(wvT_ref[...], ckvn).astype(BF16).reshape(HB, DV, tm)
    vT_ref[0, :, 0, DV:] = jnp.broadcast_to(ones_row.astype(BF16), (HB, V_ROWS - DV, tm))


def _proj(x, p, tabs):
    B, S, _ = x.shape
    tm = TOK_TILE
    nt = S // tm
    full = lambda a: pl.BlockSpec(a.shape, lambda b, i: (0,) * a.ndim)
    tab = lambda a: pl.BlockSpec((a.shape[0], tm), lambda b, i: (0, i))
    weights = (p["g_mix"], p["w_inT"], p["cq_g"], p["w_uqT"], p["ckv_g"], p["w_k"], p["w_vT"])
    out_shape = (
        jax.ShapeDtypeStruct((B, HA * LANE, S), BF16),
        jax.ShapeDtypeStruct((B, S, KVA * DA), BF16),
        jax.ShapeDtypeStruct((B, KVA * V_ROWS, S), BF16),
        jax.ShapeDtypeStruct((B, HB, nt, LANE, tm), BF16),
        jax.ShapeDtypeStruct((B, S, HB * LANE), BF16),
        jax.ShapeDtypeStruct((B, HB, nt, V_ROWS, tm), BF16),
    )
    out_specs = (
        pl.BlockSpec((1, HA * LANE, tm), lambda b, i: (b, 0, i)),
        pl.BlockSpec((1, tm, KVA * DA), lambda b, i: (b, i, 0)),
        pl.BlockSpec((1, KVA * V_ROWS, tm), lambda b, i: (b, 0, i)),
        pl.BlockSpec((1, HB, 1, LANE, tm), lambda b, i: (b, 0, i, 0, 0)),
        pl.BlockSpec((1, tm, HB * LANE), lambda b, i: (b, i, 0)),
        pl.BlockSpec((1, HB, 1, V_ROWS, tm), lambda b, i: (b, 0, i, 0, 0)),
    )
    return pl.pallas_call(
        _proj_kernel,
        grid=(B, nt),
        in_specs=[pl.BlockSpec((1, tm, D_MODEL), lambda b, i: (b, i, 0))]
        + [full(w) for w in weights] + [tab(t) for t in tabs],
        out_specs=out_specs,
        out_shape=out_shape,
        compiler_params=pltpu.CompilerParams(
            dimension_semantics=("parallel", "parallel"), vmem_limit_bytes=VMEM_LIMIT),
        name="proj",
    )(x, *weights, *tabs)


def _swa_kernel(sink_ref, qaT_ref, kp_ref, kc_ref, kn_ref, vp_ref, vc_ref, vn_ref, oT_ref, s_ref, *, seq):
    tq = qaT_ref.shape[2]
    q0 = pl.program_id(1) * tq
    kcat = jnp.concatenate([kp_ref[0], kc_ref[0], kn_ref[0]], axis=0)
    vcat = jnp.concatenate([vp_ref[0], vc_ref[0], vn_ref[0]], axis=1)
    nk = 3 * WINDOW
    wide = GA * WINDOW
    r = lax.broadcasted_iota(jnp.int32, (nk, wide), 0)
    lane = lax.broadcasted_iota(jnp.int32, (nk, wide), 1)
    d = r - (lane & (WINDOW - 1))
    band = (d >= 0) & (d <= 2 * WINDOW)
    head = lax.broadcasted_iota(jnp.int32, (1, wide), 1) // WINDOW
    units = [(c, g) for c in range(tq // WINDOW) for g in range(KVA)]

    def scores(slot, c, g):
        q4 = jnp.concatenate(
            [qaT_ref[0, (g * GA + h) * LANE:(g * GA + h + 1) * LANE, c * WINDOW:(c + 1) * WINDOW]
             for h in range(GA)], axis=1)
        valid = band
        if c in (0, tq // WINDOW - 1):
            kpos = q0 + (c - 1) * WINDOW + r
            valid = band & (kpos >= 0) & (kpos < seq)
        s = jnp.where(valid, _dot(kcat[c * WINDOW:c * WINDOW + nk], q4), NEG)
        s_ref[slot] = s
        return jnp.max(s, axis=0, keepdims=True)

    def finish(slot, c, g, cmax):
        sink = jnp.zeros((1, wide), F32)
        for h in range(GA):
            sink = jnp.where(head == h, sink_ref[g * GA + h] * LOG2E, sink)
        m = jnp.maximum(cmax, sink)
        p = jnp.exp2(s_ref[slot] - m).astype(BF16)
        pv = _dot(vcat[g * V_ROWS:(g + 1) * V_ROWS, c * WINDOW:c * WINDOW + nk], p)
        o = (pv[:DA] / (pv[DA:DA + 1] + jnp.exp2(sink - m))).astype(BF16)
        for h in range(GA):
            hq = g * GA + h
            oT_ref[0, hq * DA:(hq + 1) * DA, c * WINDOW:(c + 1) * WINDOW] = o[:, h * WINDOW:(h + 1) * WINDOW]

    cmax = scores(0, *units[0])
    for u, (c, g) in enumerate(units):
        nxt = scores((u + 1) % 2, *units[u + 1]) if u + 1 < len(units) else None
        finish(u % 2, c, g, cmax)
        cmax = nxt


def _swa(sink, qaT, ka, vaT):
    B, _, S = qaT.shape
    tq = SWA_TILE
    r = tq // WINDOW
    nb = S // WINDOW
    prev = lambda b, i, *_: (b, jnp.maximum(i * r - 1, 0), 0)
    nxt = lambda b, i, *_: (b, jnp.minimum(i * r + r, nb - 1), 0)
    prev_t = lambda b, i, *_: (b, 0, jnp.maximum(i * r - 1, 0))
    nxt_t = lambda b, i, *_: (b, 0, jnp.minimum(i * r + r, nb - 1))
    grid_spec = pltpu.PrefetchScalarGridSpec(
        num_scalar_prefetch=1,
        grid=(B, S // tq),
        in_specs=[
            pl.BlockSpec((1, HA * LANE, tq), lambda b, i, *_: (b, 0, i)),
            pl.BlockSpec((1, WINDOW, KVA * DA), prev),
            pl.BlockSpec((1, tq, KVA * DA), lambda b, i, *_: (b, i, 0)),
            pl.BlockSpec((1, WINDOW, KVA * DA), nxt),
            pl.BlockSpec((1, KVA * V_ROWS, WINDOW), prev_t),
            pl.BlockSpec((1, KVA * V_ROWS, tq), lambda b, i, *_: (b, 0, i)),
            pl.BlockSpec((1, KVA * V_ROWS, WINDOW), nxt_t),
        ],
        out_specs=pl.BlockSpec((1, HA * DA, tq), lambda b, i, *_: (b, 0, i)),
        scratch_shapes=[pltpu.VMEM((2, 3 * WINDOW, GA * WINDOW), F32)],
    )
    return pl.pallas_call(
        functools.partial(_swa_kernel, seq=S),
        grid_spec=grid_spec,
        out_shape=jax.ShapeDtypeStruct((B, HA * DA, S), BF16),
        compiler_params=pltpu.CompilerParams(
            dimension_semantics=("parallel", "parallel"), vmem_limit_bytes=VMEM_LIMIT),
        name="swa",
    )(sink, qaT, ka, ka, ka, vaT, vaT, vaT)


def _mla_kernel(qT_ref, k_ref, vT_ref, oT_ref, s_ref):
    nt, _, tw = qT_ref.shape[2:]
    nvb, nv, tv = vT_ref.shape[2:]
    tk, tq = s_ref.shape[1:]
    nkb = nvb * tv // tk
    vper = tk // tv
    sub = tw // tq
    nq = nt * sub

    def scores(slot, tile, half, j):
        q = qT_ref[0, 0, tile, :, half * tq:(half + 1) * tq]
        cmax = None
        for r in range(0, tk, MLA_SCORE_ROWS):
            s = _dot(k_ref[0, j * tk + r:j * tk + r + MLA_SCORE_ROWS, :], q)
            s_ref[slot, r:r + MLA_SCORE_ROWS] = s
            cm = jnp.max(s, axis=0, keepdims=True)
            cmax = cm if cmax is None else jnp.maximum(cmax, cm)
        return cmax

    def softmax_pv(slot, j, cmax, m, acc):
        m_new = jnp.maximum(m, cmax)
        alpha = jnp.exp2(m - m_new)
        p = jnp.exp2(s_ref[slot] - m_new).astype(BF16)
        pv = _dot(vT_ref[0, 0, j * vper, :, :LANE], p[:LANE])
        pv += _dot(vT_ref[0, 0, j * vper, :, LANE:], p[LANE:tv])
        for c in range(1, vper):
            pv += _dot(vT_ref[0, 0, j * vper + c], p[c * tv:(c + 1) * tv])
        return m_new, alpha * acc + pv

    unroll = min(nq, max(sub, MLA_BLOCKS_PER_TRIP // nkb))
    assert nq % unroll == 0 and unroll % sub == 0
    tiles_per_trip = unroll // sub

    def q_tiles(t, cmax):
        for u in range(unroll):
            tile, half = t * tiles_per_trip + u // sub, u % sub
            m, acc = jnp.full((1, tq), -jnp.inf, F32), jnp.zeros((nv, tq), F32)
            for j in range(nkb):
                if j + 1 < nkb:
                    nxt = scores((j + 1) % 2, tile, half, j + 1)
                elif u + 1 < unroll:
                    nxt = scores(0, t * tiles_per_trip + (u + 1) // sub, (u + 1) % sub, 0)
                else:
                    nxt = scores(0, jnp.minimum((t + 1) * tiles_per_trip, nt - 1), 0, 0)
                m, acc = softmax_pv(j % 2, j, cmax, m, acc)
                cmax = nxt
            oT_ref[0, 0, tile, :, half * tq:(half + 1) * tq] = (acc[:DV] / acc[DV:DV + 1]).astype(BF16)
        return cmax

    lax.fori_loop(0, nq // unroll, q_tiles, scores(0, 0, 0, 0))


def _mla(qT, k, vT):
    B, _, nq, _, tq = qT.shape
    nvb, nv, tv = vT.shape[2:]
    S = nq * tq
    tk = MLA_KEY_BLOCK
    assert tk % tv == 0 and S % (2 * tk) == 0
    return pl.pallas_call(
        _mla_kernel,
        grid=(B, HB),
        in_specs=[
            pl.BlockSpec((1, 1, nq, LANE, tq), lambda b, h: (b, h, 0, 0, 0)),
            pl.BlockSpec((1, S, LANE), lambda b, h: (b, 0, h)),
            pl.BlockSpec((1, 1, nvb, nv, tv), lambda b, h: (b, h, 0, 0, 0)),
        ],
        out_specs=pl.BlockSpec((1, 1, nq, DV, tq), lambda b, h: (b, h, 0, 0, 0)),
        out_shape=jax.ShapeDtypeStruct((B, HB, nq, DV, tq), BF16),
        scratch_shapes=[pltpu.VMEM((2, tk, MLA_Q_TILE), F32)],
        compiler_params=pltpu.CompilerParams(
            dimension_semantics=("parallel", "parallel"), vmem_limit_bytes=VMEM_LIMIT),
        name="mla",
    )(qT, k, vT)


def _ffn_kernel(x_ref, oaT_ref, obT_ref, woa_ref, wob_ref, gffn_ref, wg_ref, wu_ref, wd_ref, gfin_ref, y_ref):
    x = x_ref[0]
    obT = obT_ref[0, :, 0].reshape(HB * DV, -1)
    x = x + _dot_tn(oaT_ref[0], woa_ref[...]) + _dot_tn(obT, wob_ref[...])
    h = (x * lax.rsqrt(jnp.mean(x * x, axis=-1, keepdims=True) + EPS) * gffn_ref[...]).astype(BF16)
    g = _dot(h, wg_ref[...])
    u = _dot(h, wu_ref[...])
    a = (g / (1.0 + jnp.exp(-g)) * u).astype(BF16)
    x = x + _dot(a, wd_ref[...])
    y_ref[0] = x * lax.rsqrt(jnp.mean(x * x, axis=-1, keepdims=True) + EPS) * gfin_ref[...]


def _ffn(x, oaT, obT, p):
    B, S, _ = x.shape
    tm = TOK_TILE
    full = lambda a: pl.BlockSpec(a.shape, lambda b, i: (0,) * a.ndim, pipeline_mode=pl.Buffered(1))
    weights = (p["w_oa"], p["w_ob"], p["g_ffn"], p["w_gate"], p["w_up"], p["w_down"], p["g_final"])
    return pl.pallas_call(
        _ffn_kernel,
        grid=(B, S // tm),
        in_specs=[
            pl.BlockSpec((1, tm, D_MODEL), lambda b, i: (b, i, 0)),
            pl.BlockSpec((1, HA * DA, tm), lambda b, i: (b, 0, i)),
            pl.BlockSpec((1, HB, 1, DV, tm), lambda b, i: (b, 0, i, 0, 0)),
        ] + [full(w) for w in weights],
        out_specs=pl.BlockSpec((1, tm, D_MODEL), lambda b, i: (b, i, 0)),
        out_shape=jax.ShapeDtypeStruct(x.shape, F32),
        compiler_params=pltpu.CompilerParams(
            dimension_semantics=("parallel", "parallel"), vmem_limit_bytes=VMEM_LIMIT),
        name="ffn",
    )(x, oaT, obT, *weights)


def _rope_tables_t(seq, dim):
    inv = 1.0 / (ROPE_THETA ** (jnp.arange(0, dim, 2, dtype=F32) / dim))
    ang = jnp.arange(seq, dtype=F32)[:, None] * inv[None, :]
    return jnp.cos(ang).T, jnp.sin(ang).T


def _prep_params(g_mix, w_in, sink, cq_g, w_uq, ckv_g, w_ukv, w_o, g_ffn, w_gate, w_up, w_down, g_final):
    w_uq3 = w_uq.reshape(Q_RANK, HB, DN + DR)
    w_uq_pad = jnp.pad(w_uq3, ((0, 0), (0, 0), (0, LANE - DN - DR))).reshape(Q_RANK, HB * LANE)
    w_ukv3 = w_ukv.reshape(KV_RANK, HB, DN + DV)
    w_uk, w_uv = w_ukv3[..., :DN], w_ukv3[..., DN:]
    top = jnp.pad(w_uk, ((0, 0), (0, 0), (0, LANE - DN)))
    eye = jnp.pad(jnp.eye(DR, dtype=F32)[:, None, :], ((0, 0), (0, 0), (DN, LANE - DN - DR)))
    bot = jnp.broadcast_to(eye, (DR, HB, LANE))
    w_k = jnp.concatenate([top, bot], axis=0).reshape(KV_RANK + DR, HB * LANE)
    return {
        "g_mix": g_mix.reshape(1, D_MODEL),
        "w_inT": w_in.T.astype(BF16),
        "cq_g": cq_g.reshape(Q_RANK, 1),
        "w_uqT": w_uq_pad.T.astype(BF16),
        "ckv_g": ckv_g.reshape(KV_RANK, 1),
        "w_k": w_k.astype(BF16),
        "w_vT": w_uv.reshape(KV_RANK, HB * DV).T.astype(BF16),
        "sink": sink,
        "w_oa": w_o[:HA * DA].astype(BF16),
        "w_ob": w_o[HA * DA:].astype(BF16),
        "g_ffn": g_ffn.reshape(1, D_MODEL),
        "w_gate": w_gate.astype(BF16),
        "w_up": w_up.astype(BF16),
        "w_down": w_down.astype(BF16),
        "g_final": g_final.reshape(1, D_MODEL),
    }


def _trunk(x, p):
    S = x.shape[1]
    tabs = _rope_tables_t(S, DA) + _rope_tables_t(S, DR)
    qaT, ka, vaT, qT, k, vT = _proj(x, p, tabs)
    oaT = _swa(p["sink"], qaT, ka, vaT)
    obT = _mla(qT, k, vT)
    return _ffn(x, oaT, obT, p)


def kernel(x_prompt, x_sample, g_mix, w_in, sink, cq_g, w_uq, ckv_g, w_ukv, w_o, g_ffn,
           w_gate, w_up, w_down, g_final):
    p = _prep_params(g_mix[0], w_in[0], sink[0], cq_g[0], w_uq[0], ckv_g[0], w_ukv[0], w_o[0],
                     g_ffn[0], w_gate[0], w_up[0], w_down[0], g_final)
    return _trunk(x_prompt, p), _trunk(x_sample, p)
```

```python
import functools
import math

import jax
import jax.numpy as jnp
from jax import lax
from jax.experimental import pallas as pl
from jax.experimental.pallas import tpu as pltpu

D_MODEL = 1024
EPS = 1e-6
ROPE_THETA = 10000.0
WINDOW = 128
HA, KVA, DA = 8, 2, 64
GA = HA // KVA
HB, Q_RANK, KV_RANK, DN, DR, DV = 8, 384, 256, 64, 32, 64
D_IN = HA * DA + 2 * KVA * DA + Q_RANK + KV_RANK + DR
D_FF = int(math.ceil(8 * D_MODEL / 3 / 256) * 256)
NEG = -1e30

LANE = 128
V_ROWS = 128
LOG2E = math.log2(math.e)
SWA_Q_SCALE = DA ** -0.5 * LOG2E
MLA_Q_SCALE = (DN + DR) ** -0.5 * LOG2E
OFF_KA = HA * DA
OFF_VA = OFF_KA + KVA * DA
OFF_CQ = OFF_VA + KVA * DA
OFF_CKV = OFF_CQ + Q_RANK
OFF_KR = OFF_CKV + KV_RANK

TOK_TILE = 512
SWA_TILE = 1024
MLA_Q_TILE = 256
MLA_KEY_BLOCK = 2048
MLA_SCORE_ROWS = 1024
MLA_BLOCKS_PER_TRIP = 32
VMEM_LIMIT = 56 * 1024 * 1024

F32 = jnp.float32
BF16 = jnp.bfloat16


def _dot(a, b):
    return jnp.dot(a, b, preferred_element_type=F32)


def _dot_tn(a, b):
    return lax.dot_general(a, b, (((0,), (0,)), ((), ())), preferred_element_type=F32)


def _dot_nt(a, b):
    return lax.dot_general(a, b, (((1,), (1,)), ((), ())), preferred_element_type=F32)


def _rope_rows(x1, x2, c, s):
    return x1 * c - x2 * s, x2 * c + x1 * s


def _proj_kernel(x_ref, gmix_ref, winT_ref, cqg_ref, wuqT_ref, ckvg_ref, wk_ref, wvT_ref,
                 cosa_ref, sina_ref, cosr_ref, sinr_ref,
                 qaT_ref, ka_ref, vaT_ref, qT_ref, k_ref, vT_ref):
    x = x_ref[0]
    ms = jnp.mean(x * x, axis=-1, keepdims=True)
    h = (x * lax.rsqrt(ms + EPS) * gmix_ref[...]).astype(BF16)
    zT = _dot_nt(winT_ref[...], h)

    ca, sa = cosa_ref[...], sina_ref[...]
    cr, sr = cosr_ref[...], sinr_ref[...]
    half = DA // 2

    qaT_ref[0] = jnp.zeros(qaT_ref.shape[1:], BF16)
    for hq in range(HA):
        r = hq * DA
        o1, o2 = _rope_rows(zT[r:r + half], zT[r + half:r + DA], ca, sa)
        base = hq * LANE + (hq // GA) * DA
        qaT_ref[0, base:base + half, :] = (o1 * SWA_Q_SCALE).astype(BF16)
        qaT_ref[0, base + half:base + DA, :] = (o2 * SWA_Q_SCALE).astype(BF16)

    kaT = []
    for g in range(KVA):
        r = OFF_KA + g * DA
        kaT.extend(_rope_rows(zT[r:r + half], zT[r + half:r + DA], ca, sa))
    ka_ref[0] = jnp.concatenate(kaT, axis=0).T.astype(BF16)
    tm = zT.shape[1]
    ones_row = lax.broadcasted_iota(jnp.int32, (V_ROWS - DV, tm), 0) == 0
    for g in range(KVA):
        r = OFF_VA + g * DA
        vaT_ref[0, g * V_ROWS:g * V_ROWS + DA, :] = zT[r:r + DA].astype(BF16)
        vaT_ref[0, g * V_ROWS + DA:(g + 1) * V_ROWS, :] = ones_row.astype(BF16)

    cq = zT[OFF_CQ:OFF_CKV]
    cqn = (cq * lax.rsqrt(jnp.mean(cq * cq, axis=0, keepdims=True) + EPS) * cqg_ref[...]).astype(BF16)
    qT = _dot(wuqT_ref[...], cqn) * MLA_Q_SCALE
    qT_ref[0, :, 0] = qT.astype(BF16).reshape(HB, LANE, tm)
    hr = DR // 2
    for hh in range(HB):
        r = hh * LANE + DN
        o1, o2 = _rope_rows(qT[r:r + hr], qT[r + hr:r + DR], cr, sr)
        qT_ref[0, hh, 0, DN:DN + hr, :] = o1.astype(BF16)
        qT_ref[0, hh, 0, DN + hr:DN + DR, :] = o2.astype(BF16)

    ckv = zT[OFF_CKV:OFF_KR]
    ckvn = (ckv * lax.rsqrt(jnp.mean(ckv * ckv, axis=0, keepdims=True) + EPS) * ckvg_ref[...]).astype(BF16)
    krT = zT[OFF_KR:D_IN]
    k1, k2 = _rope_rows(krT[:hr], krT[hr:], cr, sr)
    kin = jnp.concatenate([ckvn, k1.astype(BF16), k2.astype(BF16)], axis=0)
    k_ref[0] = _dot_tn(kin, wk_ref[...]).astype(BF16)
    vT_ref[0, :, 0, :DV] = _dot(wvT_ref[...], ckvn).astype(BF16).reshape(HB, DV, tm)
    vT_ref[0, :, 0, DV:] = jnp.broadcast_to(ones_row.astype(BF16), (HB, V_ROWS - DV, tm))


def _proj(x, p, tabs):
    B, S, _ = x.shape
    tm = TOK_TILE
    nt = S // tm
    full = lambda a: pl.BlockSpec(a.shape, lambda b, i: (0,) * a.ndim)
    tab = lambda a: pl.BlockSpec((a.shape[0], tm), lambda b, i: (0, i))
    weights = (p["g_mix"], p["w_inT"], p["cq_g"], p["w_uqT"], p["ckv_g"], p["w_k"], p["w_vT"])
    out_shape = (
        jax.ShapeDtypeStruct((B, HA * LANE, S), BF16),
        jax.ShapeDtypeStruct((B, S, KVA * DA), BF16),
        jax.ShapeDtypeStruct((B, KVA * V_ROWS, S), BF16),
        jax.ShapeDtypeStruct((B, HB, nt, LANE, tm), BF16),
        jax.ShapeDtypeStruct((B, S, HB * LANE), BF16),
        jax.ShapeDtypeStruct((B, HB, nt, V_ROWS, tm), BF16),
    )
    out_specs = (
        pl.BlockSpec((1, HA * LANE, tm), lambda b, i: (b, 0, i)),
        pl.BlockSpec((1, tm, KVA * DA), lambda b, i: (b, i, 0)),
        pl.BlockSpec((1, KVA * V_ROWS, tm), lambda b, i: (b, 0, i)),
        pl.BlockSpec((1, HB, 1, LANE, tm), lambda b, i: (b, 0, i, 0, 0)),
        pl.BlockSpec((1, tm, HB * LANE), lambda b, i: (b, i, 0)),
        pl.BlockSpec((1, HB, 1, V_ROWS, tm), lambda b, i: (b, 0, i, 0, 0)),
    )
    return pl.pallas_call(
        _proj_kernel,
        grid=(B, nt),
        in_specs=[pl.BlockSpec((1, tm, D_MODEL), lambda b, i: (b, i, 0))]
        + [full(w) for w in weights] + [tab(t) for t in tabs],
        out_specs=out_specs,
        out_shape=out_shape,
        compiler_params=pltpu.CompilerParams(
            dimension_semantics=("parallel", "parallel"), vmem_limit_bytes=VMEM_LIMIT),
        name="proj",
    )(x, *weights, *tabs)


def _swa_kernel(sink_ref, qaT_ref, kp_ref, kc_ref, kn_ref, vp_ref, vc_ref, vn_ref, oT_ref, s_ref, *, seq):
    tq = qaT_ref.shape[2]
    q0 = pl.program_id(1) * tq
    kcat = jnp.concatenate([kp_ref[0], kc_ref[0], kn_ref[0]], axis=0)
    vcat = jnp.concatenate([vp_ref[0], vc_ref[0], vn_ref[0]], axis=1)
    nk = 3 * WINDOW
    wide = GA * WINDOW
    r = lax.broadcasted_iota(jnp.int32, (nk, wide), 0)
    lane = lax.broadcasted_iota(jnp.int32, (nk, wide), 1)
    d = r - (lane & (WINDOW - 1))
    band = (d >= 0) & (d <= 2 * WINDOW)
    head = lax.broadcasted_iota(jnp.int32, (1, wide), 1) // WINDOW
    units = [(c, g) for c in range(tq // WINDOW) for g in range(KVA)]

    def scores(slot, c, g):
        q4 = jnp.concatenate(
            [qaT_ref[0, (g * GA + h) * LANE:(g * GA + h + 1) * LANE, c * WINDOW:(c + 1) * WINDOW]
             for h in range(GA)], axis=1)
        valid = band
        if c in (0, tq // WINDOW - 1):
            kpos = q0 + (c - 1) * WINDOW + r
            valid = band & (kpos >= 0) & (kpos < seq)
        s = jnp.where(valid, _dot(kcat[c * WINDOW:c * WINDOW + nk], q4), NEG)
        s_ref[slot] = s
        return jnp.max(s, axis=0, keepdims=True)

    def finish(slot, c, g, cmax):
        sink = jnp.zeros((1, wide), F32)
        for h in range(GA):
            sink = jnp.where(head == h, sink_ref[g * GA + h] * LOG2E, sink)
        m = jnp.maximum(cmax, sink)
        p = jnp.exp2(s_ref[slot] - m).astype(BF16)
        pv = _dot(vcat[g * V_ROWS:(g + 1) * V_ROWS, c * WINDOW:c * WINDOW + nk], p)
        o = (pv[:DA] / (pv[DA:DA + 1] + jnp.exp2(sink - m))).astype(BF16)
        for h in range(GA):
            hq = g * GA + h
            oT_ref[0, hq * DA:(hq + 1) * DA, c * WINDOW:(c + 1) * WINDOW] = o[:, h * WINDOW:(h + 1) * WINDOW]

    cmax = scores(0, *units[0])
    for u, (c, g) in enumerate(units):
        nxt = scores((u + 1) % 2, *units[u + 1]) if u + 1 < len(units) else None
        finish(u % 2, c, g, cmax)
        cmax = nxt


def _swa(sink, qaT, ka, vaT):
    B, _, S = qaT.shape
    tq = SWA_TILE
    r = tq // WINDOW
    nb = S // WINDOW
    prev = lambda b, i, *_: (b, jnp.maximum(i * r - 1, 0), 0)
    nxt = lambda b, i, *_: (b, jnp.minimum(i * r + r, nb - 1), 0)
    prev_t = lambda b, i, *_: (b, 0, jnp.maximum(i * r - 1, 0))
    nxt_t = lambda b, i, *_: (b, 0, jnp.minimum(i * r + r, nb - 1))
    grid_spec = pltpu.PrefetchScalarGridSpec(
        num_scalar_prefetch=1,
        grid=(B, S // tq),
        in_specs=[
            pl.BlockSpec((1, HA * LANE, tq), lambda b, i, *_: (b, 0, i)),
            pl.BlockSpec((1, WINDOW, KVA * DA), prev),
            pl.BlockSpec((1, tq, KVA * DA), lambda b, i, *_: (b, i, 0)),
            pl.BlockSpec((1, WINDOW, KVA * DA), nxt),
            pl.BlockSpec((1, KVA * V_ROWS, WINDOW), prev_t),
            pl.BlockSpec((1, KVA * V_ROWS, tq), lambda b, i, *_: (b, 0, i)),
            pl.BlockSpec((1, KVA * V_ROWS, WINDOW), nxt_t),
        ],
        out_specs=pl.BlockSpec((1, HA * DA, tq), lambda b, i, *_: (b, 0, i)),
        scratch_shapes=[pltpu.VMEM((2, 3 * WINDOW, GA * WINDOW), F32)],
    )
    return pl.pallas_call(
        functools.partial(_swa_kernel, seq=S),
        grid_spec=grid_spec,
        out_shape=jax.ShapeDtypeStruct((B, HA * DA, S), BF16),
        compiler_params=pltpu.CompilerParams(
            dimension_semantics=("parallel", "parallel"), vmem_limit_bytes=VMEM_LIMIT),
        name="swa",
    )(sink, qaT, ka, ka, ka, vaT, vaT, vaT)


def _mla_kernel(qT_ref, k_ref, vT_ref, oT_ref, s_ref):
    nt, _, tw = qT_ref.shape[2:]
    nvb, nv, tv = vT_ref.shape[2:]
    tk, tq = s_ref.shape[1:]
    nkb = nvb * tv // tk
    vper = tk // tv
    sub = tw // tq
    nq = nt * sub

    def scores(slot, tile, half, j):
        q = qT_ref[0, 0, tile, :, half * tq:(half + 1) * tq]
        cmax = None
        for r in range(0, tk, MLA_SCORE_ROWS):
            s = _dot(k_ref[0, j * tk + r:j * tk + r + MLA_SCORE_ROWS, :], q)
            s_ref[slot, r:r + MLA_SCORE_ROWS] = s
            cm = jnp.max(s, axis=0, keepdims=True)
            cmax = cm if cmax is None else jnp.maximum(cmax, cm)
        return cmax

    def softmax_pv(slot, j, cmax, m, acc):
        m_new = jnp.maximum(m, cmax)
        alpha = jnp.exp2(m - m_new)
        p = jnp.exp2(s_ref[slot] - m_new).astype(BF16)
        pv = _dot(vT_ref[0, 0, j * vper, :, :LANE], p[:LANE])
        pv += _dot(vT_ref[0, 0, j * vper, :, LANE:], p[LANE:tv])
        for c in range(1, vper):
            pv += _dot(vT_ref[0, 0, j * vper + c], p[c * tv:(c + 1) * tv])
        return m_new, alpha * acc + pv

    unroll = min(nq, max(sub, MLA_BLOCKS_PER_TRIP // nkb))
    assert nq % unroll == 0 and unroll % sub == 0
    tiles_per_trip = unroll // sub

    def q_tiles(t, cmax):
        for u in range(unroll):
            tile, half = t * tiles_per_trip + u // sub, u % sub
            m, acc = jnp.full((1, tq), -jnp.inf, F32), jnp.zeros((nv, tq), F32)
            for j in range(nkb):
                if j + 1 < nkb:
                    nxt = scores((j + 1) % 2, tile, half, j + 1)
                elif u + 1 < unroll:
                    nxt = scores(0, t * tiles_per_trip + (u + 1) // sub, (u + 1) % sub, 0)
                else:
                    nxt = scores(0, jnp.minimum((t + 1) * tiles_per_trip, nt - 1), 0, 0)
                m, acc = softmax_pv(j % 2, j, cmax, m, acc)
                cmax = nxt
            oT_ref[0, 0, tile, :, half * tq:(half + 1) * tq] = (acc[:DV] / acc[DV:DV + 1]).astype(BF16)
        return cmax

    lax.fori_loop(0, nq // unroll, q_tiles, scores(0, 0, 0, 0))


def _mla(qT, k, vT):
    B, _, nq, _, tq = qT.shape
    nvb, nv, tv = vT.shape[2:]
    S = nq * tq
    tk = MLA_KEY_BLOCK
    assert tk % tv == 0 and S % (2 * tk) == 0
    return pl.pallas_call(
        _mla_kernel,
        grid=(B, HB),
        in_specs=[
            pl.BlockSpec((1, 1, nq, LANE, tq), lambda b, h: (b, h, 0, 0, 0)),
            pl.BlockSpec((1, S, LANE), lambda b, h: (b, 0, h)),
            pl.BlockSpec((1, 1, nvb, nv, tv), lambda b, h: (b, h, 0, 0, 0)),
        ],
        out_specs=pl.BlockSpec((1, 1, nq, DV, tq), lambda b, h: (b, h, 0, 0, 0)),
        out_shape=jax.ShapeDtypeStruct((B, HB, nq, DV, tq), BF16),
        scratch_shapes=[pltpu.VMEM((2, tk, MLA_Q_TILE), F32)],
        compiler_params=pltpu.CompilerParams(
            dimension_semantics=("parallel", "parallel"), vmem_limit_bytes=VMEM_LIMIT),
        name="mla",
    )(qT, k, vT)


def _ffn_kernel(x_ref, oaT_ref, obT_ref, woa_ref, wob_ref, gffn_ref, wg_ref, wu_ref, wd_ref, gfin_ref, y_ref):
    x = x_ref[0]
    obT = obT_ref[0, :, 0].reshape(HB * DV, -1)
    x = x + _dot_tn(oaT_ref[0], woa_ref[...]) + _dot_tn(obT, wob_ref[...])
    h = (x * lax.rsqrt(jnp.mean(x * x, axis=-1, keepdims=True) + EPS) * gffn_ref[...]).astype(BF16)
    g = _dot(h, wg_ref[...])
    u = _dot(h, wu_ref[...])
    a = (g / (1.0 + jnp.exp(-g)) * u).astype(BF16)
    x = x + _dot(a, wd_ref[...])
    y_ref[0] = x * lax.rsqrt(jnp.mean(x * x, axis=-1, keepdims=True) + EPS) * gfin_ref[...]


def _ffn(x, oaT, obT, p):
    B, S, _ = x.shape
    tm = TOK_TILE
    full = lambda a: pl.BlockSpec(a.shape, lambda b, i: (0,) * a.ndim, pipeline_mode=pl.Buffered(1))
    weights = (p["w_oa"], p["w_ob"], p["g_ffn"], p["w_gate"], p["w_up"], p["w_down"], p["g_final"])
    return pl.pallas_call(
        _ffn_kernel,
        grid=(B, S // tm),
        in_specs=[
            pl.BlockSpec((1, tm, D_MODEL), lambda b, i: (b, i, 0)),
            pl.BlockSpec((1, HA * DA, tm), lambda b, i: (b, 0, i)),
            pl.BlockSpec((1, HB, 1, DV, tm), lambda b, i: (b, 0, i, 0, 0)),
        ] + [full(w) for w in weights],
        out_specs=pl.BlockSpec((1, tm, D_MODEL), lambda b, i: (b, i, 0)),
        out_shape=jax.ShapeDtypeStruct(x.shape, F32),
        compiler_params=pltpu.CompilerParams(
            dimension_semantics=("parallel", "parallel"), vmem_limit_bytes=VMEM_LIMIT),
        name="ffn",
    )(x, oaT, obT, *weights)


def _rope_tables_t(seq, dim):
    inv = 1.0 / (ROPE_THETA ** (jnp.arange(0, dim, 2, dtype=F32) / dim))
    ang = jnp.arange(seq, dtype=F32)[:, None] * inv[None, :]
    return jnp.cos(ang).T, jnp.sin(ang).T


def _prep_params(g_mix, w_in, sink, cq_g, w_uq, ckv_g, w_ukv, w_o, g_ffn, w_gate, w_up, w_down, g_final):
    w_uq3 = w_uq.reshape(Q_RANK, HB, DN + DR)
    w_uq_pad = jnp.pad(w_uq3, ((0, 0), (0, 0), (0, LANE - DN - DR))).reshape(Q_RANK, HB * LANE)
    w_ukv3 = w_ukv.reshape(KV_RANK, HB, DN + DV)
    w_uk, w_uv = w_ukv3[..., :DN], w_ukv3[..., DN:]
    top = jnp.pad(w_uk, ((0, 0), (0, 0), (0, LANE - DN)))
    eye = jnp.pad(jnp.eye(DR, dtype=F32)[:, None, :], ((0, 0), (0, 0), (DN, LANE - DN - DR)))
    bot = jnp.broadcast_to(eye, (DR, HB, LANE))
    w_k = jnp.concatenate([top, bot], axis=0).reshape(KV_RANK + DR, HB * LANE)
    return {
        "g_mix": g_mix.reshape(1, D_MODEL),
        "w_inT": w_in.T.astype(BF16),
        "cq_g": cq_g.reshape(Q_RANK, 1),
        "w_uqT": w_uq_pad.T.astype(BF16),
        "ckv_g": ckv_g.reshape(KV_RANK, 1),
        "w_k": w_k.astype(BF16),
        "w_vT": w_uv.reshape(KV_RANK, HB * DV).T.astype(BF16),
        "sink": sink,
        "w_oa": w_o[:HA * DA].astype(BF16),
        "w_ob": w_o[HA * DA:].astype(BF16),
        "g_ffn": g_ffn.reshape(1, D_MODEL),
        "w_gate": w_gate.astype(BF16),
        "w_up": w_up.astype(BF16),
        "w_down": w_down.astype(BF16),
        "g_final": g_final.reshape(1, D_MODEL),
    }


def _trunk(x, p):
    S = x.shape[1]
    tabs = _rope_tables_t(S, DA) + _rope_tables_t(S, DR)
    qaT, ka, vaT, qT, k, vT = _proj(x, p, tabs)
    oaT = _swa(p["sink"], qaT, ka, vaT)
    obT = _mla(qT, k, vT)
    return _ffn(x, oaT, obT, p)


def kernel(x_prompt, x_sample, g_mix, w_in, sink, cq_g, w_uq, ckv_g, w_ukv, w_o, g_ffn,
           w_gate, w_up, w_down, g_final):
    p = _prep_params(g_mix[0], w_in[0], sink[0], cq_g[0], w_uq[0], ckv_g[0], w_ukv[0], w_o[0],
                     g_ffn[0], w_gate[0], w_up[0], w_down[0], g_final)
    return _trunk(x_prompt, p), _trunk(x_sample, p)
```

```python
import functools
import math

import jax
import jax.numpy as jnp
from jax import lax
from jax.experimental import pallas as pl
from jax.experimental.pallas import tpu as pltpu

D_MODEL = 1024
EPS = 1e-6
ROPE_THETA = 10000.0
WINDOW = 128
HA, KVA, DA = 8, 2, 64
GA = HA // KVA
HB, Q_RANK, KV_RANK, DN, DR, DV = 8, 384, 256, 64, 32, 64
D_IN = HA * DA + 2 * KVA * DA + Q_RANK + KV_RANK + DR
D_FF = int(math.ceil(8 * D_MODEL / 3 / 256) * 256)
NEG = -1e30

LANE = 128
V_ROWS = 128
LOG2E = math.log2(math.e)
SWA_Q_SCALE = DA ** -0.5 * LOG2E
MLA_Q_SCALE = (DN + DR) ** -0.5 * LOG2E
OFF_KA = HA * DA
OFF_VA = OFF_KA + KVA * DA
OFF_CQ = OFF_VA + KVA * DA
OFF_CKV = OFF_CQ + Q_RANK
OFF_KR = OFF_CKV + KV_RANK

TOK_TILE = 512
SWA_TILE = 1024
MLA_Q_TILE = 256
MLA_KEY_BLOCK = 2048
MLA_SCORE_ROWS = 1024
MLA_Q_TILES_PER_TRIP = 8
VMEM_LIMIT = 56 * 1024 * 1024

F32 = jnp.float32
BF16 = jnp.bfloat16


def _dot(a, b):
    return jnp.dot(a, b, preferred_element_type=F32)


def _dot_tn(a, b):
    return lax.dot_general(a, b, (((0,), (0,)), ((), ())), preferred_element_type=F32)


def _dot_nt(a, b):
    return lax.dot_general(a, b, (((1,), (1,)), ((), ())), preferred_element_type=F32)


def _rope_rows(x1, x2, c, s):
    return x1 * c - x2 * s, x2 * c + x1 * s


def _proj_kernel(x_ref, gmix_ref, winT_ref, cqg_ref, wuqT_ref, ckvg_ref, wk_ref, wvT_ref,
                 cosa_ref, sina_ref, cosr_ref, sinr_ref,
                 qaT_ref, ka_ref, vaT_ref, qT_ref, k_ref, vT_ref):
    x = x_ref[0]
    ms = jnp.mean(x * x, axis=-1, keepdims=True)
    h = (x * lax.rsqrt(ms + EPS) * gmix_ref[...]).astype(BF16)
    zT = _dot_nt(winT_ref[...], h)

    ca, sa = cosa_ref[...], sina_ref[...]
    cr, sr = cosr_ref[...], sinr_ref[...]
    half = DA // 2

    qaT_ref[0] = jnp.zeros(qaT_ref.shape[1:], BF16)
    for hq in range(HA):
        r = hq * DA
        o1, o2 = _rope_rows(zT[r:r + half], zT[r + half:r + DA], ca, sa)
        base = hq * LANE + (hq // GA) * DA
        qaT_ref[0, base:base + half, :] = (o1 * SWA_Q_SCALE).astype(BF16)
        qaT_ref[0, base + half:base + DA, :] = (o2 * SWA_Q_SCALE).astype(BF16)

    kaT = []
    for g in range(KVA):
        r = OFF_KA + g * DA
        kaT.extend(_rope_rows(zT[r:r + half], zT[r + half:r + DA], ca, sa))
    ka_ref[0] = jnp.concatenate(kaT, axis=0).T.astype(BF16)
    tm = zT.shape[1]
    ones_row = lax.broadcasted_iota(jnp.int32, (V_ROWS - DV, tm), 0) == 0
    for g in range(KVA):
        r = OFF_VA + g * DA
        vaT_ref[0, g * V_ROWS:g * V_ROWS + DA, :] = zT[r:r + DA].astype(BF16)
        vaT_ref[0, g * V_ROWS + DA:(g + 1) * V_ROWS, :] = ones_row.astype(BF16)

    cq = zT[OFF_CQ:OFF_CKV]
    cqn = (cq * lax.rsqrt(jnp.mean(cq * cq, axis=0, keepdims=True) + EPS) * cqg_ref[...]).astype(BF16)
    qT = _dot(wuqT_ref[...], cqn) * MLA_Q_SCALE
    qT_ref[0, :, 0] = qT.astype(BF16).reshape(HB, LANE, tm)
    hr = DR // 2
    for hh in range(HB):
        r = hh * LANE + DN
        o1, o2 = _rope_rows(qT[r:r + hr], qT[r + hr:r + DR], cr, sr)
        qT_ref[0, hh, 0, DN:DN + hr, :] = o1.astype(BF16)
        qT_ref[0, hh, 0, DN + hr:DN + DR, :] = o2.astype(BF16)

    ckv = zT[OFF_CKV:OFF_KR]
    ckvn = (ckv * lax.rsqrt(jnp.mean(ckv * ckv, axis=0, keepdims=True) + EPS) * ckvg_ref[...]).astype(BF16)
    krT = zT[OFF_KR:D_IN]
    k1, k2 = _rope_rows(krT[:hr], krT[hr:], cr, sr)
    kin = jnp.concatenate([ckvn, k1.astype(BF16), k2.astype(BF16)], axis=0)
    k_ref[0] = _dot_tn(kin, wk_ref[...]).astype(BF16)
    vT_ref[0, :, 0, :DV] = _dot(wvT_ref[...], ckvn).astype(BF16).reshape(HB, DV, tm)
    vT_ref[0, :, 0, DV:] = jnp.broadcast_to(ones_row.astype(BF16), (HB, V_ROWS - DV, tm))


def _proj(x, p, tabs):
    B, S, _ = x.shape
    tm = TOK_TILE
    nt = S // tm
    full = lambda a: pl.BlockSpec(a.shape, lambda b, i: (0,) * a.ndim)
    tab = lambda a: pl.BlockSpec((a.shape[0], tm), lambda b, i: (0, i))
    weights = (p["g_mix"], p["w_inT"], p["cq_g"], p["w_uqT"], p["ckv_g"], p["w_k"], p["w_vT"])
    out_shape = (
        jax.ShapeDtypeStruct((B, HA * LANE, S), BF16),
        jax.ShapeDtypeStruct((B, S, KVA * DA), BF16),
        jax.ShapeDtypeStruct((B, KVA * V_ROWS, S), BF16),
        jax.ShapeDtypeStruct((B, HB, nt, LANE, tm), BF16),
        jax.ShapeDtypeStruct((B, S, HB * LANE), BF16),
        jax.ShapeDtypeStruct((B, HB, nt, V_ROWS, tm), BF16),
    )
    out_specs = (
        pl.BlockSpec((1, HA * LANE, tm), lambda b, i: (b, 0, i)),
        pl.BlockSpec((1, tm, KVA * DA), lambda b, i: (b, i, 0)),
        pl.BlockSpec((1, KVA * V_ROWS, tm), lambda b, i: (b, 0, i)),
        pl.BlockSpec((1, HB, 1, LANE, tm), lambda b, i: (b, 0, i, 0, 0)),
        pl.BlockSpec((1, tm, HB * LANE), lambda b, i: (b, i, 0)),
        pl.BlockSpec((1, HB, 1, V_ROWS, tm), lambda b, i: (b, 0, i, 0, 0)),
    )
    return pl.pallas_call(
        _proj_kernel,
        grid=(B, nt),
        in_specs=[pl.BlockSpec((1, tm, D_MODEL), lambda b, i: (b, i, 0))]
        + [full(w) for w in weights] + [tab(t) for t in tabs],
        out_specs=out_specs,
        out_shape=out_shape,
        compiler_params=pltpu.CompilerParams(
            dimension_semantics=("parallel", "parallel"), vmem_limit_bytes=VMEM_LIMIT),
        name="proj",
    )(x, *weights, *tabs)


def _swa_kernel(sink_ref, qaT_ref, kp_ref, kc_ref, kn_ref, vp_ref, vc_ref, vn_ref, oT_ref, s_ref, *, seq):
    tq = qaT_ref.shape[2]
    q0 = pl.program_id(1) * tq
    kcat = jnp.concatenate([kp_ref[0], kc_ref[0], kn_ref[0]], axis=0)
    vcat = jnp.concatenate([vp_ref[0], vc_ref[0], vn_ref[0]], axis=1)
    nk = 3 * WINDOW
    wide = GA * WINDOW
    r = lax.broadcasted_iota(jnp.int32, (nk, wide), 0)
    lane = lax.broadcasted_iota(jnp.int32, (nk, wide), 1)
    d = r - (lane & (WINDOW - 1))
    band = (d >= 0) & (d <= 2 * WINDOW)
    head = lax.broadcasted_iota(jnp.int32, (1, wide), 1) // WINDOW
    units = [(c, g) for c in range(tq // WINDOW) for g in range(KVA)]

    def scores(slot, c, g):
        q4 = jnp.concatenate(
            [qaT_ref[0, (g * GA + h) * LANE:(g * GA + h + 1) * LANE, c * WINDOW:(c + 1) * WINDOW]
             for h in range(GA)], axis=1)
        valid = band
        if c in (0, tq // WINDOW - 1):
            kpos = q0 + (c - 1) * WINDOW + r
            valid = band & (kpos >= 0) & (kpos < seq)
        s = jnp.where(valid, _dot(kcat[c * WINDOW:c * WINDOW + nk], q4), NEG)
        s_ref[slot] = s
        return jnp.max(s, axis=0, keepdims=True)

    def finish(slot, c, g, cmax):
        sink = jnp.zeros((1, wide), F32)
        for h in range(GA):
            sink = jnp.where(head == h, sink_ref[g * GA + h] * LOG2E, sink)
        m = jnp.maximum(cmax, sink)
        p = jnp.exp2(s_ref[slot] - m).astype(BF16)
        pv = _dot(vcat[g * V_ROWS:(g + 1) * V_ROWS, c * WINDOW:c * WINDOW + nk], p)
        o = (pv[:DA] / (pv[DA:DA + 1] + jnp.exp2(sink - m))).astype(BF16)
        for h in range(GA):
            hq = g * GA + h
            oT_ref[0, hq * DA:(hq + 1) * DA, c * WINDOW:(c + 1) * WINDOW] = o[:, h * WINDOW:(h + 1) * WINDOW]

    cmax = scores(0, *units[0])
    for u, (c, g) in enumerate(units):
        nxt = scores((u + 1) % 2, *units[u + 1]) if u + 1 < len(units) else None
        finish(u % 2, c, g, cmax)
        cmax = nxt


def _swa(sink, qaT, ka, vaT):
    B, _, S = qaT.shape
    tq = SWA_TILE
    r = tq // WINDOW
    nb = S // WINDOW
    prev = lambda b, i, *_: (b, jnp.maximum(i * r - 1, 0), 0)
    nxt = lambda b, i, *_: (b, jnp.minimum(i * r + r, nb - 1), 0)
    prev_t = lambda b, i, *_: (b, 0, jnp.maximum(i * r - 1, 0))
    nxt_t = lambda b, i, *_: (b, 0, jnp.minimum(i * r + r, nb - 1))
    grid_spec = pltpu.PrefetchScalarGridSpec(
        num_scalar_prefetch=1,
        grid=(B, S // tq),
        in_specs=[
            pl.BlockSpec((1, HA * LANE, tq), lambda b, i, *_: (b, 0, i)),
            pl.BlockSpec((1, WINDOW, KVA * DA), prev),
            pl.BlockSpec((1, tq, KVA * DA), lambda b, i, *_: (b, i, 0)),
            pl.BlockSpec((1, WINDOW, KVA * DA), nxt),
            pl.BlockSpec((1, KVA * V_ROWS, WINDOW), prev_t),
            pl.BlockSpec((1, KVA * V_ROWS, tq), lambda b, i, *_: (b, 0, i)),
            pl.BlockSpec((1, KVA * V_ROWS, WINDOW), nxt_t),
        ],
        out_specs=pl.BlockSpec((1, HA * DA, tq), lambda b, i, *_: (b, 0, i)),
        scratch_shapes=[pltpu.VMEM((2, 3 * WINDOW, GA * WINDOW), F32)],
    )
    return pl.pallas_call(
        functools.partial(_swa_kernel, seq=S),
        grid_spec=grid_spec,
        out_shape=jax.ShapeDtypeStruct((B, HA * DA, S), BF16),
        compiler_params=pltpu.CompilerParams(
            dimension_semantics=("parallel", "parallel"), vmem_limit_bytes=VMEM_LIMIT),
        name="swa",
    )(sink, qaT, ka, ka, ka, vaT, vaT, vaT)


def _mla_kernel(qT_ref, k_ref, vT_ref, oT_ref, s_ref):
    nt, _, tw = qT_ref.shape[2:]
    nvb, nv, tv = vT_ref.shape[2:]
    tk, tq = s_ref.shape[1:]
    nkb = nvb * tv // tk
    vper = tk // tv
    sub = tw // tq
    nq = nt * sub

    def scores(slot, tile, half, j):
        q = qT_ref[0, 0, tile, :, half * tq:(half + 1) * tq]
        cmax = None
        for r in range(0, tk, MLA_SCORE_ROWS):
            s = _dot(k_ref[0, j * tk + r:j * tk + r + MLA_SCORE_ROWS, :], q)
            s_ref[slot, r:r + MLA_SCORE_ROWS] = s
            cm = jnp.max(s, axis=0, keepdims=True)
            cmax = cm if cmax is None else jnp.maximum(cmax, cm)
        return cmax

    def softmax_pv(slot, j, cmax, m, acc):
        m_new = jnp.maximum(m, cmax)
        alpha = jnp.exp2(m - m_new)
        p = jnp.exp2(s_ref[slot] - m_new).astype(BF16)
        pv = _dot(vT_ref[0, 0, j * vper, :, :LANE], p[:LANE])
        pv += _dot(vT_ref[0, 0, j * vper, :, LANE:], p[LANE:tv])
        for c in range(1, vper):
            pv += _dot(vT_ref[0, 0, j * vper + c], p[c * tv:(c + 1) * tv])
        return m_new, alpha * acc + pv

    unroll = min(nq, MLA_Q_TILES_PER_TRIP)
    assert nq % unroll == 0 and unroll % sub == 0
    tiles_per_trip = unroll // sub

    def q_tiles(t, cmax):
        for u in range(unroll):
            tile, half = t * tiles_per_trip + u // sub, u % sub
            m, acc = jnp.full((1, tq), -jnp.inf, F32), jnp.zeros((nv, tq), F32)
            for j in range(nkb):
                if j + 1 < nkb:
                    nxt = scores((j + 1) % 2, tile, half, j + 1)
                elif u + 1 < unroll:
                    nxt = scores(0, t * tiles_per_trip + (u + 1) // sub, (u + 1) % sub, 0)
                else:
                    nxt = scores(0, jnp.minimum((t + 1) * tiles_per_trip, nt - 1), 0, 0)
                m, acc = softmax_pv(j % 2, j, cmax, m, acc)
                cmax = nxt
            oT_ref[0, 0, tile, :, half * tq:(half + 1) * tq] = (acc[:DV] / acc[DV:DV + 1]).astype(BF16)
        return cmax

    lax.fori_loop(0, nq // unroll, q_tiles, scores(0, 0, 0, 0))


def _mla(qT, k, vT):
    B, _, nq, _, tq = qT.shape
    nvb, nv, tv = vT.shape[2:]
    S = nq * tq
    tk = MLA_KEY_BLOCK
    assert tk % tv == 0 and S % (2 * tk) == 0
    return pl.pallas_call(
        _mla_kernel,
        grid=(B, HB),
        in_specs=[
            pl.BlockSpec((1, 1, nq, LANE, tq), lambda b, h: (b, h, 0, 0, 0)),
            pl.BlockSpec((1, S, LANE), lambda b, h: (b, 0, h)),
            pl.BlockSpec((1, 1, nvb, nv, tv), lambda b, h: (b, h, 0, 0, 0)),
        ],
        out_specs=pl.BlockSpec((1, 1, nq, DV, tq), lambda b, h: (b, h, 0, 0, 0)),
        out_shape=jax.ShapeDtypeStruct((B, HB, nq, DV, tq), BF16),
        scratch_shapes=[pltpu.VMEM((2, tk, MLA_Q_TILE), F32)],
        compiler_params=pltpu.CompilerParams(
            dimension_semantics=("parallel", "parallel"), vmem_limit_bytes=VMEM_LIMIT),
        name="mla",
    )(qT, k, vT)


def _ffn_kernel(x_ref, oaT_ref, obT_ref, woa_ref, wob_ref, gffn_ref, wg_ref, wu_ref, wd_ref, gfin_ref, y_ref):
    x = x_ref[0]
    obT = obT_ref[0, :, 0].reshape(HB * DV, -1)
    x = x + _dot_tn(oaT_ref[0], woa_ref[...]) + _dot_tn(obT, wob_ref[...])
    h = (x * lax.rsqrt(jnp.mean(x * x, axis=-1, keepdims=True) + EPS) * gffn_ref[...]).astype(BF16)
    g = _dot(h, wg_ref[...])
    u = _dot(h, wu_ref[...])
    a = (g / (1.0 + jnp.exp(-g)) * u).astype(BF16)
    x = x + _dot(a, wd_ref[...])
    y_ref[0] = x * lax.rsqrt(jnp.mean(x * x, axis=-1, keepdims=True) + EPS) * gfin_ref[...]


def _ffn(x, oaT, obT, p):
    B, S, _ = x.shape
    tm = TOK_TILE
    full = lambda a: pl.BlockSpec(a.shape, lambda b, i: (0,) * a.ndim, pipeline_mode=pl.Buffered(1))
    weights = (p["w_oa"], p["w_ob"], p["g_ffn"], p["w_gate"], p["w_up"], p["w_down"], p["g_final"])
    return pl.pallas_call(
        _ffn_kernel,
        grid=(B, S // tm),
        in_specs=[
            pl.BlockSpec((1, tm, D_MODEL), lambda b, i: (b, i, 0)),
            pl.BlockSpec((1, HA * DA, tm), lambda b, i: (b, 0, i)),
            pl.BlockSpec((1, HB, 1, DV, tm), lambda b, i: (b, 0, i, 0, 0)),
        ] + [full(w) for w in weights],
        out_specs=pl.BlockSpec((1, tm, D_MODEL), lambda b, i: (b, i, 0)),
        out_shape=jax.ShapeDtypeStruct(x.shape, F32),
        compiler_params=pltpu.CompilerParams(
            dimension_semantics=("parallel", "parallel"), vmem_limit_bytes=VMEM_LIMIT),
        name="ffn",
    )(x, oaT, obT, *weights)


def _rope_tables_t(seq, dim):
    inv = 1.0 / (ROPE_THETA ** (jnp.arange(0, dim, 2, dtype=F32) / dim))
    ang = jnp.arange(seq, dtype=F32)[:, None] * inv[None, :]
    return jnp.cos(ang).T, jnp.sin(ang).T


def _prep_params(g_mix, w_in, sink, cq_g, w_uq, ckv_g, w_ukv, w_o, g_ffn, w_gate, w_up, w_down, g_final):
    w_uq3 = w_uq.reshape(Q_RANK, HB, DN + DR)
    w_uq_pad = jnp.pad(w_uq3, ((0, 0), (0, 0), (0, LANE - DN - DR))).reshape(Q_RANK, HB * LANE)
    w_ukv3 = w_ukv.reshape(KV_RANK, HB, DN + DV)
    w_uk, w_uv = w_ukv3[..., :DN], w_ukv3[..., DN:]
    top = jnp.pad(w_uk, ((0, 0), (0, 0), (0, LANE - DN)))
    eye = jnp.pad(jnp.eye(DR, dtype=F32)[:, None, :], ((0, 0), (0, 0), (DN, LANE - DN - DR)))
    bot = jnp.broadcast_to(eye, (DR, HB, LANE))
    w_k = jnp.concatenate([top, bot], axis=0).reshape(KV_RANK + DR, HB * LANE)
    return {
        "g_mix": g_mix.reshape(1, D_MODEL),
        "w_inT": w_in.T.astype(BF16),
        "cq_g": cq_g.reshape(Q_RANK, 1),
        "w_uqT": w_uq_pad.T.astype(BF16),
        "ckv_g": ckv_g.reshape(KV_RANK, 1),
        "w_k": w_k.astype(BF16),
        "w_vT": w_uv.reshape(KV_RANK, HB * DV).T.astype(BF16),
        "sink": sink,
        "w_oa": w_o[:HA * DA].astype(BF16),
        "w_ob": w_o[HA * DA:].astype(BF16),
        "g_ffn": g_ffn.reshape(1, D_MODEL),
        "w_gate": w_gate.astype(BF16),
        "w_up": w_up.astype(BF16),
        "w_down": w_down.astype(BF16),
        "g_final": g_final.reshape(1, D_MODEL),
    }


def _trunk(x, p):
    S = x.shape[1]
    tabs = _rope_tables_t(S, DA) + _rope_tables_t(S, DR)
    qaT, ka, vaT, qT, k, vT = _proj(x, p, tabs)
    oaT = _swa(p["sink"], qaT, ka, vaT)
    obT = _mla(qT, k, vT)
    return _ffn(x, oaT, obT, p)


def kernel(x_prompt, x_sample, g_mix, w_in, sink, cq_g, w_uq, ckv_g, w_ukv, w_o, g_ffn,
           w_gate, w_up, w_down, g_final):
    p = _prep_params(g_mix[0], w_in[0], sink[0], cq_g[0], w_uq[0], ckv_g[0], w_ukv[0], w_o[0],
                     g_ffn[0], w_gate[0], w_up[0], w_down[0], g_final)
    return _trunk(x_prompt, p), _trunk(x_sample, p)
```

```python
import functools
import math

import jax
import jax.numpy as jnp
from jax import lax
from jax.experimental import pallas as pl
from jax.experimental.pallas import tpu as pltpu

D_MODEL = 1024
EPS = 1e-6
ROPE_THETA = 10000.0
WINDOW = 128
HA, KVA, DA = 8, 2, 64
GA = HA // KVA
HB, Q_RANK, KV_RANK, DN, DR, DV = 8, 384, 256, 64, 32, 64
D_IN = HA * DA + 2 * KVA * DA + Q_RANK + KV_RANK + DR
D_FF = int(math.ceil(8 * D_MODEL / 3 / 256) * 256)
NEG = -1e30

LANE = 128
V_ROWS = 128
LOG2E = math.log2(math.e)
SWA_Q_SCALE = DA ** -0.5 * LOG2E
MLA_Q_SCALE = (DN + DR) ** -0.5 * LOG2E
OFF_KA = HA * DA
OFF_VA = OFF_KA + KVA * DA
OFF_CQ = OFF_VA + KVA * DA
OFF_CKV = OFF_CQ + Q_RANK
OFF_KR = OFF_CKV + KV_RANK

TOK_TILE = 512
SWA_TILE = 1024
MLA_Q_TILE = 256
MLA_KEY_BLOCK = 2048
MLA_SCORE_ROWS = 1024
MLA_Q_TILES_PER_TRIP = 8
VMEM_LIMIT = 56 * 1024 * 1024

F32 = jnp.float32
BF16 = jnp.bfloat16


def _dot(a, b):
    return jnp.dot(a, b, preferred_element_type=F32)


def _dot_tn(a, b):
    return lax.dot_general(a, b, (((0,), (0,)), ((), ())), preferred_element_type=F32)


def _dot_nt(a, b):
    return lax.dot_general(a, b, (((1,), (1,)), ((), ())), preferred_element_type=F32)


def _rope_rows(x1, x2, c, s):
    return x1 * c - x2 * s, x2 * c + x1 * s


def _proj_kernel(x_ref, gmix_ref, winT_ref, cqg_ref, wuqT_ref, ckvg_ref, wk_ref, wvT_ref,
                 cosa_ref, sina_ref, cosr_ref, sinr_ref,
                 qaT_ref, ka_ref, vaT_ref, qT_ref, k_ref, vT_ref):
    x = x_ref[0]
    ms = jnp.mean(x * x, axis=-1, keepdims=True)
    h = (x * lax.rsqrt(ms + EPS) * gmix_ref[...]).astype(BF16)
    zT = _dot_nt(winT_ref[...], h)

    ca, sa = cosa_ref[...], sina_ref[...]
    cr, sr = cosr_ref[...], sinr_ref[...]
    half = DA // 2

    qaT_ref[0] = jnp.zeros(qaT_ref.shape[1:], BF16)
    for hq in range(HA):
        r = hq * DA
        o1, o2 = _rope_rows(zT[r:r + half], zT[r + half:r + DA], ca, sa)
        base = hq * LANE + (hq // GA) * DA
        qaT_ref[0, base:base + half, :] = (o1 * SWA_Q_SCALE).astype(BF16)
        qaT_ref[0, base + half:base + DA, :] = (o2 * SWA_Q_SCALE).astype(BF16)

    kaT = []
    for g in range(KVA):
        r = OFF_KA + g * DA
        kaT.extend(_rope_rows(zT[r:r + half], zT[r + half:r + DA], ca, sa))
    ka_ref[0] = jnp.concatenate(kaT, axis=0).T.astype(BF16)
    tm = zT.shape[1]
    ones_row = lax.broadcasted_iota(jnp.int32, (V_ROWS - DV, tm), 0) == 0
    for g in range(KVA):
        r = OFF_VA + g * DA
        vaT_ref[0, g * V_ROWS:g * V_ROWS + DA, :] = zT[r:r + DA].astype(BF16)
        vaT_ref[0, g * V_ROWS + DA:(g + 1) * V_ROWS, :] = ones_row.astype(BF16)

    cq = zT[OFF_CQ:OFF_CKV]
    cqn = (cq * lax.rsqrt(jnp.mean(cq * cq, axis=0, keepdims=True) + EPS) * cqg_ref[...]).astype(BF16)
    qT = _dot(wuqT_ref[...], cqn) * MLA_Q_SCALE
    qT_ref[0, :, 0] = qT.astype(BF16).reshape(HB, LANE, tm)
    hr = DR // 2
    for hh in range(HB):
        r = hh * LANE + DN
        o1, o2 = _rope_rows(qT[r:r + hr], qT[r + hr:r + DR], cr, sr)
        qT_ref[0, hh, 0, DN:DN + hr, :] = o1.astype(BF16)
        qT_ref[0, hh, 0, DN + hr:DN + DR, :] = o2.astype(BF16)

    ckv = zT[OFF_CKV:OFF_KR]
    ckvn = (ckv * lax.rsqrt(jnp.mean(ckv * ckv, axis=0, keepdims=True) + EPS) * ckvg_ref[...]).astype(BF16)
    krT = zT[OFF_KR:D_IN]
    k1, k2 = _rope_rows(krT[:hr], krT[hr:], cr, sr)
    kin = jnp.concatenate([ckvn, k1.astype(BF16), k2.astype(BF16)], axis=0)
    k_ref[0] = _dot_tn(kin, wk_ref[...]).astype(BF16)
    vT_ref[0, :, 0, :DV] = _dot(wvT_ref[...], ckvn).astype(BF16).reshape(HB, DV, tm)
    vT_ref[0, :, 0, DV:] = jnp.broadcast_to(ones_row.astype(BF16), (HB, V_ROWS - DV, tm))


def _proj(x, p, tabs):
    B, S, _ = x.shape
    tm = TOK_TILE
    nt = S // tm
    full = lambda a: pl.BlockSpec(a.shape, lambda b, i: (0,) * a.ndim)
    tab = lambda a: pl.BlockSpec((a.shape[0], tm), lambda b, i: (0, i))
    weights = (p["g_mix"], p["w_inT"], p["cq_g"], p["w_uqT"], p["ckv_g"], p["w_k"], p["w_vT"])
    out_shape = (
        jax.ShapeDtypeStruct((B, HA * LANE, S), BF16),
        jax.ShapeDtypeStruct((B, S, KVA * DA), BF16),
        jax.ShapeDtypeStruct((B, KVA * V_ROWS, S), BF16),
        jax.ShapeDtypeStruct((B, HB, nt, LANE, tm), BF16),
        jax.ShapeDtypeStruct((B, S, HB * LANE), BF16),
        jax.ShapeDtypeStruct((B, HB, nt, V_ROWS, tm), BF16),
    )
    out_specs = (
        pl.BlockSpec((1, HA * LANE, tm), lambda b, i: (b, 0, i)),
        pl.BlockSpec((1, tm, KVA * DA), lambda b, i: (b, i, 0)),
        pl.BlockSpec((1, KVA * V_ROWS, tm), lambda b, i: (b, 0, i)),
        pl.BlockSpec((1, HB, 1, LANE, tm), lambda b, i: (b, 0, i, 0, 0)),
        pl.BlockSpec((1, tm, HB * LANE), lambda b, i: (b, i, 0)),
        pl.BlockSpec((1, HB, 1, V_ROWS, tm), lambda b, i: (b, 0, i, 0, 0)),
    )
    return pl.pallas_call(
        _proj_kernel,
        grid=(B, nt),
        in_specs=[pl.BlockSpec((1, tm, D_MODEL), lambda b, i: (b, i, 0))]
        + [full(w) for w in weights] + [tab(t) for t in tabs],
        out_specs=out_specs,
        out_shape=out_shape,
        compiler_params=pltpu.CompilerParams(
            dimension_semantics=("parallel", "parallel"), vmem_limit_bytes=VMEM_LIMIT),
        name="proj",
    )(x, *weights, *tabs)


def _swa_kernel(sink_ref, qaT_ref, kp_ref, kc_ref, kn_ref, vp_ref, vc_ref, vn_ref, oT_ref, s_ref, *, seq):
    tq = qaT_ref.shape[2]
    q0 = pl.program_id(1) * tq
    kcat = jnp.concatenate([kp_ref[0], kc_ref[0], kn_ref[0]], axis=0)
    vcat = jnp.concatenate([vp_ref[0], vc_ref[0], vn_ref[0]], axis=1)
    nk = 3 * WINDOW
    wide = GA * WINDOW
    r = lax.broadcasted_iota(jnp.int32, (nk, wide), 0)
    lane = lax.broadcasted_iota(jnp.int32, (nk, wide), 1)
    d = r - (lane & (WINDOW - 1))
    band = (d >= 0) & (d <= 2 * WINDOW)
    head = lax.broadcasted_iota(jnp.int32, (1, wide), 1) // WINDOW
    units = [(c, g) for c in range(tq // WINDOW) for g in range(KVA)]

    def scores(slot, c, g):
        q4 = jnp.concatenate(
            [qaT_ref[0, (g * GA + h) * LANE:(g * GA + h + 1) * LANE, c * WINDOW:(c + 1) * WINDOW]
             for h in range(GA)], axis=1)
        valid = band
        if c in (0, tq // WINDOW - 1):
            kpos = q0 + (c - 1) * WINDOW + r
            valid = band & (kpos >= 0) & (kpos < seq)
        s = jnp.where(valid, _dot(kcat[c * WINDOW:c * WINDOW + nk], q4), NEG)
        s_ref[slot] = s
        return jnp.max(s, axis=0, keepdims=True)

    def finish(slot, c, g, cmax):
        sink = jnp.zeros((1, wide), F32)
        for h in range(GA):
            sink = jnp.where(head == h, sink_ref[g * GA + h] * LOG2E, sink)
        m = jnp.maximum(cmax, sink)
        p = jnp.exp2(s_ref[slot] - m).astype(BF16)
        pv = _dot(vcat[g * V_ROWS:(g + 1) * V_ROWS, c * WINDOW:c * WINDOW + nk], p)
        o = (pv[:DA] / (pv[DA:DA + 1] + jnp.exp2(sink - m))).astype(BF16)
        for h in range(GA):
            hq = g * GA + h
            oT_ref[0, hq * DA:(hq + 1) * DA, c * WINDOW:(c + 1) * WINDOW] = o[:, h * WINDOW:(h + 1) * WINDOW]

    cmax = scores(0, *units[0])
    for u, (c, g) in enumerate(units):
        nxt = scores((u + 1) % 2, *units[u + 1]) if u + 1 < len(units) else None
        finish(u % 2, c, g, cmax)
        cmax = nxt


def _swa(sink, qaT, ka, vaT):
    B, _, S = qaT.shape
    tq = SWA_TILE
    r = tq // WINDOW
    nb = S // WINDOW
    prev = lambda b, i, *_: (b, jnp.maximum(i * r - 1, 0), 0)
    nxt = lambda b, i, *_: (b, jnp.minimum(i * r + r, nb - 1), 0)
    prev_t = lambda b, i, *_: (b, 0, jnp.maximum(i * r - 1, 0))
    nxt_t = lambda b, i, *_: (b, 0, jnp.minimum(i * r + r, nb - 1))
    grid_spec = pltpu.PrefetchScalarGridSpec(
        num_scalar_prefetch=1,
        grid=(B, S // tq),
        in_specs=[
            pl.BlockSpec((1, HA * LANE, tq), lambda b, i, *_: (b, 0, i)),
            pl.BlockSpec((1, WINDOW, KVA * DA), prev),
            pl.BlockSpec((1, tq, KVA * DA), lambda b, i, *_: (b, i, 0)),
            pl.BlockSpec((1, WINDOW, KVA * DA), nxt),
            pl.BlockSpec((1, KVA * V_ROWS, WINDOW), prev_t),
            pl.BlockSpec((1, KVA * V_ROWS, tq), lambda b, i, *_: (b, 0, i)),
            pl.BlockSpec((1, KVA * V_ROWS, WINDOW), nxt_t),
        ],
        out_specs=pl.BlockSpec((1, HA * DA, tq), lambda b, i, *_: (b, 0, i)),
        scratch_shapes=[pltpu.VMEM((2, 3 * WINDOW, GA * WINDOW), F32)],
    )
    return pl.pallas_call(
        functools.partial(_swa_kernel, seq=S),
        grid_spec=grid_spec,
        out_shape=jax.ShapeDtypeStruct((B, HA * DA, S), BF16),
        compiler_params=pltpu.CompilerParams(
            dimension_semantics=("parallel", "parallel"), vmem_limit_bytes=VMEM_LIMIT),
        name="swa",
    )(sink, qaT, ka, ka, ka, vaT, vaT, vaT)


def _mla_kernel(qT_ref, k_ref, vT_ref, oT_ref, s_ref):
    nt, _, tw = qT_ref.shape[2:]
    nvb, nv, tv = vT_ref.shape[2:]
    tk, tq = s_ref.shape[1:]
    nkb = nvb * tv // tk
    vper = tk // tv
    sub = tw // tq
    nq = nt * sub

    def scores(slot, tile, half, j):
        q = qT_ref[0, 0, tile, :, half * tq:(half + 1) * tq]
        cmax = None
        for r in range(0, tk, MLA_SCORE_ROWS):
            s = _dot(k_ref[0, j * tk + r:j * tk + r + MLA_SCORE_ROWS, :], q)
            s_ref[slot, r:r + MLA_SCORE_ROWS] = s
            cm = jnp.max(s, axis=0, keepdims=True)
            cmax = cm if cmax is None else jnp.maximum(cmax, cm)
        return cmax

    def softmax_pv(slot, j, cmax, m, acc):
        m_new = jnp.maximum(m, cmax)
        alpha = jnp.exp2(m - m_new)
        p = jnp.exp2(s_ref[slot] - m_new).astype(BF16)
        pv = _dot(vT_ref[0, 0, j * vper, :, :LANE], p[:LANE])
        pv += _dot(vT_ref[0, 0, j * vper, :, LANE:], p[LANE:tv])
        for c in range(1, vper):
            pv += _dot(vT_ref[0, 0, j * vper + c], p[c * tv:(c + 1) * tv])
        return m_new, alpha * acc + pv

    unroll = min(nq, MLA_Q_TILES_PER_TRIP)
    assert nq % unroll == 0 and unroll % sub == 0
    tiles_per_trip = unroll // sub

    def q_tiles(t, cmax):
        for u in range(unroll):
            tile, half = t * tiles_per_trip + u // sub, u % sub
            m, acc = jnp.full((1, tq), -jnp.inf, F32), jnp.zeros((nv, tq), F32)
            for j in range(nkb):
                if j + 1 < nkb:
                    nxt = scores((j + 1) % 2, tile, half, j + 1)
                elif u + 1 < unroll:
                    nxt = scores(0, t * tiles_per_trip + (u + 1) // sub, (u + 1) % sub, 0)
                else:
                    nxt = scores(0, jnp.minimum((t + 1) * tiles_per_trip, nt - 1), 0, 0)
                m, acc = softmax_pv(j % 2, j, cmax, m, acc)
                cmax = nxt
            oT_ref[0, 0, tile, :, half * tq:(half + 1) * tq] = (acc[:DV] / acc[DV:DV + 1]).astype(BF16)
        return cmax

    lax.fori_loop(0, nq // unroll, q_tiles, scores(0, 0, 0, 0))


def _mla(qT, k, vT):
    B, _, nq, _, tq = qT.shape
    nvb, nv, tv = vT.shape[2:]
    S = nq * tq
    tk = MLA_KEY_BLOCK
    assert tk % tv == 0 and S % (2 * tk) == 0
    return pl.pallas_call(
        _mla_kernel,
        grid=(B, HB),
        in_specs=[
            pl.BlockSpec((1, 1, nq, LANE, tq), lambda b, h: (b, h, 0, 0, 0)),
            pl.BlockSpec((1, S, LANE), lambda b, h: (b, 0, h)),
            pl.BlockSpec((1, 1, nvb, nv, tv), lambda b, h: (b, h, 0, 0, 0)),
        ],
        out_specs=pl.BlockSpec((1, 1, nq, DV, tq), lambda b, h: (b, h, 0, 0, 0)),
        out_shape=jax.ShapeDtypeStruct((B, HB, nq, DV, tq), BF16),
        scratch_shapes=[pltpu.VMEM((2, tk, MLA_Q_TILE), F32)],
        compiler_params=pltpu.CompilerParams(
            dimension_semantics=("parallel", "parallel"), vmem_limit_bytes=VMEM_LIMIT),
        name="mla",
    )(qT, k, vT)


def _ffn_kernel(x_ref, oaT_ref, obT_ref, woa_ref, wob_ref, gffn_ref, wg_ref, wu_ref, wd_ref, gfin_ref, y_ref):
    tm = x_ref.shape[1]
    half = tm // 2
    rows = (slice(0, half), slice(half, tm))

    def norm(x, g_ref):
        return x * lax.rsqrt(jnp.mean(x * x, axis=-1, keepdims=True) + EPS) * g_ref[...]

    def attn_out(r):
        obT = obT_ref[0, :, 0, :, r].reshape(HB * DV, half)
        return x_ref[0, r] + _dot_tn(oaT_ref[0, :, r], woa_ref[...]) + _dot_tn(obT, wob_ref[...])

    def gate_up(x):
        h = norm(x, gffn_ref).astype(BF16)
        return _dot(h, wg_ref[...]), _dot(h, wu_ref[...])

    def act(g, u):
        return (g / (1.0 + jnp.exp(-g)) * u).astype(BF16)

    xa = attn_out(rows[0])
    xb = attn_out(rows[1])
    ga, ua = gate_up(xa)
    gb, ub = gate_up(xb)
    xa = xa + _dot(act(ga, ua), wd_ref[...])
    xb = xb + _dot(act(gb, ub), wd_ref[...])
    y_ref[0, rows[0]] = norm(xa, gfin_ref)
    y_ref[0, rows[1]] = norm(xb, gfin_ref)


def _ffn(x, oaT, obT, p):
    B, S, _ = x.shape
    tm = TOK_TILE
    full = lambda a: pl.BlockSpec(a.shape, lambda b, i: (0,) * a.ndim, pipeline_mode=pl.Buffered(1))
    weights = (p["w_oa"], p["w_ob"], p["g_ffn"], p["w_gate"], p["w_up"], p["w_down"], p["g_final"])
    return pl.pallas_call(
        _ffn_kernel,
        grid=(B, S // tm),
        in_specs=[
            pl.BlockSpec((1, tm, D_MODEL), lambda b, i: (b, i, 0)),
            pl.BlockSpec((1, HA * DA, tm), lambda b, i: (b, 0, i)),
            pl.BlockSpec((1, HB, 1, DV, tm), lambda b, i: (b, 0, i, 0, 0)),
        ] + [full(w) for w in weights],
        out_specs=pl.BlockSpec((1, tm, D_MODEL), lambda b, i: (b, i, 0)),
        out_shape=jax.ShapeDtypeStruct(x.shape, F32),
        compiler_params=pltpu.CompilerParams(
            dimension_semantics=("parallel", "parallel"), vmem_limit_bytes=VMEM_LIMIT),
        name="ffn",
    )(x, oaT, obT, *weights)


def _rope_tables_t(seq, dim):
    inv = 1.0 / (ROPE_THETA ** (jnp.arange(0, dim, 2, dtype=F32) / dim))
    ang = jnp.arange(seq, dtype=F32)[:, None] * inv[None, :]
    return jnp.cos(ang).T, jnp.sin(ang).T


def _prep_params(g_mix, w_in, sink, cq_g, w_uq, ckv_g, w_ukv, w_o, g_ffn, w_gate, w_up, w_down, g_final):
    w_uq3 = w_uq.reshape(Q_RANK, HB, DN + DR)
    w_uq_pad = jnp.pad(w_uq3, ((0, 0), (0, 0), (0, LANE - DN - DR))).reshape(Q_RANK, HB * LANE)
    w_ukv3 = w_ukv.reshape(KV_RANK, HB, DN + DV)
    w_uk, w_uv = w_ukv3[..., :DN], w_ukv3[..., DN:]
    top = jnp.pad(w_uk, ((0, 0), (0, 0), (0, LANE - DN)))
    eye = jnp.pad(jnp.eye(DR, dtype=F32)[:, None, :], ((0, 0), (0, 0), (DN, LANE - DN - DR)))
    bot = jnp.broadcast_to(eye, (DR, HB, LANE))
    w_k = jnp.concatenate([top, bot], axis=0).reshape(KV_RANK + DR, HB * LANE)
    return {
        "g_mix": g_mix.reshape(1, D_MODEL),
        "w_inT": w_in.T.astype(BF16),
        "cq_g": cq_g.reshape(Q_RANK, 1),
        "w_uqT": w_uq_pad.T.astype(BF16),
        "ckv_g": ckv_g.reshape(KV_RANK, 1),
        "w_k": w_k.astype(BF16),
        "w_vT": w_uv.reshape(KV_RANK, HB * DV).T.astype(BF16),
        "sink": sink,
        "w_oa": w_o[:HA * DA].astype(BF16),
        "w_ob": w_o[HA * DA:].astype(BF16),
        "g_ffn": g_ffn.reshape(1, D_MODEL),
        "w_gate": w_gate.astype(BF16),
        "w_up": w_up.astype(BF16),
        "w_down": w_down.astype(BF16),
        "g_final": g_final.reshape(1, D_MODEL),
    }


def _trunk(x, p):
    S = x.shape[1]
    tabs = _rope_tables_t(S, DA) + _rope_tables_t(S, DR)
    qaT, ka, vaT, qT, k, vT = _proj(x, p, tabs)
    oaT = _swa(p["sink"], qaT, ka, vaT)
    obT = _mla(qT, k, vT)
    return _ffn(x, oaT, obT, p)


def kernel(x_prompt, x_sample, g_mix, w_in, sink, cq_g, w_uq, ckv_g, w_ukv, w_o, g_ffn,
           w_gate, w_up, w_down, g_final):
    p = _prep_params(g_mix[0], w_in[0], sink[0], cq_g[0], w_uq[0], ckv_g[0], w_ukv[0], w_o[0],
                     g_ffn[0], w_gate[0], w_up[0], w_down[0], g_final)
    return _trunk(x_prompt, p), _trunk(x_sample, p)
```

```python
import functools
import math

import jax
import jax.numpy as jnp
from jax import lax
from jax.experimental import pallas as pl
from jax.experimental.pallas import tpu as pltpu

D_MODEL = 1024
EPS = 1e-6
ROPE_THETA = 10000.0
WINDOW = 128
HA, KVA, DA = 8, 2, 64
GA = HA // KVA
HB, Q_RANK, KV_RANK, DN, DR, DV = 8, 384, 256, 64, 32, 64
D_IN = HA * DA + 2 * KVA * DA + Q_RANK + KV_RANK + DR
D_FF = int(math.ceil(8 * D_MODEL / 3 / 256) * 256)
NEG = -1e30

LANE = 128
V_ROWS = 128
LOG2E = math.log2(math.e)
SWA_Q_SCALE = DA ** -0.5 * LOG2E
MLA_Q_SCALE = (DN + DR) ** -0.5 * LOG2E
OFF_KA = HA * DA
OFF_VA = OFF_KA + KVA * DA
OFF_CQ = OFF_VA + KVA * DA
OFF_CKV = OFF_CQ + Q_RANK
OFF_KR = OFF_CKV + KV_RANK

TOK_TILE = 512
SWA_TILE = 1024
MLA_Q_TILE = 256
MLA_KEY_BLOCK = 2048
MLA_SCORE_ROWS = 1024
MLA_Q_TILES_PER_TRIP = 8
VMEM_LIMIT = 56 * 1024 * 1024

F32 = jnp.float32
BF16 = jnp.bfloat16


def _dot(a, b):
    return jnp.dot(a, b, preferred_element_type=F32)


def _dot_tn(a, b):
    return lax.dot_general(a, b, (((0,), (0,)), ((), ())), preferred_element_type=F32)


def _dot_nt(a, b):
    return lax.dot_general(a, b, (((1,), (1,)), ((), ())), preferred_element_type=F32)


def _rope_rows(x1, x2, c, s):
    return x1 * c - x2 * s, x2 * c + x1 * s


def _proj_kernel(x_ref, gmix_ref, winT_ref, cqg_ref, wuqT_ref, ckvg_ref, wk_ref, wvT_ref,
                 cosa_ref, sina_ref, cosr_ref, sinr_ref,
                 qaT_ref, ka_ref, vaT_ref, qT_ref, k_ref, vT_ref):
    tm = x_ref.shape[1]
    sub = tm // 2
    half, hr = DA // 2, DR // 2

    def project(r):
        x = x_ref[0, r]
        ms = jnp.mean(x * x, axis=-1, keepdims=True)
        h = (x * lax.rsqrt(ms + EPS) * gmix_ref[...]).astype(BF16)
        return _dot_nt(winT_ref[...], h)

    def emit(r, zT):
        ca, sa = cosa_ref[:, r], sina_ref[:, r]
        cr, sr = cosr_ref[:, r], sinr_ref[:, r]

        qaT_ref[0, :, r] = jnp.zeros((HA * LANE, sub), BF16)
        for hq in range(HA):
            o = hq * DA
            o1, o2 = _rope_rows(zT[o:o + half], zT[o + half:o + DA], ca, sa)
            base = hq * LANE + (hq // GA) * DA
            qaT_ref[0, base:base + half, r] = (o1 * SWA_Q_SCALE).astype(BF16)
            qaT_ref[0, base + half:base + DA, r] = (o2 * SWA_Q_SCALE).astype(BF16)

        kaT = []
        for g in range(KVA):
            o = OFF_KA + g * DA
            kaT.extend(_rope_rows(zT[o:o + half], zT[o + half:o + DA], ca, sa))
        ka_ref[0, r] = jnp.concatenate(kaT, axis=0).T.astype(BF16)
        ones_row = (lax.broadcasted_iota(jnp.int32, (V_ROWS - DV, sub), 0) == 0).astype(BF16)
        for g in range(KVA):
            o = OFF_VA + g * DA
            vaT_ref[0, g * V_ROWS:g * V_ROWS + DA, r] = zT[o:o + DA].astype(BF16)
            vaT_ref[0, g * V_ROWS + DA:(g + 1) * V_ROWS, r] = ones_row

        cq = zT[OFF_CQ:OFF_CKV]
        cqn = (cq * lax.rsqrt(jnp.mean(cq * cq, axis=0, keepdims=True) + EPS) * cqg_ref[...]).astype(BF16)
        qT = _dot(wuqT_ref[...], cqn) * MLA_Q_SCALE
        qT_ref[0, :, 0, :, r] = qT.astype(BF16).reshape(HB, LANE, sub)
        for hh in range(HB):
            o = hh * LANE + DN
            o1, o2 = _rope_rows(qT[o:o + hr], qT[o + hr:o + DR], cr, sr)
            qT_ref[0, hh, 0, DN:DN + hr, r] = o1.astype(BF16)
            qT_ref[0, hh, 0, DN + hr:DN + DR, r] = o2.astype(BF16)

        ckv = zT[OFF_CKV:OFF_KR]
        ckvn = (ckv * lax.rsqrt(jnp.mean(ckv * ckv, axis=0, keepdims=True) + EPS) * ckvg_ref[...]).astype(BF16)
        krT = zT[OFF_KR:D_IN]
        k1, k2 = _rope_rows(krT[:hr], krT[hr:], cr, sr)
        kin = jnp.concatenate([ckvn, k1.astype(BF16), k2.astype(BF16)], axis=0)
        k_ref[0, r] = _dot_tn(kin, wk_ref[...]).astype(BF16)
        vT_ref[0, :, 0, :DV, r] = _dot(wvT_ref[...], ckvn).astype(BF16).reshape(HB, DV, sub)
        vT_ref[0, :, 0, DV:, r] = jnp.broadcast_to(ones_row, (HB, V_ROWS - DV, sub))

    ra, rb = slice(0, sub), slice(sub, tm)
    za = project(ra)
    zb = project(rb)
    emit(ra, za)
    emit(rb, zb)


def _proj(x, p, tabs):
    B, S, _ = x.shape
    tm = TOK_TILE
    nt = S // tm
    full = lambda a: pl.BlockSpec(a.shape, lambda b, i: (0,) * a.ndim)
    tab = lambda a: pl.BlockSpec((a.shape[0], tm), lambda b, i: (0, i))
    weights = (p["g_mix"], p["w_inT"], p["cq_g"], p["w_uqT"], p["ckv_g"], p["w_k"], p["w_vT"])
    out_shape = (
        jax.ShapeDtypeStruct((B, HA * LANE, S), BF16),
        jax.ShapeDtypeStruct((B, S, KVA * DA), BF16),
        jax.ShapeDtypeStruct((B, KVA * V_ROWS, S), BF16),
        jax.ShapeDtypeStruct((B, HB, nt, LANE, tm), BF16),
        jax.ShapeDtypeStruct((B, S, HB * LANE), BF16),
        jax.ShapeDtypeStruct((B, HB, nt, V_ROWS, tm), BF16),
    )
    out_specs = (
        pl.BlockSpec((1, HA * LANE, tm), lambda b, i: (b, 0, i)),
        pl.BlockSpec((1, tm, KVA * DA), lambda b, i: (b, i, 0)),
        pl.BlockSpec((1, KVA * V_ROWS, tm), lambda b, i: (b, 0, i)),
        pl.BlockSpec((1, HB, 1, LANE, tm), lambda b, i: (b, 0, i, 0, 0)),
        pl.BlockSpec((1, tm, HB * LANE), lambda b, i: (b, i, 0)),
        pl.BlockSpec((1, HB, 1, V_ROWS, tm), lambda b, i: (b, 0, i, 0, 0)),
    )
    return pl.pallas_call(
        _proj_kernel,
        grid=(B, nt),
        in_specs=[pl.BlockSpec((1, tm, D_MODEL), lambda b, i: (b, i, 0))]
        + [full(w) for w in weights] + [tab(t) for t in tabs],
        out_specs=out_specs,
        out_shape=out_shape,
        compiler_params=pltpu.CompilerParams(
            dimension_semantics=("parallel", "parallel"), vmem_limit_bytes=VMEM_LIMIT),
        name="proj",
    )(x, *weights, *tabs)


def _swa_kernel(sink_ref, qaT_ref, kp_ref, kc_ref, kn_ref, vp_ref, vc_ref, vn_ref, oT_ref, s_ref, *, seq):
    tq = qaT_ref.shape[2]
    q0 = pl.program_id(1) * tq
    kcat = jnp.concatenate([kp_ref[0], kc_ref[0], kn_ref[0]], axis=0)
    vcat = jnp.concatenate([vp_ref[0], vc_ref[0], vn_ref[0]], axis=1)
    nk = 3 * WINDOW
    wide = GA * WINDOW
    r = lax.broadcasted_iota(jnp.int32, (nk, wide), 0)
    lane = lax.broadcasted_iota(jnp.int32, (nk, wide), 1)
    d = r - (lane & (WINDOW - 1))
    band = (d >= 0) & (d <= 2 * WINDOW)
    head = lax.broadcasted_iota(jnp.int32, (1, wide), 1) // WINDOW
    units = [(c, g) for c in range(tq // WINDOW) for g in range(KVA)]

    def scores(slot, c, g):
        q4 = jnp.concatenate(
            [qaT_ref[0, (g * GA + h) * LANE:(g * GA + h + 1) * LANE, c * WINDOW:(c + 1) * WINDOW]
             for h in range(GA)], axis=1)
        valid = band
        if c in (0, tq // WINDOW - 1):
            kpos = q0 + (c - 1) * WINDOW + r
            valid = band & (kpos >= 0) & (kpos < seq)
        s = jnp.where(valid, _dot(kcat[c * WINDOW:c * WINDOW + nk], q4), NEG)
        s_ref[slot] = s
        return jnp.max(s, axis=0, keepdims=True)

    def finish(slot, c, g, cmax):
        sink = jnp.zeros((1, wide), F32)
        for h in range(GA):
            sink = jnp.where(head == h, sink_ref[g * GA + h] * LOG2E, sink)
        m = jnp.maximum(cmax, sink)
        p = jnp.exp2(s_ref[slot] - m).astype(BF16)
        pv = _dot(vcat[g * V_ROWS:(g + 1) * V_ROWS, c * WINDOW:c * WINDOW + nk], p)
        o = (pv[:DA] / (pv[DA:DA + 1] + jnp.exp2(sink - m))).astype(BF16)
        for h in range(GA):
            hq = g * GA + h
            oT_ref[0, hq * DA:(hq + 1) * DA, c * WINDOW:(c + 1) * WINDOW] = o[:, h * WINDOW:(h + 1) * WINDOW]

    cmax = scores(0, *units[0])
    for u, (c, g) in enumerate(units):
        nxt = scores((u + 1) % 2, *units[u + 1]) if u + 1 < len(units) else None
        finish(u % 2, c, g, cmax)
        cmax = nxt


def _swa(sink, qaT, ka, vaT):
    B, _, S = qaT.shape
    tq = SWA_TILE
    r = tq // WINDOW
    nb = S // WINDOW
    prev = lambda b, i, *_: (b, jnp.maximum(i * r - 1, 0), 0)
    nxt = lambda b, i, *_: (b, jnp.minimum(i * r + r, nb - 1), 0)
    prev_t = lambda b, i, *_: (b, 0, jnp.maximum(i * r - 1, 0))
    nxt_t = lambda b, i, *_: (b, 0, jnp.minimum(i * r + r, nb - 1))
    grid_spec = pltpu.PrefetchScalarGridSpec(
        num_scalar_prefetch=1,
        grid=(B, S // tq),
        in_specs=[
            pl.BlockSpec((1, HA * LANE, tq), lambda b, i, *_: (b, 0, i)),
            pl.BlockSpec((1, WINDOW, KVA * DA), prev),
            pl.BlockSpec((1, tq, KVA * DA), lambda b, i, *_: (b, i, 0)),
            pl.BlockSpec((1, WINDOW, KVA * DA), nxt),
            pl.BlockSpec((1, KVA * V_ROWS, WINDOW), prev_t),
            pl.BlockSpec((1, KVA * V_ROWS, tq), lambda b, i, *_: (b, 0, i)),
            pl.BlockSpec((1, KVA * V_ROWS, WINDOW), nxt_t),
        ],
        out_specs=pl.BlockSpec((1, HA * DA, tq), lambda b, i, *_: (b, 0, i)),
        scratch_shapes=[pltpu.VMEM((2, 3 * WINDOW, GA * WINDOW), F32)],
    )
    return pl.pallas_call(
        functools.partial(_swa_kernel, seq=S),
        grid_spec=grid_spec,
        out_shape=jax.ShapeDtypeStruct((B, HA * DA, S), BF16),
        compiler_params=pltpu.CompilerParams(
            dimension_semantics=("parallel", "parallel"), vmem_limit_bytes=VMEM_LIMIT),
        name="swa",
    )(sink, qaT, ka, ka, ka, vaT, vaT, vaT)


def _mla_kernel(qT_ref, k_ref, vT_ref, oT_ref, s_ref):
    nt, _, tw = qT_ref.shape[2:]
    nvb, nv, tv = vT_ref.shape[2:]
    tk, tq = s_ref.shape[1:]
    nkb = nvb * tv // tk
    vper = tk // tv
    sub = tw // tq
    nq = nt * sub

    def scores(slot, tile, half, j):
        q = qT_ref[0, 0, tile, :, half * tq:(half + 1) * tq]
        cmax = None
        for r in range(0, tk, MLA_SCORE_ROWS):
            s = _dot(k_ref[0, j * tk + r:j * tk + r + MLA_SCORE_ROWS, :], q)
            s_ref[slot, r:r + MLA_SCORE_ROWS] = s
            cm = jnp.max(s, axis=0, keepdims=True)
            cmax = cm if cmax is None else jnp.maximum(cmax, cm)
        return cmax

    def softmax_pv(slot, j, cmax, m, acc):
        m_new = jnp.maximum(m, cmax)
        alpha = jnp.exp2(m - m_new)
        p = jnp.exp2(s_ref[slot] - m_new).astype(BF16)
        pv = _dot(vT_ref[0, 0, j * vper, :, :LANE], p[:LANE])
        pv += _dot(vT_ref[0, 0, j * vper, :, LANE:], p[LANE:tv])
        for c in range(1, vper):
            pv += _dot(vT_ref[0, 0, j * vper + c], p[c * tv:(c + 1) * tv])
        return m_new, alpha * acc + pv

    unroll = min(nq, MLA_Q_TILES_PER_TRIP)
    assert nq % unroll == 0 and unroll % sub == 0
    tiles_per_trip = unroll // sub

    def q_tiles(t, cmax):
        for u in range(unroll):
            tile, half = t * tiles_per_trip + u // sub, u % sub
            m, acc = jnp.full((1, tq), -jnp.inf, F32), jnp.zeros((nv, tq), F32)
            for j in range(nkb):
                if j + 1 < nkb:
                    nxt = scores((j + 1) % 2, tile, half, j + 1)
                elif u + 1 < unroll:
                    nxt = scores(0, t * tiles_per_trip + (u + 1) // sub, (u + 1) % sub, 0)
                else:
                    nxt = scores(0, jnp.minimum((t + 1) * tiles_per_trip, nt - 1), 0, 0)
                m, acc = softmax_pv(j % 2, j, cmax, m, acc)
                cmax = nxt
            oT_ref[0, 0, tile, :, half * tq:(half + 1) * tq] = (acc[:DV] / acc[DV:DV + 1]).astype(BF16)
        return cmax

    lax.fori_loop(0, nq // unroll, q_tiles, scores(0, 0, 0, 0))


def _mla(qT, k, vT):
    B, _, nq, _, tq = qT.shape
    nvb, nv, tv = vT.shape[2:]
    S = nq * tq
    tk = MLA_KEY_BLOCK
    assert tk % tv == 0 and S % (2 * tk) == 0
    return pl.pallas_call(
        _mla_kernel,
        grid=(B, HB),
        in_specs=[
            pl.BlockSpec((1, 1, nq, LANE, tq), lambda b, h: (b, h, 0, 0, 0)),
            pl.BlockSpec((1, S, LANE), lambda b, h: (b, 0, h)),
            pl.BlockSpec((1, 1, nvb, nv, tv), lambda b, h: (b, h, 0, 0, 0)),
        ],
        out_specs=pl.BlockSpec((1, 1, nq, DV, tq), lambda b, h: (b, h, 0, 0, 0)),
        out_shape=jax.ShapeDtypeStruct((B, HB, nq, DV, tq), BF16),
        scratch_shapes=[pltpu.VMEM((2, tk, MLA_Q_TILE), F32)],
        compiler_params=pltpu.CompilerParams(
            dimension_semantics=("parallel", "parallel"), vmem_limit_bytes=VMEM_LIMIT),
        name="mla",
    )(qT, k, vT)


def _ffn_kernel(x_ref, oaT_ref, obT_ref, woa_ref, wob_ref, gffn_ref, wg_ref, wu_ref, wd_ref, gfin_ref, y_ref):
    tm = x_ref.shape[1]
    half = tm // 2
    rows = (slice(0, half), slice(half, tm))

    def norm(x, g_ref):
        return x * lax.rsqrt(jnp.mean(x * x, axis=-1, keepdims=True) + EPS) * g_ref[...]

    def attn_out(r):
        obT = obT_ref[0, :, 0, :, r].reshape(HB * DV, half)
        return x_ref[0, r] + _dot_tn(oaT_ref[0, :, r], woa_ref[...]) + _dot_tn(obT, wob_ref[...])

    def gate_up(x):
        h = norm(x, gffn_ref).astype(BF16)
        return _dot(h, wg_ref[...]), _dot(h, wu_ref[...])

    def act(g, u):
        return (g / (1.0 + jnp.exp(-g)) * u).astype(BF16)

    xa = attn_out(rows[0])
    xb = attn_out(rows[1])
    ga, ua = gate_up(xa)
    gb, ub = gate_up(xb)
    xa = xa + _dot(act(ga, ua), wd_ref[...])
    xb = xb + _dot(act(gb, ub), wd_ref[...])
    y_ref[0, rows[0]] = norm(xa, gfin_ref)
    y_ref[0, rows[1]] = norm(xb, gfin_ref)


def _ffn(x, oaT, obT, p):
    B, S, _ = x.shape
    tm = TOK_TILE
    full = lambda a: pl.BlockSpec(a.shape, lambda b, i: (0,) * a.ndim, pipeline_mode=pl.Buffered(1))
    weights = (p["w_oa"], p["w_ob"], p["g_ffn"], p["w_gate"], p["w_up"], p["w_down"], p["g_final"])
    return pl.pallas_call(
        _ffn_kernel,
        grid=(B, S // tm),
        in_specs=[
            pl.BlockSpec((1, tm, D_MODEL), lambda b, i: (b, i, 0)),
            pl.BlockSpec((1, HA * DA, tm), lambda b, i: (b, 0, i)),
            pl.BlockSpec((1, HB, 1, DV, tm), lambda b, i: (b, 0, i, 0, 0)),
        ] + [full(w) for w in weights],
        out_specs=pl.BlockSpec((1, tm, D_MODEL), lambda b, i: (b, i, 0)),
        out_shape=jax.ShapeDtypeStruct(x.shape, F32),
        compiler_params=pltpu.CompilerParams(
            dimension_semantics=("parallel", "parallel"), vmem_limit_bytes=VMEM_LIMIT),
        name="ffn",
    )(x, oaT, obT, *weights)


def _rope_tables_t(seq, dim):
    inv = 1.0 / (ROPE_THETA ** (jnp.arange(0, dim, 2, dtype=F32) / dim))
    ang = jnp.arange(seq, dtype=F32)[:, None] * inv[None, :]
    return jnp.cos(ang).T, jnp.sin(ang).T


def _prep_params(g_mix, w_in, sink, cq_g, w_uq, ckv_g, w_ukv, w_o, g_ffn, w_gate, w_up, w_down, g_final):
    w_uq3 = w_uq.reshape(Q_RANK, HB, DN + DR)
    w_uq_pad = jnp.pad(w_uq3, ((0, 0), (0, 0), (0, LANE - DN - DR))).reshape(Q_RANK, HB * LANE)
    w_ukv3 = w_ukv.reshape(KV_RANK, HB, DN + DV)
    w_uk, w_uv = w_ukv3[..., :DN], w_ukv3[..., DN:]
    top = jnp.pad(w_uk, ((0, 0), (0, 0), (0, LANE - DN)))
    eye = jnp.pad(jnp.eye(DR, dtype=F32)[:, None, :], ((0, 0), (0, 0), (DN, LANE - DN - DR)))
    bot = jnp.broadcast_to(eye, (DR, HB, LANE))
    w_k = jnp.concatenate([top, bot], axis=0).reshape(KV_RANK + DR, HB * LANE)
    return {
        "g_mix": g_mix.reshape(1, D_MODEL),
        "w_inT": w_in.T.astype(BF16),
        "cq_g": cq_g.reshape(Q_RANK, 1),
        "w_uqT": w_uq_pad.T.astype(BF16),
        "ckv_g": ckv_g.reshape(KV_RANK, 1),
        "w_k": w_k.astype(BF16),
        "w_vT": w_uv.reshape(KV_RANK, HB * DV).T.astype(BF16),
        "sink": sink,
        "w_oa": w_o[:HA * DA].astype(BF16),
        "w_ob": w_o[HA * DA:].astype(BF16),
        "g_ffn": g_ffn.reshape(1, D_MODEL),
        "w_gate": w_gate.astype(BF16),
        "w_up": w_up.astype(BF16),
        "w_down": w_down.astype(BF16),
        "g_final": g_final.reshape(1, D_MODEL),
    }


def _trunk(x, p):
    S = x.shape[1]
    tabs = _rope_tables_t(S, DA) + _rope_tables_t(S, DR)
    qaT, ka, vaT, qT, k, vT = _proj(x, p, tabs)
    oaT = _swa(p["sink"], qaT, ka, vaT)
    obT = _mla(qT, k, vT)
    return _ffn(x, oaT, obT, p)


def kernel(x_prompt, x_sample, g_mix, w_in, sink, cq_g, w_uq, ckv_g, w_ukv, w_o, g_ffn,
           w_gate, w_up, w_down, g_final):
    p = _prep_params(g_mix[0], w_in[0], sink[0], cq_g[0], w_uq[0], ckv_g[0], w_ukv[0], w_o[0],
                     g_ffn[0], w_gate[0], w_up[0], w_down[0], g_final)
    return _trunk(x_prompt, p), _trunk(x_sample, p)
```

```python
import functools
import math

import jax
import jax.numpy as jnp
from jax import lax
from jax.experimental import pallas as pl
from jax.experimental.pallas import tpu as pltpu

D_MODEL = 1024
EPS = 1e-6
ROPE_THETA = 10000.0
WINDOW = 128
HA, KVA, DA = 8, 2, 64
GA = HA // KVA
HB, Q_RANK, KV_RANK, DN, DR, DV = 8, 384, 256, 64, 32, 64
D_IN = HA * DA + 2 * KVA * DA + Q_RANK + KV_RANK + DR
D_FF = int(math.ceil(8 * D_MODEL / 3 / 256) * 256)
NEG = -1e30

LANE = 128
V_ROWS = 128
LOG2E = math.log2(math.e)
SWA_Q_SCALE = DA ** -0.5 * LOG2E
MLA_Q_SCALE = (DN + DR) ** -0.5 * LOG2E
OFF_KA = HA * DA
OFF_VA = OFF_KA + KVA * DA
OFF_CQ = OFF_VA + KVA * DA
OFF_CKV = OFF_CQ + Q_RANK
OFF_KR = OFF_CKV + KV_RANK

TOK_TILE = 512
SWA_TILE = 1024
MLA_Q_TILE = 256
MLA_KEY_BLOCK = 2048
MLA_SCORE_ROWS = 1024
MLA_Q_TILES_PER_TRIP = 8
VMEM_LIMIT = 56 * 1024 * 1024

F32 = jnp.float32
BF16 = jnp.bfloat16


def _dot(a, b):
    return jnp.dot(a, b, preferred_element_type=F32)


def _dot_tn(a, b):
    return lax.dot_general(a, b, (((0,), (0,)), ((), ())), preferred_element_type=F32)


def _dot_nt(a, b):
    return lax.dot_general(a, b, (((1,), (1,)), ((), ())), preferred_element_type=F32)


def _rope_rows(x1, x2, c, s):
    return x1 * c - x2 * s, x2 * c + x1 * s


def _proj_kernel(x_ref, gmix_ref, winT_ref, cqg_ref, wuqT_ref, ckvg_ref, wk_ref, wvT_ref,
                 cosa_ref, sina_ref, cosr_ref, sinr_ref,
                 qaT_ref, ka_ref, vaT_ref, qT_ref, k_ref, vT_ref):
    x = x_ref[0]
    ms = jnp.mean(x * x, axis=-1, keepdims=True)
    h = (x * lax.rsqrt(ms + EPS) * gmix_ref[...]).astype(BF16)
    zT = _dot_nt(winT_ref[...], h)

    ca, sa = cosa_ref[...], sina_ref[...]
    cr, sr = cosr_ref[...], sinr_ref[...]
    half = DA // 2

    qaT_ref[0] = jnp.zeros(qaT_ref.shape[1:], BF16)
    for hq in range(HA):
        r = hq * DA
        o1, o2 = _rope_rows(zT[r:r + half], zT[r + half:r + DA], ca, sa)
        base = hq * LANE + (hq // GA) * DA
        qaT_ref[0, base:base + half, :] = (o1 * SWA_Q_SCALE).astype(BF16)
        qaT_ref[0, base + half:base + DA, :] = (o2 * SWA_Q_SCALE).astype(BF16)

    kaT = []
    for g in range(KVA):
        r = OFF_KA + g * DA
        kaT.extend(_rope_rows(zT[r:r + half], zT[r + half:r + DA], ca, sa))
    ka_ref[0] = jnp.concatenate(kaT, axis=0).T.astype(BF16)
    tm = zT.shape[1]
    ones_row = lax.broadcasted_iota(jnp.int32, (V_ROWS - DV, tm), 0) == 0
    for g in range(KVA):
        r = OFF_VA + g * DA
        vaT_ref[0, g * V_ROWS:g * V_ROWS + DA, :] = zT[r:r + DA].astype(BF16)
        vaT_ref[0, g * V_ROWS + DA:(g + 1) * V_ROWS, :] = ones_row.astype(BF16)

    cq = zT[OFF_CQ:OFF_CKV]
    cqn = (cq * lax.rsqrt(jnp.mean(cq * cq, axis=0, keepdims=True) + EPS) * cqg_ref[...]).astype(BF16)
    qT = _dot(wuqT_ref[...], cqn) * MLA_Q_SCALE
    qT_ref[0, :, 0] = qT.astype(BF16).reshape(HB, LANE, tm)
    hr = DR // 2
    for hh in range(HB):
        r = hh * LANE + DN
        o1, o2 = _rope_rows(qT[r:r + hr], qT[r + hr:r + DR], cr, sr)
        qT_ref[0, hh, 0, DN:DN + hr, :] = o1.astype(BF16)
        qT_ref[0, hh, 0, DN + hr:DN + DR, :] = o2.astype(BF16)

    ckv = zT[OFF_CKV:OFF_KR]
    ckvn = (ckv * lax.rsqrt(jnp.mean(ckv * ckv, axis=0, keepdims=True) + EPS) * ckvg_ref[...]).astype(BF16)
    krT = zT[OFF_KR:D_IN]
    k1, k2 = _rope_rows(krT[:hr], krT[hr:], cr, sr)
    kin = jnp.concatenate([ckvn, k1.astype(BF16), k2.astype(BF16)], axis=0)
    k_ref[0] = _dot_tn(kin, wk_ref[...]).astype(BF16)
    vT_ref[0, :, 0, :DV] = _dot(wvT_ref[...], ckvn).astype(BF16).reshape(HB, DV, tm)
    vT_ref[0, :, 0, DV:] = jnp.broadcast_to(ones_row.astype(BF16), (HB, V_ROWS - DV, tm))


def _proj(x, p, tabs):
    B, S, _ = x.shape
    tm = TOK_TILE
    nt = S // tm
    full = lambda a: pl.BlockSpec(a.shape, lambda b, i: (0,) * a.ndim)
    tab = lambda a: pl.BlockSpec((a.shape[0], tm), lambda b, i: (0, i))
    weights = (p["g_mix"], p["w_inT"], p["cq_g"], p["w_uqT"], p["ckv_g"], p["w_k"], p["w_vT"])
    out_shape = (
        jax.ShapeDtypeStruct((B, HA * LANE, S), BF16),
        jax.ShapeDtypeStruct((B, S, KVA * DA), BF16),
        jax.ShapeDtypeStruct((B, KVA * V_ROWS, S), BF16),
        jax.ShapeDtypeStruct((B, HB, nt, LANE, tm), BF16),
        jax.ShapeDtypeStruct((B, S, HB * LANE), BF16),
        jax.ShapeDtypeStruct((B, HB, nt, V_ROWS, tm), BF16),
    )
    out_specs = (
        pl.BlockSpec((1, HA * LANE, tm), lambda b, i: (b, 0, i)),
        pl.BlockSpec((1, tm, KVA * DA), lambda b, i: (b, i, 0)),
        pl.BlockSpec((1, KVA * V_ROWS, tm), lambda b, i: (b, 0, i)),
        pl.BlockSpec((1, HB, 1, LANE, tm), lambda b, i: (b, 0, i, 0, 0)),
        pl.BlockSpec((1, tm, HB * LANE), lambda b, i: (b, i, 0)),
        pl.BlockSpec((1, HB, 1, V_ROWS, tm), lambda b, i: (b, 0, i, 0, 0)),
    )
    return pl.pallas_call(
        _proj_kernel,
        grid=(B, nt),
        in_specs=[pl.BlockSpec((1, tm, D_MODEL), lambda b, i: (b, i, 0))]
        + [full(w) for w in weights] + [tab(t) for t in tabs],
        out_specs=out_specs,
        out_shape=out_shape,
        compiler_params=pltpu.CompilerParams(
            dimension_semantics=("parallel", "parallel"), vmem_limit_bytes=VMEM_LIMIT),
        name="proj",
    )(x, *weights, *tabs)


def _swa_kernel(sink_ref, qaT_ref, kp_ref, kc_ref, kn_ref, vp_ref, vc_ref, vn_ref, oT_ref, s_ref, *, seq):
    tq = qaT_ref.shape[2]
    q0 = pl.program_id(1) * tq
    kcat = jnp.concatenate([kp_ref[0], kc_ref[0], kn_ref[0]], axis=0)
    vcat = jnp.concatenate([vp_ref[0], vc_ref[0], vn_ref[0]], axis=1)
    nk = 3 * WINDOW
    wide = GA * WINDOW
    r = lax.broadcasted_iota(jnp.int32, (nk, wide), 0)
    lane = lax.broadcasted_iota(jnp.int32, (nk, wide), 1)
    d = r - (lane & (WINDOW - 1))
    band = (d >= 0) & (d <= 2 * WINDOW)
    head = lax.broadcasted_iota(jnp.int32, (1, wide), 1) // WINDOW
    units = [(c, g) for c in range(tq // WINDOW) for g in range(KVA)]

    def scores(slot, c, g):
        q4 = jnp.concatenate(
            [qaT_ref[0, (g * GA + h) * LANE:(g * GA + h + 1) * LANE, c * WINDOW:(c + 1) * WINDOW]
             for h in range(GA)], axis=1)
        valid = band
        if c in (0, tq // WINDOW - 1):
            kpos = q0 + (c - 1) * WINDOW + r
            valid = band & (kpos >= 0) & (kpos < seq)
        s = jnp.where(valid, _dot(kcat[c * WINDOW:c * WINDOW + nk], q4), NEG)
        s_ref[slot] = s
        return jnp.max(s, axis=0, keepdims=True)

    def finish(slot, c, g, cmax):
        sink = jnp.zeros((1, wide), F32)
        for h in range(GA):
            sink = jnp.where(head == h, sink_ref[g * GA + h] * LOG2E, sink)
        m = jnp.maximum(cmax, sink)
        p = jnp.exp2(s_ref[slot] - m).astype(BF16)
        pv = _dot(vcat[g * V_ROWS:(g + 1) * V_ROWS, c * WINDOW:c * WINDOW + nk], p)
        o = (pv[:DA] / (pv[DA:DA + 1] + jnp.exp2(sink - m))).astype(BF16)
        for h in range(GA):
            hq = g * GA + h
            oT_ref[0, hq * DA:(hq + 1) * DA, c * WINDOW:(c + 1) * WINDOW] = o[:, h * WINDOW:(h + 1) * WINDOW]

    cmax = scores(0, *units[0])
    for u, (c, g) in enumerate(units):
        nxt = scores((u + 1) % 2, *units[u + 1]) if u + 1 < len(units) else None
        finish(u % 2, c, g, cmax)
        cmax = nxt


def _swa(sink, qaT, ka, vaT):
    B, _, S = qaT.shape
    tq = SWA_TILE
    r = tq // WINDOW
    nb = S // WINDOW
    prev = lambda b, i, *_: (b, jnp.maximum(i * r - 1, 0), 0)
    nxt = lambda b, i, *_: (b, jnp.minimum(i * r + r, nb - 1), 0)
    prev_t = lambda b, i, *_: (b, 0, jnp.maximum(i * r - 1, 0))
    nxt_t = lambda b, i, *_: (b, 0, jnp.minimum(i * r + r, nb - 1))
    grid_spec = pltpu.PrefetchScalarGridSpec(
        num_scalar_prefetch=1,
        grid=(B, S // tq),
        in_specs=[
            pl.BlockSpec((1, HA * LANE, tq), lambda b, i, *_: (b, 0, i)),
            pl.BlockSpec((1, WINDOW, KVA * DA), prev),
            pl.BlockSpec((1, tq, KVA * DA), lambda b, i, *_: (b, i, 0)),
            pl.BlockSpec((1, WINDOW, KVA * DA), nxt),
            pl.BlockSpec((1, KVA * V_ROWS, WINDOW), prev_t),
            pl.BlockSpec((1, KVA * V_ROWS, tq), lambda b, i, *_: (b, 0, i)),
            pl.BlockSpec((1, KVA * V_ROWS, WINDOW), nxt_t),
        ],
        out_specs=pl.BlockSpec((1, HA * DA, tq), lambda b, i, *_: (b, 0, i)),
        scratch_shapes=[pltpu.VMEM((2, 3 * WINDOW, GA * WINDOW), F32)],
    )
    return pl.pallas_call(
        functools.partial(_swa_kernel, seq=S),
        grid_spec=grid_spec,
        out_shape=jax.ShapeDtypeStruct((B, HA * DA, S), BF16),
        compiler_params=pltpu.CompilerParams(
            dimension_semantics=("parallel", "parallel"), vmem_limit_bytes=VMEM_LIMIT),
        name="swa",
    )(sink, qaT, ka, ka, ka, vaT, vaT, vaT)


def _mla_kernel(qT_ref, k_ref, vT_ref, oT_ref, s_ref):
    nt, _, tw = qT_ref.shape[2:]
    nvb, nv, tv = vT_ref.shape[2:]
    tk, tq = s_ref.shape[1:]
    nkb = nvb * tv // tk
    vper = tk // tv
    sub = tw // tq
    nq = nt * sub

    def scores(slot, tile, half, j):
        q = qT_ref[0, 0, tile, :, half * tq:(half + 1) * tq]
        cmax = None
        for r in range(0, tk, MLA_SCORE_ROWS):
            s = _dot(k_ref[0, j * tk + r:j * tk + r + MLA_SCORE_ROWS, :], q)
            s_ref[slot, r:r + MLA_SCORE_ROWS] = s
            cm = jnp.max(s, axis=0, keepdims=True)
            cmax = cm if cmax is None else jnp.maximum(cmax, cm)
        return cmax

    def softmax_pv(slot, j, cmax, m, acc):
        m_new = jnp.maximum(m, cmax)
        alpha = jnp.exp2(m - m_new)
        p = jnp.exp2(s_ref[slot] - m_new).astype(BF16)
        pv = _dot(vT_ref[0, 0, j * vper], p[:tv])
        for c in range(1, vper):
            pv += _dot(vT_ref[0, 0, j * vper + c], p[c * tv:(c + 1) * tv])
        return m_new, alpha * acc + pv

    unroll = min(nq, MLA_Q_TILES_PER_TRIP)
    assert nq % unroll == 0 and unroll % sub == 0
    tiles_per_trip = unroll // sub

    def q_tiles(t, cmax):
        for u in range(unroll):
            tile, half = t * tiles_per_trip + u // sub, u % sub
            m, acc = jnp.full((1, tq), -jnp.inf, F32), jnp.zeros((nv, tq), F32)
            for j in range(nkb):
                if j + 1 < nkb:
                    nxt = scores((j + 1) % 2, tile, half, j + 1)
                elif u + 1 < unroll:
                    nxt = scores(0, t * tiles_per_trip + (u + 1) // sub, (u + 1) % sub, 0)
                else:
                    nxt = scores(0, jnp.minimum((t + 1) * tiles_per_trip, nt - 1), 0, 0)
                m, acc = softmax_pv(j % 2, j, cmax, m, acc)
                cmax = nxt
            oT_ref[0, 0, tile, :, half * tq:(half + 1) * tq] = (acc[:DV] / acc[DV:DV + 1]).astype(BF16)
        return cmax

    lax.fori_loop(0, nq // unroll, q_tiles, scores(0, 0, 0, 0))


def _mla(qT, k, vT):
    B, _, nq, _, tq = qT.shape
    nvb, nv, tv = vT.shape[2:]
    S = nq * tq
    tk = MLA_KEY_BLOCK
    assert tk % tv == 0 and S % (2 * tk) == 0
    return pl.pallas_call(
        _mla_kernel,
        grid=(B, HB),
        in_specs=[
            pl.BlockSpec((1, 1, nq, LANE, tq), lambda b, h: (b, h, 0, 0, 0)),
            pl.BlockSpec((1, S, LANE), lambda b, h: (b, 0, h)),
            pl.BlockSpec((1, 1, nvb, nv, tv), lambda b, h: (b, h, 0, 0, 0)),
        ],
        out_specs=pl.BlockSpec((1, 1, nq, DV, tq), lambda b, h: (b, h, 0, 0, 0)),
        out_shape=jax.ShapeDtypeStruct((B, HB, nq, DV, tq), BF16),
        scratch_shapes=[pltpu.VMEM((2, tk, MLA_Q_TILE), F32)],
        compiler_params=pltpu.CompilerParams(
            dimension_semantics=("parallel", "parallel"), vmem_limit_bytes=VMEM_LIMIT),
        name="mla",
    )(qT, k, vT)


def _ffn_kernel(x_ref, oaT_ref, obT_ref, woa_ref, wob_ref, gffn_ref, wg_ref, wu_ref, wd_ref, gfin_ref, y_ref):
    tm = x_ref.shape[1]
    half = tm // 2
    rows = (slice(0, half), slice(half, tm))

    def norm(x, g_ref):
        return x * lax.rsqrt(jnp.mean(x * x, axis=-1, keepdims=True) + EPS) * g_ref[...]

    def attn_out(r):
        obT = obT_ref[0, :, 0, :, r].reshape(HB * DV, half)
        return x_ref[0, r] + _dot_tn(oaT_ref[0, :, r], woa_ref[...]) + _dot_tn(obT, wob_ref[...])

    def gate_up(x):
        h = norm(x, gffn_ref).astype(BF16)
        return _dot(h, wg_ref[...]), _dot(h, wu_ref[...])

    def act(g, u):
        return (g / (1.0 + jnp.exp(-g)) * u).astype(BF16)

    xa = attn_out(rows[0])
    xb = attn_out(rows[1])
    ga, ua = gate_up(xa)
    gb, ub = gate_up(xb)
    xa = xa + _dot(act(ga, ua), wd_ref[...])
    xb = xb + _dot(act(gb, ub), wd_ref[...])
    y_ref[0, rows[0]] = norm(xa, gfin_ref)
    y_ref[0, rows[1]] = norm(xb, gfin_ref)


def _ffn(x, oaT, obT, p):
    B, S, _ = x.shape
    tm = TOK_TILE
    full = lambda a: pl.BlockSpec(a.shape, lambda b, i: (0,) * a.ndim, pipeline_mode=pl.Buffered(1))
    weights = (p["w_oa"], p["w_ob"], p["g_ffn"], p["w_gate"], p["w_up"], p["w_down"], p["g_final"])
    return pl.pallas_call(
        _ffn_kernel,
        grid=(B, S // tm),
        in_specs=[
            pl.BlockSpec((1, tm, D_MODEL), lambda b, i: (b, i, 0)),
            pl.BlockSpec((1, HA * DA, tm), lambda b, i: (b, 0, i)),
            pl.BlockSpec((1, HB, 1, DV, tm), lambda b, i: (b, 0, i, 0, 0)),
        ] + [full(w) for w in weights],
        out_specs=pl.BlockSpec((1, tm, D_MODEL), lambda b, i: (b, i, 0)),
        out_shape=jax.ShapeDtypeStruct(x.shape, F32),
        compiler_params=pltpu.CompilerParams(
            dimension_semantics=("parallel", "parallel"), vmem_limit_bytes=VMEM_LIMIT),
        name="ffn",
    )(x, oaT, obT, *weights)


def _rope_tables_t(seq, dim):
    inv = 1.0 / (ROPE_THETA ** (jnp.arange(0, dim, 2, dtype=F32) / dim))
    ang = jnp.arange(seq, dtype=F32)[:, None] * inv[None, :]
    return jnp.cos(ang).T, jnp.sin(ang).T


def _prep_params(g_mix, w_in, sink, cq_g, w_uq, ckv_g, w_ukv, w_o, g_ffn, w_gate, w_up, w_down, g_final):
    w_uq3 = w_uq.reshape(Q_RANK, HB, DN + DR)
    w_uq_pad = jnp.pad(w_uq3, ((0, 0), (0, 0), (0, LANE - DN - DR))).reshape(Q_RANK, HB * LANE)
    w_ukv3 = w_ukv.reshape(KV_RANK, HB, DN + DV)
    w_uk, w_uv = w_ukv3[..., :DN], w_ukv3[..., DN:]
    top = jnp.pad(w_uk, ((0, 0), (0, 0), (0, LANE - DN)))
    eye = jnp.pad(jnp.eye(DR, dtype=F32)[:, None, :], ((0, 0), (0, 0), (DN, LANE - DN - DR)))
    bot = jnp.broadcast_to(eye, (DR, HB, LANE))
    w_k = jnp.concatenate([top, bot], axis=0).reshape(KV_RANK + DR, HB * LANE)
    return {
        "g_mix": g_mix.reshape(1, D_MODEL),
        "w_inT": w_in.T.astype(BF16),
        "cq_g": cq_g.reshape(Q_RANK, 1),
        "w_uqT": w_uq_pad.T.astype(BF16),
        "ckv_g": ckv_g.reshape(KV_RANK, 1),
        "w_k": w_k.astype(BF16),
        "w_vT": w_uv.reshape(KV_RANK, HB * DV).T.astype(BF16),
        "sink": sink,
        "w_oa": w_o[:HA * DA].astype(BF16),
        "w_ob": w_o[HA * DA:].astype(BF16),
        "g_ffn": g_ffn.reshape(1, D_MODEL),
        "w_gate": w_gate.astype(BF16),
        "w_up": w_up.astype(BF16),
        "w_down": w_down.astype(BF16),
        "g_final": g_final.reshape(1, D_MODEL),
    }


def _trunk(x, p):
    S = x.shape[1]
    tabs = _rope_tables_t(S, DA) + _rope_tables_t(S, DR)
    qaT, ka, vaT, qT, k, vT = _proj(x, p, tabs)
    oaT = _swa(p["sink"], qaT, ka, vaT)
    obT = _mla(qT, k, vT)
    return _ffn(x, oaT, obT, p)


def kernel(x_prompt, x_sample, g_mix, w_in, sink, cq_g, w_uq, ckv_g, w_ukv, w_o, g_ffn,
           w_gate, w_up, w_down, g_final):
    p = _prep_params(g_mix[0], w_in[0], sink[0], cq_g[0], w_uq[0], ckv_g[0], w_ukv[0], w_o[0],
                     g_ffn[0], w_gate[0], w_up[0], w_down[0], g_final)
    return _trunk(x_prompt, p), _trunk(x_sample, p)
```

```python
import functools
import math

import jax
import jax.numpy as jnp
from jax import lax
from jax.experimental import pallas as pl
from jax.experimental.pallas import tpu as pltpu

D_MODEL = 1024
EPS = 1e-6
ROPE_THETA = 10000.0
WINDOW = 128
HA, KVA, DA = 8, 2, 64
GA = HA // KVA
HB, Q_RANK, KV_RANK, DN, DR, DV = 8, 384, 256, 64, 32, 64
D_IN = HA * DA + 2 * KVA * DA + Q_RANK + KV_RANK + DR
D_FF = int(math.ceil(8 * D_MODEL / 3 / 256) * 256)
NEG = -1e30

LANE = 128
V_ROWS = 128
LOG2E = math.log2(math.e)
SWA_Q_SCALE = DA ** -0.5 * LOG2E
MLA_Q_SCALE = (DN + DR) ** -0.5 * LOG2E
OFF_KA = HA * DA
OFF_VA = OFF_KA + KVA * DA
OFF_CQ = OFF_VA + KVA * DA
OFF_CKV = OFF_CQ + Q_RANK
OFF_KR = OFF_CKV + KV_RANK

TOK_TILE = 512
SWA_TILE = 1024
MLA_Q_TILE = 256
MLA_KEY_BLOCK = 2048
MLA_SCORE_ROWS = 1024
MLA_Q_TILES_PER_TRIP = 8
VMEM_LIMIT = 56 * 1024 * 1024

F32 = jnp.float32
BF16 = jnp.bfloat16


def _dot(a, b):
    return jnp.dot(a, b, preferred_element_type=F32)


def _dot_tn(a, b):
    return lax.dot_general(a, b, (((0,), (0,)), ((), ())), preferred_element_type=F32)


def _dot_nt(a, b):
    return lax.dot_general(a, b, (((1,), (1,)), ((), ())), preferred_element_type=F32)


def _rope_rows(x1, x2, c, s):
    return x1 * c - x2 * s, x2 * c + x1 * s


def _proj_kernel(x_ref, gmix_ref, winT_ref, cqg_ref, wuqT_ref, ckvg_ref, wk_ref, wvT_ref,
                 cosa_ref, sina_ref, cosr_ref, sinr_ref,
                 qaT_ref, ka_ref, vaT_ref, qT_ref, k_ref, vT_ref):
    x = x_ref[0]
    ms = jnp.mean(x * x, axis=-1, keepdims=True)
    h = (x * lax.rsqrt(ms + EPS) * gmix_ref[...]).astype(BF16)
    z_lat = _dot_nt(winT_ref[OFF_CQ:], h)
    zT = jnp.concatenate([_dot_nt(winT_ref[:OFF_CQ], h), z_lat], axis=0)

    ca, sa = cosa_ref[...], sina_ref[...]
    cr, sr = cosr_ref[...], sinr_ref[...]
    half = DA // 2

    qaT_ref[0] = jnp.zeros(qaT_ref.shape[1:], BF16)
    for hq in range(HA):
        r = hq * DA
        o1, o2 = _rope_rows(zT[r:r + half], zT[r + half:r + DA], ca, sa)
        base = hq * LANE + (hq // GA) * DA
        qaT_ref[0, base:base + half, :] = (o1 * SWA_Q_SCALE).astype(BF16)
        qaT_ref[0, base + half:base + DA, :] = (o2 * SWA_Q_SCALE).astype(BF16)

    kaT = []
    for g in range(KVA):
        r = OFF_KA + g * DA
        kaT.extend(_rope_rows(zT[r:r + half], zT[r + half:r + DA], ca, sa))
    ka_ref[0] = jnp.concatenate(kaT, axis=0).T.astype(BF16)
    tm = zT.shape[1]
    ones_row = lax.broadcasted_iota(jnp.int32, (V_ROWS - DV, tm), 0) == 0
    for g in range(KVA):
        r = OFF_VA + g * DA
        vaT_ref[0, g * V_ROWS:g * V_ROWS + DA, :] = zT[r:r + DA].astype(BF16)
        vaT_ref[0, g * V_ROWS + DA:(g + 1) * V_ROWS, :] = ones_row.astype(BF16)

    cq = zT[OFF_CQ:OFF_CKV]
    cqn = (cq * lax.rsqrt(jnp.mean(cq * cq, axis=0, keepdims=True) + EPS) * cqg_ref[...]).astype(BF16)
    qT = _dot(wuqT_ref[...], cqn) * MLA_Q_SCALE
    qT_ref[0, :, 0] = qT.astype(BF16).reshape(HB, LANE, tm)
    hr = DR // 2
    for hh in range(HB):
        r = hh * LANE + DN
        o1, o2 = _rope_rows(qT[r:r + hr], qT[r + hr:r + DR], cr, sr)
        qT_ref[0, hh, 0, DN:DN + hr, :] = o1.astype(BF16)
        qT_ref[0, hh, 0, DN + hr:DN + DR, :] = o2.astype(BF16)

    ckv = zT[OFF_CKV:OFF_KR]
    ckvn = (ckv * lax.rsqrt(jnp.mean(ckv * ckv, axis=0, keepdims=True) + EPS) * ckvg_ref[...]).astype(BF16)
    krT = zT[OFF_KR:D_IN]
    k1, k2 = _rope_rows(krT[:hr], krT[hr:], cr, sr)
    kin = jnp.concatenate([ckvn, k1.astype(BF16), k2.astype(BF16)], axis=0)
    k_ref[0] = _dot_tn(kin, wk_ref[...]).astype(BF16)
    vT_ref[0, :, 0, :DV] = _dot(wvT_ref[...], ckvn).astype(BF16).reshape(HB, DV, tm)
    vT_ref[0, :, 0, DV:] = jnp.broadcast_to(ones_row.astype(BF16), (HB, V_ROWS - DV, tm))


def _proj(x, p, tabs):
    B, S, _ = x.shape
    tm = TOK_TILE
    nt = S // tm
    full = lambda a: pl.BlockSpec(a.shape, lambda b, i: (0,) * a.ndim)
    tab = lambda a: pl.BlockSpec((a.shape[0], tm), lambda b, i: (0, i))
    weights = (p["g_mix"], p["w_inT"], p["cq_g"], p["w_uqT"], p["ckv_g"], p["w_k"], p["w_vT"])
    out_shape = (
        jax.ShapeDtypeStruct((B, HA * LANE, S), BF16),
        jax.ShapeDtypeStruct((B, S, KVA * DA), BF16),
        jax.ShapeDtypeStruct((B, KVA * V_ROWS, S), BF16),
        jax.ShapeDtypeStruct((B, HB, nt, LANE, tm), BF16),
        jax.ShapeDtypeStruct((B, S, HB * LANE), BF16),
        jax.ShapeDtypeStruct((B, HB, nt, V_ROWS, tm), BF16),
    )
    out_specs = (
        pl.BlockSpec((1, HA * LANE, tm), lambda b, i: (b, 0, i)),
        pl.BlockSpec((1, tm, KVA * DA), lambda b, i: (b, i, 0)),
        pl.BlockSpec((1, KVA * V_ROWS, tm), lambda b, i: (b, 0, i)),
        pl.BlockSpec((1, HB, 1, LANE, tm), lambda b, i: (b, 0, i, 0, 0)),
        pl.BlockSpec((1, tm, HB * LANE), lambda b, i: (b, i, 0)),
        pl.BlockSpec((1, HB, 1, V_ROWS, tm), lambda b, i: (b, 0, i, 0, 0)),
    )
    return pl.pallas_call(
        _proj_kernel,
        grid=(B, nt),
        in_specs=[pl.BlockSpec((1, tm, D_MODEL), lambda b, i: (b, i, 0))]
        + [full(w) for w in weights] + [tab(t) for t in tabs],
        out_specs=out_specs,
        out_shape=out_shape,
        compiler_params=pltpu.CompilerParams(
            dimension_semantics=("parallel", "parallel"), vmem_limit_bytes=VMEM_LIMIT),
        name="proj",
    )(x, *weights, *tabs)


def _swa_kernel(sink_ref, qaT_ref, kp_ref, kc_ref, kn_ref, vp_ref, vc_ref, vn_ref, oT_ref, s_ref, *, seq):
    tq = qaT_ref.shape[2]
    q0 = pl.program_id(1) * tq
    kcat = jnp.concatenate([kp_ref[0], kc_ref[0], kn_ref[0]], axis=0)
    vcat = jnp.concatenate([vp_ref[0], vc_ref[0], vn_ref[0]], axis=1)
    nk = 3 * WINDOW
    wide = GA * WINDOW
    r = lax.broadcasted_iota(jnp.int32, (nk, wide), 0)
    lane = lax.broadcasted_iota(jnp.int32, (nk, wide), 1)
    d = r - (lane & (WINDOW - 1))
    band = (d >= 0) & (d <= 2 * WINDOW)
    head = lax.broadcasted_iota(jnp.int32, (1, wide), 1) // WINDOW
    units = [(c, g) for c in range(tq // WINDOW) for g in range(KVA)]

    def scores(slot, c, g):
        q4 = jnp.concatenate(
            [qaT_ref[0, (g * GA + h) * LANE:(g * GA + h + 1) * LANE, c * WINDOW:(c + 1) * WINDOW]
             for h in range(GA)], axis=1)
        valid = band
        if c in (0, tq // WINDOW - 1):
            kpos = q0 + (c - 1) * WINDOW + r
            valid = band & (kpos >= 0) & (kpos < seq)
        s = jnp.where(valid, _dot(kcat[c * WINDOW:c * WINDOW + nk], q4), NEG)
        s_ref[slot] = s
        return jnp.max(s, axis=0, keepdims=True)

    def finish(slot, c, g, cmax):
        sink = jnp.zeros((1, wide), F32)
        for h in range(GA):
            sink = jnp.where(head == h, sink_ref[g * GA + h] * LOG2E, sink)
        m = jnp.maximum(cmax, sink)
        p = jnp.exp2(s_ref[slot] - m).astype(BF16)
        pv = _dot(vcat[g * V_ROWS:(g + 1) * V_ROWS, c * WINDOW:c * WINDOW + nk], p)
        o = (pv[:DA] / (pv[DA:DA + 1] + jnp.exp2(sink - m))).astype(BF16)
        for h in range(GA):
            hq = g * GA + h
            oT_ref[0, hq * DA:(hq + 1) * DA, c * WINDOW:(c + 1) * WINDOW] = o[:, h * WINDOW:(h + 1) * WINDOW]

    cmax = scores(0, *units[0])
    for u, (c, g) in enumerate(units):
        nxt = scores((u + 1) % 2, *units[u + 1]) if u + 1 < len(units) else None
        finish(u % 2, c, g, cmax)
        cmax = nxt


def _swa(sink, qaT, ka, vaT):
    B, _, S = qaT.shape
    tq = SWA_TILE
    r = tq // WINDOW
    nb = S // WINDOW
    prev = lambda b, i, *_: (b, jnp.maximum(i * r - 1, 0), 0)
    nxt = lambda b, i, *_: (b, jnp.minimum(i * r + r, nb - 1), 0)
    prev_t = lambda b, i, *_: (b, 0, jnp.maximum(i * r - 1, 0))
    nxt_t = lambda b, i, *_: (b, 0, jnp.minimum(i * r + r, nb - 1))
    grid_spec = pltpu.PrefetchScalarGridSpec(
        num_scalar_prefetch=1,
        grid=(B, S // tq),
        in_specs=[
            pl.BlockSpec((1, HA * LANE, tq), lambda b, i, *_: (b, 0, i)),
            pl.BlockSpec((1, WINDOW, KVA * DA), prev),
            pl.BlockSpec((1, tq, KVA * DA), lambda b, i, *_: (b, i, 0)),
            pl.BlockSpec((1, WINDOW, KVA * DA), nxt),
            pl.BlockSpec((1, KVA * V_ROWS, WINDOW), prev_t),
            pl.BlockSpec((1, KVA * V_ROWS, tq), lambda b, i, *_: (b, 0, i)),
            pl.BlockSpec((1, KVA * V_ROWS, WINDOW), nxt_t),
        ],
        out_specs=pl.BlockSpec((1, HA * DA, tq), lambda b, i, *_: (b, 0, i)),
        scratch_shapes=[pltpu.VMEM((2, 3 * WINDOW, GA * WINDOW), F32)],
    )
    return pl.pallas_call(
        functools.partial(_swa_kernel, seq=S),
        grid_spec=grid_spec,
        out_shape=jax.ShapeDtypeStruct((B, HA * DA, S), BF16),
        compiler_params=pltpu.CompilerParams(
            dimension_semantics=("parallel", "parallel"), vmem_limit_bytes=VMEM_LIMIT),
        name="swa",
    )(sink, qaT, ka, ka, ka, vaT, vaT, vaT)


def _mla_kernel(qT_ref, k_ref, vT_ref, oT_ref, s_ref):
    nt, _, tw = qT_ref.shape[2:]
    nvb, nv, tv = vT_ref.shape[2:]
    tk, tq = s_ref.shape[1:]
    nkb = nvb * tv // tk
    vper = tk // tv
    sub = tw // tq
    nq = nt * sub

    def scores(slot, tile, half, j):
        q = qT_ref[0, 0, tile, :, half * tq:(half + 1) * tq]
        cmax = None
        for r in range(0, tk, MLA_SCORE_ROWS):
            s = _dot(k_ref[0, j * tk + r:j * tk + r + MLA_SCORE_ROWS, :], q)
            s_ref[slot, r:r + MLA_SCORE_ROWS] = s
            cm = jnp.max(s, axis=0, keepdims=True)
            cmax = cm if cmax is None else jnp.maximum(cmax, cm)
        return cmax

    def softmax_pv(slot, j, cmax, m, acc):
        m_new = jnp.maximum(m, cmax)
        alpha = jnp.exp2(m - m_new)
        p = jnp.exp2(s_ref[slot] - m_new).astype(BF16)
        pv = _dot(vT_ref[0, 0, j * vper], p[:tv])
        for c in range(1, vper):
            pv += _dot(vT_ref[0, 0, j * vper + c], p[c * tv:(c + 1) * tv])
        return m_new, alpha * acc + pv

    unroll = min(nq, MLA_Q_TILES_PER_TRIP)
    assert nq % unroll == 0 and unroll % sub == 0
    tiles_per_trip = unroll // sub

    def q_tiles(t, cmax):
        for u in range(unroll):
            tile, half = t * tiles_per_trip + u // sub, u % sub
            m, acc = jnp.full((1, tq), -jnp.inf, F32), jnp.zeros((nv, tq), F32)
            for j in range(nkb):
                if j + 1 < nkb:
                    nxt = scores((j + 1) % 2, tile, half, j + 1)
                elif u + 1 < unroll:
                    nxt = scores(0, t * tiles_per_trip + (u + 1) // sub, (u + 1) % sub, 0)
                else:
                    nxt = scores(0, jnp.minimum((t + 1) * tiles_per_trip, nt - 1), 0, 0)
                m, acc = softmax_pv(j % 2, j, cmax, m, acc)
                cmax = nxt
            oT_ref[0, 0, tile, :, half * tq:(half + 1) * tq] = (acc[:DV] / acc[DV:DV + 1]).astype(BF16)
        return cmax

    lax.fori_loop(0, nq // unroll, q_tiles, scores(0, 0, 0, 0))


def _mla(qT, k, vT):
    B, _, nq, _, tq = qT.shape
    nvb, nv, tv = vT.shape[2:]
    S = nq * tq
    tk = MLA_KEY_BLOCK
    assert tk % tv == 0 and S % (2 * tk) == 0
    return pl.pallas_call(
        _mla_kernel,
        grid=(B, HB),
        in_specs=[
            pl.BlockSpec((1, 1, nq, LANE, tq), lambda b, h: (b, h, 0, 0, 0)),
            pl.BlockSpec((1, S, LANE), lambda b, h: (b, 0, h)),
            pl.BlockSpec((1, 1, nvb, nv, tv), lambda b, h: (b, h, 0, 0, 0)),
        ],
        out_specs=pl.BlockSpec((1, 1, nq, DV, tq), lambda b, h: (b, h, 0, 0, 0)),
        out_shape=jax.ShapeDtypeStruct((B, HB, nq, DV, tq), BF16),
        scratch_shapes=[pltpu.VMEM((2, tk, MLA_Q_TILE), F32)],
        compiler_params=pltpu.CompilerParams(
            dimension_semantics=("parallel", "parallel"), vmem_limit_bytes=VMEM_LIMIT),
        name="mla",
    )(qT, k, vT)


def _ffn_kernel(x_ref, oaT_ref, obT_ref, woa_ref, wob_ref, gffn_ref, wg_ref, wu_ref, wd_ref, gfin_ref, y_ref):
    tm = x_ref.shape[1]
    half = tm // 2
    rows = (slice(0, half), slice(half, tm))

    def norm(x, g_ref):
        return x * lax.rsqrt(jnp.mean(x * x, axis=-1, keepdims=True) + EPS) * g_ref[...]

    def attn_out(r):
        obT = obT_ref[0, :, 0, :, r].reshape(HB * DV, half)
        return x_ref[0, r] + _dot_tn(oaT_ref[0, :, r], woa_ref[...]) + _dot_tn(obT, wob_ref[...])

    def gate_up(x):
        h = norm(x, gffn_ref).astype(BF16)
        return _dot(h, wg_ref[...]), _dot(h, wu_ref[...])

    def act(g, u):
        return (g / (1.0 + jnp.exp(-g)) * u).astype(BF16)

    xa = attn_out(rows[0])
    xb = attn_out(rows[1])
    ga, ua = gate_up(xa)
    gb, ub = gate_up(xb)
    xa = xa + _dot(act(ga, ua), wd_ref[...])
    xb = xb + _dot(act(gb, ub), wd_ref[...])
    y_ref[0, rows[0]] = norm(xa, gfin_ref)
    y_ref[0, rows[1]] = norm(xb, gfin_ref)


def _ffn(x, oaT, obT, p):
    B, S, _ = x.shape
    tm = TOK_TILE
    full = lambda a: pl.BlockSpec(a.shape, lambda b, i: (0,) * a.ndim, pipeline_mode=pl.Buffered(1))
    weights = (p["w_oa"], p["w_ob"], p["g_ffn"], p["w_gate"], p["w_up"], p["w_down"], p["g_final"])
    return pl.pallas_call(
        _ffn_kernel,
        grid=(B, S // tm),
        in_specs=[
            pl.BlockSpec((1, tm, D_MODEL), lambda b, i: (b, i, 0)),
            pl.BlockSpec((1, HA * DA, tm), lambda b, i: (b, 0, i)),
            pl.BlockSpec((1, HB, 1, DV, tm), lambda b, i: (b, 0, i, 0, 0)),
        ] + [full(w) for w in weights],
        out_specs=pl.BlockSpec((1, tm, D_MODEL), lambda b, i: (b, i, 0)),
        out_shape=jax.ShapeDtypeStruct(x.shape, F32),
        compiler_params=pltpu.CompilerParams(
            dimension_semantics=("parallel", "parallel"), vmem_limit_bytes=VMEM_LIMIT),
        name="ffn",
    )(x, oaT, obT, *weights)


def _rope_tables_t(seq, dim):
    inv = 1.0 / (ROPE_THETA ** (jnp.arange(0, dim, 2, dtype=F32) / dim))
    ang = jnp.arange(seq, dtype=F32)[:, None] * inv[None, :]
    return jnp.cos(ang).T, jnp.sin(ang).T


def _prep_params(g_mix, w_in, sink, cq_g, w_uq, ckv_g, w_ukv, w_o, g_ffn, w_gate, w_up, w_down, g_final):
    w_uq3 = w_uq.reshape(Q_RANK, HB, DN + DR)
    w_uq_pad = jnp.pad(w_uq3, ((0, 0), (0, 0), (0, LANE - DN - DR))).reshape(Q_RANK, HB * LANE)
    w_ukv3 = w_ukv.reshape(KV_RANK, HB, DN + DV)
    w_uk, w_uv = w_ukv3[..., :DN], w_ukv3[..., DN:]
    top = jnp.pad(w_uk, ((0, 0), (0, 0), (0, LANE - DN)))
    eye = jnp.pad(jnp.eye(DR, dtype=F32)[:, None, :], ((0, 0), (0, 0), (DN, LANE - DN - DR)))
    bot = jnp.broadcast_to(eye, (DR, HB, LANE))
    w_k = jnp.concatenate([top, bot], axis=0).reshape(KV_RANK + DR, HB * LANE)
    return {
        "g_mix": g_mix.reshape(1, D_MODEL),
        "w_inT": w_in.T.astype(BF16),
        "cq_g": cq_g.reshape(Q_RANK, 1),
        "w_uqT": w_uq_pad.T.astype(BF16),
        "ckv_g": ckv_g.reshape(KV_RANK, 1),
        "w_k": w_k.astype(BF16),
        "w_vT": w_uv.reshape(KV_RANK, HB * DV).T.astype(BF16),
        "sink": sink,
        "w_oa": w_o[:HA * DA].astype(BF16),
        "w_ob": w_o[HA * DA:].astype(BF16),
        "g_ffn": g_ffn.reshape(1, D_MODEL),
        "w_gate": w_gate.astype(BF16),
        "w_up": w_up.astype(BF16),
        "w_down": w_down.astype(BF16),
        "g_final": g_final.reshape(1, D_MODEL),
    }


def _trunk(x, p):
    S = x.shape[1]
    tabs = _rope_tables_t(S, DA) + _rope_tables_t(S, DR)
    qaT, ka, vaT, qT, k, vT = _proj(x, p, tabs)
    oaT = _swa(p["sink"], qaT, ka, vaT)
    obT = _mla(qT, k, vT)
    return _ffn(x, oaT, obT, p)


def kernel(x_prompt, x_sample, g_mix, w_in, sink, cq_g, w_uq, ckv_g, w_ukv, w_o, g_ffn,
           w_gate, w_up, w_down, g_final):
    p = _prep_params(g_mix[0], w_in[0], sink[0], cq_g[0], w_uq[0], ckv_g[0], w_ukv[0], w_o[0],
                     g_ffn[0], w_gate[0], w_up[0], w_down[0], g_final)
    return _trunk(x_prompt, p), _trunk(x_sample, p)
```

```python
import functools
import math

import jax
import jax.numpy as jnp
from jax import lax
from jax.experimental import pallas as pl
from jax.experimental.pallas import tpu as pltpu

D_MODEL = 1024
EPS = 1e-6
ROPE_THETA = 10000.0
WINDOW = 128
HA, KVA, DA = 8, 2, 64
GA = HA // KVA
HB, Q_RANK, KV_RANK, DN, DR, DV = 8, 384, 256, 64, 32, 64
D_IN = HA * DA + 2 * KVA * DA + Q_RANK + KV_RANK + DR
D_FF = int(math.ceil(8 * D_MODEL / 3 / 256) * 256)
NEG = -1e30

LANE = 128
V_ROWS = 128
LOG2E = math.log2(math.e)
SWA_Q_SCALE = DA ** -0.5 * LOG2E
MLA_Q_SCALE = (DN + DR) ** -0.5 * LOG2E
OFF_KA = HA * DA
OFF_VA = OFF_KA + KVA * DA
OFF_CQ = OFF_VA + KVA * DA
OFF_CKV = OFF_CQ + Q_RANK
OFF_KR = OFF_CKV + KV_RANK

TOK_TILE = 512
SWA_TILE = 1024
MLA_Q_TILE = 256
MLA_KEY_BLOCK = 2048
MLA_SCORE_ROWS = 1024
MLA_Q_TILES_PER_TRIP = 8
VMEM_LIMIT = 56 * 1024 * 1024

F32 = jnp.float32
BF16 = jnp.bfloat16


def _dot(a, b):
    return jnp.dot(a, b, preferred_element_type=F32)


def _dot_tn(a, b):
    return lax.dot_general(a, b, (((0,), (0,)), ((), ())), preferred_element_type=F32)


def _dot_nt(a, b):
    return lax.dot_general(a, b, (((1,), (1,)), ((), ())), preferred_element_type=F32)


def _rope_rows(x1, x2, c, s):
    return x1 * c - x2 * s, x2 * c + x1 * s


def _proj_kernel(x_ref, gmix_ref, winT_ref, cqg_ref, wuqT_ref, ckvg_ref, wk_ref, wvT_ref,
                 cosa_ref, sina_ref, cosr_ref, sinr_ref,
                 qaT_ref, ka_ref, vaT_ref, qT_ref, k_ref, vT_ref):
    tm = x_ref.shape[1]
    parts = []
    for r in (slice(0, tm // 2), slice(tm // 2, tm)):
        x = x_ref[0, r]
        ms = jnp.mean(x * x, axis=-1, keepdims=True)
        h = (x * lax.rsqrt(ms + EPS) * gmix_ref[...]).astype(BF16)
        z_lat = _dot_nt(winT_ref[OFF_CQ:], h)
        parts.append(jnp.concatenate([_dot_nt(winT_ref[:OFF_CQ], h), z_lat], axis=0))
    zT = jnp.concatenate(parts, axis=1)

    ca, sa = cosa_ref[...], sina_ref[...]
    cr, sr = cosr_ref[...], sinr_ref[...]
    half = DA // 2

    qaT_ref[0] = jnp.zeros(qaT_ref.shape[1:], BF16)
    for hq in range(HA):
        r = hq * DA
        o1, o2 = _rope_rows(zT[r:r + half], zT[r + half:r + DA], ca, sa)
        base = hq * LANE + (hq // GA) * DA
        qaT_ref[0, base:base + half, :] = (o1 * SWA_Q_SCALE).astype(BF16)
        qaT_ref[0, base + half:base + DA, :] = (o2 * SWA_Q_SCALE).astype(BF16)

    kaT = []
    for g in range(KVA):
        r = OFF_KA + g * DA
        kaT.extend(_rope_rows(zT[r:r + half], zT[r + half:r + DA], ca, sa))
    ka_ref[0] = jnp.concatenate(kaT, axis=0).T.astype(BF16)
    ones_row = lax.broadcasted_iota(jnp.int32, (V_ROWS - DV, tm), 0) == 0
    for g in range(KVA):
        r = OFF_VA + g * DA
        vaT_ref[0, g * V_ROWS:g * V_ROWS + DA, :] = zT[r:r + DA].astype(BF16)
        vaT_ref[0, g * V_ROWS + DA:(g + 1) * V_ROWS, :] = ones_row.astype(BF16)

    cq = zT[OFF_CQ:OFF_CKV]
    cqn = (cq * lax.rsqrt(jnp.mean(cq * cq, axis=0, keepdims=True) + EPS) * cqg_ref[...]).astype(BF16)
    qT = _dot(wuqT_ref[...], cqn) * MLA_Q_SCALE
    qT_ref[0, :, 0] = qT.astype(BF16).reshape(HB, LANE, tm)
    hr = DR // 2
    for hh in range(HB):
        r = hh * LANE + DN
        o1, o2 = _rope_rows(qT[r:r + hr], qT[r + hr:r + DR], cr, sr)
        qT_ref[0, hh, 0, DN:DN + hr, :] = o1.astype(BF16)
        qT_ref[0, hh, 0, DN + hr:DN + DR, :] = o2.astype(BF16)

    ckv = zT[OFF_CKV:OFF_KR]
    ckvn = (ckv * lax.rsqrt(jnp.mean(ckv * ckv, axis=0, keepdims=True) + EPS) * ckvg_ref[...]).astype(BF16)
    krT = zT[OFF_KR:D_IN]
    k1, k2 = _rope_rows(krT[:hr], krT[hr:], cr, sr)
    kin = jnp.concatenate([ckvn, k1.astype(BF16), k2.astype(BF16)], axis=0)
    k_ref[0] = _dot_tn(kin, wk_ref[...]).astype(BF16)
    vT_ref[0, :, 0, :DV] = _dot(wvT_ref[...], ckvn).astype(BF16).reshape(HB, DV, tm)
    vT_ref[0, :, 0, DV:] = jnp.broadcast_to(ones_row.astype(BF16), (HB, V_ROWS - DV, tm))


def _proj(x, p, tabs):
    B, S, _ = x.shape
    tm = TOK_TILE
    nt = S // tm
    full = lambda a: pl.BlockSpec(a.shape, lambda b, i: (0,) * a.ndim)
    tab = lambda a: pl.BlockSpec((a.shape[0], tm), lambda b, i: (0, i))
    weights = (p["g_mix"], p["w_inT"], p["cq_g"], p["w_uqT"], p["ckv_g"], p["w_k"], p["w_vT"])
    out_shape = (
        jax.ShapeDtypeStruct((B, HA * LANE, S), BF16),
        jax.ShapeDtypeStruct((B, S, KVA * DA), BF16),
        jax.ShapeDtypeStruct((B, KVA * V_ROWS, S), BF16),
        jax.ShapeDtypeStruct((B, HB, nt, LANE, tm), BF16),
        jax.ShapeDtypeStruct((B, S, HB * LANE), BF16),
        jax.ShapeDtypeStruct((B, HB, nt, V_ROWS, tm), BF16),
    )
    out_specs = (
        pl.BlockSpec((1, HA * LANE, tm), lambda b, i: (b, 0, i)),
        pl.BlockSpec((1, tm, KVA * DA), lambda b, i: (b, i, 0)),
        pl.BlockSpec((1, KVA * V_ROWS, tm), lambda b, i: (b, 0, i)),
        pl.BlockSpec((1, HB, 1, LANE, tm), lambda b, i: (b, 0, i, 0, 0)),
        pl.BlockSpec((1, tm, HB * LANE), lambda b, i: (b, i, 0)),
        pl.BlockSpec((1, HB, 1, V_ROWS, tm), lambda b, i: (b, 0, i, 0, 0)),
    )
    return pl.pallas_call(
        _proj_kernel,
        grid=(B, nt),
        in_specs=[pl.BlockSpec((1, tm, D_MODEL), lambda b, i: (b, i, 0))]
        + [full(w) for w in weights] + [tab(t) for t in tabs],
        out_specs=out_specs,
        out_shape=out_shape,
        compiler_params=pltpu.CompilerParams(
            dimension_semantics=("parallel", "parallel"), vmem_limit_bytes=VMEM_LIMIT),
        name="proj",
    )(x, *weights, *tabs)


def _swa_kernel(sink_ref, qaT_ref, kp_ref, kc_ref, kn_ref, vp_ref, vc_ref, vn_ref, oT_ref, s_ref, *, seq):
    tq = qaT_ref.shape[2]
    q0 = pl.program_id(1) * tq
    kcat = jnp.concatenate([kp_ref[0], kc_ref[0], kn_ref[0]], axis=0)
    vcat = jnp.concatenate([vp_ref[0], vc_ref[0], vn_ref[0]], axis=1)
    nk = 3 * WINDOW
    wide = GA * WINDOW
    r = lax.broadcasted_iota(jnp.int32, (nk, wide), 0)
    lane = lax.broadcasted_iota(jnp.int32, (nk, wide), 1)
    d = r - (lane & (WINDOW - 1))
    band = (d >= 0) & (d <= 2 * WINDOW)
    head = lax.broadcasted_iota(jnp.int32, (1, wide), 1) // WINDOW
    units = [(c, g) for c in range(tq // WINDOW) for g in range(KVA)]

    def scores(slot, c, g):
        q4 = jnp.concatenate(
            [qaT_ref[0, (g * GA + h) * LANE:(g * GA + h + 1) * LANE, c * WINDOW:(c + 1) * WINDOW]
             for h in range(GA)], axis=1)
        valid = band
        if c in (0, tq // WINDOW - 1):
            kpos = q0 + (c - 1) * WINDOW + r
            valid = band & (kpos >= 0) & (kpos < seq)
        s = jnp.where(valid, _dot(kcat[c * WINDOW:c * WINDOW + nk], q4), NEG)
        s_ref[slot] = s
        return jnp.max(s, axis=0, keepdims=True)

    def finish(slot, c, g, cmax):
        sink = jnp.zeros((1, wide), F32)
        for h in range(GA):
            sink = jnp.where(head == h, sink_ref[g * GA + h] * LOG2E, sink)
        m = jnp.maximum(cmax, sink)
        p = jnp.exp2(s_ref[slot] - m).astype(BF16)
        pv = _dot(vcat[g * V_ROWS:(g + 1) * V_ROWS, c * WINDOW:c * WINDOW + nk], p)
        o = (pv[:DA] / (pv[DA:DA + 1] + jnp.exp2(sink - m))).astype(BF16)
        for h in range(GA):
            hq = g * GA + h
            oT_ref[0, hq * DA:(hq + 1) * DA, c * WINDOW:(c + 1) * WINDOW] = o[:, h * WINDOW:(h + 1) * WINDOW]

    cmax = scores(0, *units[0])
    for u, (c, g) in enumerate(units):
        nxt = scores((u + 1) % 2, *units[u + 1]) if u + 1 < len(units) else None
        finish(u % 2, c, g, cmax)
        cmax = nxt


def _swa(sink, qaT, ka, vaT):
    B, _, S = qaT.shape
    tq = SWA_TILE
    r = tq // WINDOW
    nb = S // WINDOW
    prev = lambda b, i, *_: (b, jnp.maximum(i * r - 1, 0), 0)
    nxt = lambda b, i, *_: (b, jnp.minimum(i * r + r, nb - 1), 0)
    prev_t = lambda b, i, *_: (b, 0, jnp.maximum(i * r - 1, 0))
    nxt_t = lambda b, i, *_: (b, 0, jnp.minimum(i * r + r, nb - 1))
    grid_spec = pltpu.PrefetchScalarGridSpec(
        num_scalar_prefetch=1,
        grid=(B, S // tq),
        in_specs=[
            pl.BlockSpec((1, HA * LANE, tq), lambda b, i, *_: (b, 0, i)),
            pl.BlockSpec((1, WINDOW, KVA * DA), prev),
            pl.BlockSpec((1, tq, KVA * DA), lambda b, i, *_: (b, i, 0)),
            pl.BlockSpec((1, WINDOW, KVA * DA), nxt),
            pl.BlockSpec((1, KVA * V_ROWS, WINDOW), prev_t),
            pl.BlockSpec((1, KVA * V_ROWS, tq), lambda b, i, *_: (b, 0, i)),
            pl.BlockSpec((1, KVA * V_ROWS, WINDOW), nxt_t),
        ],
        out_specs=pl.BlockSpec((1, HA * DA, tq), lambda b, i, *_: (b, 0, i)),
        scratch_shapes=[pltpu.VMEM((2, 3 * WINDOW, GA * WINDOW), F32)],
    )
    return pl.pallas_call(
        functools.partial(_swa_kernel, seq=S),
        grid_spec=grid_spec,
        out_shape=jax.ShapeDtypeStruct((B, HA * DA, S), BF16),
        compiler_params=pltpu.CompilerParams(
            dimension_semantics=("parallel", "parallel"), vmem_limit_bytes=VMEM_LIMIT),
        name="swa",
    )(sink, qaT, ka, ka, ka, vaT, vaT, vaT)


def _mla_kernel(qT_ref, k_ref, vT_ref, oT_ref, s_ref):
    nt, _, tw = qT_ref.shape[2:]
    nvb, nv, tv = vT_ref.shape[2:]
    tk, tq = s_ref.shape[1:]
    nkb = nvb * tv // tk
    vper = tk // tv
    sub = tw // tq
    nq = nt * sub

    def scores(slot, tile, half, j):
        q = qT_ref[0, 0, tile, :, half * tq:(half + 1) * tq]
        cmax = None
        for r in range(0, tk, MLA_SCORE_ROWS):
            s = _dot(k_ref[0, j * tk + r:j * tk + r + MLA_SCORE_ROWS, :], q)
            s_ref[slot, r:r + MLA_SCORE_ROWS] = s
            cm = jnp.max(s, axis=0, keepdims=True)
            cmax = cm if cmax is None else jnp.maximum(cmax, cm)
        return cmax

    def softmax_pv(slot, j, cmax, m, acc):
        m_new = jnp.maximum(m, cmax)
        alpha = jnp.exp2(m - m_new)
        p = jnp.exp2(s_ref[slot] - m_new).astype(BF16)
        pv = _dot(vT_ref[0, 0, j * vper], p[:tv])
        for c in range(1, vper):
            pv += _dot(vT_ref[0, 0, j * vper + c], p[c * tv:(c + 1) * tv])
        return m_new, alpha * acc + pv

    unroll = min(nq, MLA_Q_TILES_PER_TRIP)
    assert nq % unroll == 0 and unroll % sub == 0
    tiles_per_trip = unroll // sub

    def q_tiles(t, cmax):
        for u in range(unroll):
            tile, half = t * tiles_per_trip + u // sub, u % sub
            m, acc = jnp.full((1, tq), -jnp.inf, F32), jnp.zeros((nv, tq), F32)
            for j in range(nkb):
                if j + 1 < nkb:
                    nxt = scores((j + 1) % 2, tile, half, j + 1)
                elif u + 1 < unroll:
                    nxt = scores(0, t * tiles_per_trip + (u + 1) // sub, (u + 1) % sub, 0)
                else:
                    nxt = scores(0, jnp.minimum((t + 1) * tiles_per_trip, nt - 1), 0, 0)
                m, acc = softmax_pv(j % 2, j, cmax, m, acc)
                cmax = nxt
            oT_ref[0, 0, tile, :, half * tq:(half + 1) * tq] = (acc[:DV] / acc[DV:DV + 1]).astype(BF16)
        return cmax

    lax.fori_loop(0, nq // unroll, q_tiles, scores(0, 0, 0, 0))


def _mla(qT, k, vT):
    B, _, nq, _, tq = qT.shape
    nvb, nv, tv = vT.shape[2:]
    S = nq * tq
    tk = MLA_KEY_BLOCK
    assert tk % tv == 0 and S % (2 * tk) == 0
    return pl.pallas_call(
        _mla_kernel,
        grid=(B, HB),
        in_specs=[
            pl.BlockSpec((1, 1, nq, LANE, tq), lambda b, h: (b, h, 0, 0, 0)),
            pl.BlockSpec((1, S, LANE), lambda b, h: (b, 0, h)),
            pl.BlockSpec((1, 1, nvb, nv, tv), lambda b, h: (b, h, 0, 0, 0)),
        ],
        out_specs=pl.BlockSpec((1, 1, nq, DV, tq), lambda b, h: (b, h, 0, 0, 0)),
        out_shape=jax.ShapeDtypeStruct((B, HB, nq, DV, tq), BF16),
        scratch_shapes=[pltpu.VMEM((2, tk, MLA_Q_TILE), F32)],
        compiler_params=pltpu.CompilerParams(
            dimension_semantics=("parallel", "parallel"), vmem_limit_bytes=VMEM_LIMIT),
        name="mla",
    )(qT, k, vT)


def _ffn_kernel(x_ref, oaT_ref, obT_ref, woa_ref, wob_ref, gffn_ref, wg_ref, wu_ref, wd_ref, gfin_ref, y_ref):
    tm = x_ref.shape[1]
    half = tm // 2
    rows = (slice(0, half), slice(half, tm))

    def norm(x, g_ref):
        return x * lax.rsqrt(jnp.mean(x * x, axis=-1, keepdims=True) + EPS) * g_ref[...]

    def attn_out(r):
        obT = obT_ref[0, :, 0, :, r].reshape(HB * DV, half)
        return x_ref[0, r] + _dot_tn(oaT_ref[0, :, r], woa_ref[...]) + _dot_tn(obT, wob_ref[...])

    def gate_up(x):
        h = norm(x, gffn_ref).astype(BF16)
        return _dot(h, wg_ref[...]), _dot(h, wu_ref[...])

    def act(g, u):
        return (g / (1.0 + jnp.exp(-g)) * u).astype(BF16)

    xa = attn_out(rows[0])
    xb = attn_out(rows[1])
    ga, ua = gate_up(xa)
    gb, ub = gate_up(xb)
    xa = xa + _dot(act(ga, ua), wd_ref[...])
    xb = xb + _dot(act(gb, ub), wd_ref[...])
    y_ref[0, rows[0]] = norm(xa, gfin_ref)
    y_ref[0, rows[1]] = norm(xb, gfin_ref)


def _ffn(x, oaT, obT, p):
    B, S, _ = x.shape
    tm = TOK_TILE
    full = lambda a: pl.BlockSpec(a.shape, lambda b, i: (0,) * a.ndim, pipeline_mode=pl.Buffered(1))
    weights = (p["w_oa"], p["w_ob"], p["g_ffn"], p["w_gate"], p["w_up"], p["w_down"], p["g_final"])
    return pl.pallas_call(
        _ffn_kernel,
        grid=(B, S // tm),
        in_specs=[
            pl.BlockSpec((1, tm, D_MODEL), lambda b, i: (b, i, 0)),
            pl.BlockSpec((1, HA * DA, tm), lambda b, i: (b, 0, i)),
            pl.BlockSpec((1, HB, 1, DV, tm), lambda b, i: (b, 0, i, 0, 0)),
        ] + [full(w) for w in weights],
        out_specs=pl.BlockSpec((1, tm, D_MODEL), lambda b, i: (b, i, 0)),
        out_shape=jax.ShapeDtypeStruct(x.shape, F32),
        compiler_params=pltpu.CompilerParams(
            dimension_semantics=("parallel", "parallel"), vmem_limit_bytes=VMEM_LIMIT),
        name="ffn",
    )(x, oaT, obT, *weights)


def _rope_tables_t(seq, dim):
    inv = 1.0 / (ROPE_THETA ** (jnp.arange(0, dim, 2, dtype=F32) / dim))
    ang = jnp.arange(seq, dtype=F32)[:, None] * inv[None, :]
    return jnp.cos(ang).T, jnp.sin(ang).T


def _prep_params(g_mix, w_in, sink, cq_g, w_uq, ckv_g, w_ukv, w_o, g_ffn, w_gate, w_up, w_down, g_final):
    w_uq3 = w_uq.reshape(Q_RANK, HB, DN + DR)
    w_uq_pad = jnp.pad(w_uq3, ((0, 0), (0, 0), (0, LANE - DN - DR))).reshape(Q_RANK, HB * LANE)
    w_ukv3 = w_ukv.reshape(KV_RANK, HB, DN + DV)
    w_uk, w_uv = w_ukv3[..., :DN], w_ukv3[..., DN:]
    top = jnp.pad(w_uk, ((0, 0), (0, 0), (0, LANE - DN)))
    eye = jnp.pad(jnp.eye(DR, dtype=F32)[:, None, :], ((0, 0), (0, 0), (DN, LANE - DN - DR)))
    bot = jnp.broadcast_to(eye, (DR, HB, LANE))
    w_k = jnp.concatenate([top, bot], axis=0).reshape(KV_RANK + DR, HB * LANE)
    return {
        "g_mix": g_mix.reshape(1, D_MODEL),
        "w_inT": w_in.T.astype(BF16),
        "cq_g": cq_g.reshape(Q_RANK, 1),
        "w_uqT": w_uq_pad.T.astype(BF16),
        "ckv_g": ckv_g.reshape(KV_RANK, 1),
        "w_k": w_k.astype(BF16),
        "w_vT": w_uv.reshape(KV_RANK, HB * DV).T.astype(BF16),
        "sink": sink,
        "w_oa": w_o[:HA * DA].astype(BF16),
        "w_ob": w_o[HA * DA:].astype(BF16),
        "g_ffn": g_ffn.reshape(1, D_MODEL),
        "w_gate": w_gate.astype(BF16),
        "w_up": w_up.astype(BF16),
        "w_down": w_down.astype(BF16),
        "g_final": g_final.reshape(1, D_MODEL),
    }


def _trunk(x, p):
    S = x.shape[1]
    tabs = _rope_tables_t(S, DA) + _rope_tables_t(S, DR)
    qaT, ka, vaT, qT, k, vT = _proj(x, p, tabs)
    oaT = _swa(p["sink"], qaT, ka, vaT)
    obT = _mla(qT, k, vT)
    return _ffn(x, oaT, obT, p)


def kernel(x_prompt, x_sample, g_mix, w_in, sink, cq_g, w_uq, ckv_g, w_ukv, w_o, g_ffn,
           w_gate, w_up, w_down, g_final):
    p = _prep_params(g_mix[0], w_in[0], sink[0], cq_g[0], w_uq[0], ckv_g[0], w_ukv[0], w_o[0],
                     g_ffn[0], w_gate[0], w_up[0], w_down[0], g_final)
    return _trunk(x_prompt, p), _trunk(x_sample, p)
```

```python
import functools
import math

import jax
import jax.numpy as jnp
from jax import lax
from jax.experimental import pallas as pl
from jax.experimental.pallas import tpu as pltpu

D_MODEL = 1024
EPS = 1e-6
ROPE_THETA = 10000.0
WINDOW = 128
HA, KVA, DA = 8, 2, 64
GA = HA // KVA
HB, Q_RANK, KV_RANK, DN, DR, DV = 8, 384, 256, 64, 32, 64
D_IN = HA * DA + 2 * KVA * DA + Q_RANK + KV_RANK + DR
D_FF = int(math.ceil(8 * D_MODEL / 3 / 256) * 256)
NEG = -1e30

LANE = 128
V_ROWS = 128
LOG2E = math.log2(math.e)
SWA_Q_SCALE = DA ** -0.5 * LOG2E
MLA_Q_SCALE = (DN + DR) ** -0.5 * LOG2E
OFF_KA = HA * DA
OFF_VA = OFF_KA + KVA * DA
OFF_CQ = OFF_VA + KVA * DA
OFF_CKV = OFF_CQ + Q_RANK
OFF_KR = OFF_CKV + KV_RANK

TOK_TILE = 512
SWA_TILE = 2048
MLA_Q_TILE = 256
MLA_KEY_BLOCK = 2048
MLA_SCORE_ROWS = 1024
MLA_Q_TILES_PER_TRIP = 8
VMEM_LIMIT = 56 * 1024 * 1024

F32 = jnp.float32
BF16 = jnp.bfloat16


def _dot(a, b):
    return jnp.dot(a, b, preferred_element_type=F32)


def _dot_tn(a, b):
    return lax.dot_general(a, b, (((0,), (0,)), ((), ())), preferred_element_type=F32)


def _dot_nt(a, b):
    return lax.dot_general(a, b, (((1,), (1,)), ((), ())), preferred_element_type=F32)


def _rope_rows(x1, x2, c, s):
    return x1 * c - x2 * s, x2 * c + x1 * s


def _proj_kernel(x_ref, gmix_ref, winT_ref, cqg_ref, wuqT_ref, ckvg_ref, wk_ref, wvT_ref,
                 cosa_ref, sina_ref, cosr_ref, sinr_ref,
                 qaT_ref, ka_ref, vaT_ref, qT_ref, k_ref, vT_ref):
    tm = x_ref.shape[1]
    parts = []
    for r in (slice(0, tm // 2), slice(tm // 2, tm)):
        x = x_ref[0, r]
        ms = jnp.mean(x * x, axis=-1, keepdims=True)
        h = (x * lax.rsqrt(ms + EPS) * gmix_ref[...]).astype(BF16)
        z_lat = _dot_nt(winT_ref[OFF_CQ:], h)
        parts.append(jnp.concatenate([_dot_nt(winT_ref[:OFF_CQ], h), z_lat], axis=0))
    zT = jnp.concatenate(parts, axis=1)

    ca, sa = cosa_ref[...], sina_ref[...]
    cr, sr = cosr_ref[...], sinr_ref[...]
    half = DA // 2

    qaT_ref[0] = jnp.zeros(qaT_ref.shape[1:], BF16)
    for hq in range(HA):
        r = hq * DA
        o1, o2 = _rope_rows(zT[r:r + half], zT[r + half:r + DA], ca, sa)
        base = hq * LANE + (hq // GA) * DA
        qaT_ref[0, base:base + half, :] = (o1 * SWA_Q_SCALE).astype(BF16)
        qaT_ref[0, base + half:base + DA, :] = (o2 * SWA_Q_SCALE).astype(BF16)

    kaT = []
    for g in range(KVA):
        r = OFF_KA + g * DA
        kaT.extend(_rope_rows(zT[r:r + half], zT[r + half:r + DA], ca, sa))
    ka_ref[0] = jnp.concatenate(kaT, axis=0).T.astype(BF16)
    ones_row = lax.broadcasted_iota(jnp.int32, (V_ROWS - DV, tm), 0) == 0
    for g in range(KVA):
        r = OFF_VA + g * DA
        vaT_ref[0, g * V_ROWS:g * V_ROWS + DA, :] = zT[r:r + DA].astype(BF16)
        vaT_ref[0, g * V_ROWS + DA:(g + 1) * V_ROWS, :] = ones_row.astype(BF16)

    cq = zT[OFF_CQ:OFF_CKV]
    cqn = (cq * lax.rsqrt(jnp.mean(cq * cq, axis=0, keepdims=True) + EPS) * cqg_ref[...]).astype(BF16)
    qT = _dot(wuqT_ref[...], cqn) * MLA_Q_SCALE
    qT_ref[0, :, 0] = qT.astype(BF16).reshape(HB, LANE, tm)
    hr = DR // 2
    for hh in range(HB):
        r = hh * LANE + DN
        o1, o2 = _rope_rows(qT[r:r + hr], qT[r + hr:r + DR], cr, sr)
        qT_ref[0, hh, 0, DN:DN + hr, :] = o1.astype(BF16)
        qT_ref[0, hh, 0, DN + hr:DN + DR, :] = o2.astype(BF16)

    ckv = zT[OFF_CKV:OFF_KR]
    ckvn = (ckv * lax.rsqrt(jnp.mean(ckv * ckv, axis=0, keepdims=True) + EPS) * ckvg_ref[...]).astype(BF16)
    krT = zT[OFF_KR:D_IN]
    k1, k2 = _rope_rows(krT[:hr], krT[hr:], cr, sr)
    kin = jnp.concatenate([ckvn, k1.astype(BF16), k2.astype(BF16)], axis=0)
    k_ref[0] = _dot_tn(kin, wk_ref[...]).astype(BF16)
    vT_ref[0, :, 0, :DV] = _dot(wvT_ref[...], ckvn).astype(BF16).reshape(HB, DV, tm)
    vT_ref[0, :, 0, DV:] = jnp.broadcast_to(ones_row.astype(BF16), (HB, V_ROWS - DV, tm))


def _proj(x, p, tabs):
    B, S, _ = x.shape
    tm = TOK_TILE
    nt = S // tm
    full = lambda a: pl.BlockSpec(a.shape, lambda b, i: (0,) * a.ndim)
    tab = lambda a: pl.BlockSpec((a.shape[0], tm), lambda b, i: (0, i))
    weights = (p["g_mix"], p["w_inT"], p["cq_g"], p["w_uqT"], p["ckv_g"], p["w_k"], p["w_vT"])
    out_shape = (
        jax.ShapeDtypeStruct((B, HA * LANE, S), BF16),
        jax.ShapeDtypeStruct((B, S, KVA * DA), BF16),
        jax.ShapeDtypeStruct((B, KVA * V_ROWS, S), BF16),
        jax.ShapeDtypeStruct((B, HB, nt, LANE, tm), BF16),
        jax.ShapeDtypeStruct((B, S, HB * LANE), BF16),
        jax.ShapeDtypeStruct((B, HB, nt, V_ROWS, tm), BF16),
    )
    out_specs = (
        pl.BlockSpec((1, HA * LANE, tm), lambda b, i: (b, 0, i)),
        pl.BlockSpec((1, tm, KVA * DA), lambda b, i: (b, i, 0)),
        pl.BlockSpec((1, KVA * V_ROWS, tm), lambda b, i: (b, 0, i)),
        pl.BlockSpec((1, HB, 1, LANE, tm), lambda b, i: (b, 0, i, 0, 0)),
        pl.BlockSpec((1, tm, HB * LANE), lambda b, i: (b, i, 0)),
        pl.BlockSpec((1, HB, 1, V_ROWS, tm), lambda b, i: (b, 0, i, 0, 0)),
    )
    return pl.pallas_call(
        _proj_kernel,
        grid=(B, nt),
        in_specs=[pl.BlockSpec((1, tm, D_MODEL), lambda b, i: (b, i, 0))]
        + [full(w) for w in weights] + [tab(t) for t in tabs],
        out_specs=out_specs,
        out_shape=out_shape,
        compiler_params=pltpu.CompilerParams(
            dimension_semantics=("parallel", "parallel"), vmem_limit_bytes=VMEM_LIMIT),
        name="proj",
    )(x, *weights, *tabs)


def _swa_kernel(sink_ref, qaT_ref, kp_ref, kc_ref, kn_ref, vp_ref, vc_ref, vn_ref, oT_ref, s_ref, *, seq):
    tq = qaT_ref.shape[2]
    q0 = pl.program_id(1) * tq
    kcat = jnp.concatenate([kp_ref[0], kc_ref[0], kn_ref[0]], axis=0)
    vcat = jnp.concatenate([vp_ref[0], vc_ref[0], vn_ref[0]], axis=1)
    nk = 3 * WINDOW
    wide = GA * WINDOW
    r = lax.broadcasted_iota(jnp.int32, (nk, wide), 0)
    lane = lax.broadcasted_iota(jnp.int32, (nk, wide), 1)
    d = r - (lane & (WINDOW - 1))
    band = (d >= 0) & (d <= 2 * WINDOW)
    head = lax.broadcasted_iota(jnp.int32, (1, wide), 1) // WINDOW
    units = [(c, g) for c in range(tq // WINDOW) for g in range(KVA)]

    def scores(slot, c, g):
        q4 = jnp.concatenate(
            [qaT_ref[0, (g * GA + h) * LANE:(g * GA + h + 1) * LANE, c * WINDOW:(c + 1) * WINDOW]
             for h in range(GA)], axis=1)
        valid = band
        if c in (0, tq // WINDOW - 1):
            kpos = q0 + (c - 1) * WINDOW + r
            valid = band & (kpos >= 0) & (kpos < seq)
        s = jnp.where(valid, _dot(kcat[c * WINDOW:c * WINDOW + nk], q4), NEG)
        s_ref[slot] = s
        return jnp.max(s, axis=0, keepdims=True)

    def finish(slot, c, g, cmax):
        sink = jnp.zeros((1, wide), F32)
        for h in range(GA):
            sink = jnp.where(head == h, sink_ref[g * GA + h] * LOG2E, sink)
        m = jnp.maximum(cmax, sink)
        p = jnp.exp2(s_ref[slot] - m).astype(BF16)
        pv = _dot(vcat[g * V_ROWS:(g + 1) * V_ROWS, c * WINDOW:c * WINDOW + nk], p)
        o = (pv[:DA] / (pv[DA:DA + 1] + jnp.exp2(sink - m))).astype(BF16)
        for h in range(GA):
            hq = g * GA + h
            oT_ref[0, hq * DA:(hq + 1) * DA, c * WINDOW:(c + 1) * WINDOW] = o[:, h * WINDOW:(h + 1) * WINDOW]

    cmax = scores(0, *units[0])
    for u, (c, g) in enumerate(units):
        nxt = scores((u + 1) % 2, *units[u + 1]) if u + 1 < len(units) else None
        finish(u % 2, c, g, cmax)
        cmax = nxt


def _swa(sink, qaT, ka, vaT):
    B, _, S = qaT.shape
    tq = SWA_TILE
    r = tq // WINDOW
    nb = S // WINDOW
    prev = lambda b, i, *_: (b, jnp.maximum(i * r - 1, 0), 0)
    nxt = lambda b, i, *_: (b, jnp.minimum(i * r + r, nb - 1), 0)
    prev_t = lambda b, i, *_: (b, 0, jnp.maximum(i * r - 1, 0))
    nxt_t = lambda b, i, *_: (b, 0, jnp.minimum(i * r + r, nb - 1))
    grid_spec = pltpu.PrefetchScalarGridSpec(
        num_scalar_prefetch=1,
        grid=(B, S // tq),
        in_specs=[
            pl.BlockSpec((1, HA * LANE, tq), lambda b, i, *_: (b, 0, i)),
            pl.BlockSpec((1, WINDOW, KVA * DA), prev),
            pl.BlockSpec((1, tq, KVA * DA), lambda b, i, *_: (b, i, 0)),
            pl.BlockSpec((1, WINDOW, KVA * DA), nxt),
            pl.BlockSpec((1, KVA * V_ROWS, WINDOW), prev_t),
            pl.BlockSpec((1, KVA * V_ROWS, tq), lambda b, i, *_: (b, 0, i)),
            pl.BlockSpec((1, KVA * V_ROWS, WINDOW), nxt_t),
        ],
        out_specs=pl.BlockSpec((1, HA * DA, tq), lambda b, i, *_: (b, 0, i)),
        scratch_shapes=[pltpu.VMEM((2, 3 * WINDOW, GA * WINDOW), F32)],
    )
    return pl.pallas_call(
        functools.partial(_swa_kernel, seq=S),
        grid_spec=grid_spec,
        out_shape=jax.ShapeDtypeStruct((B, HA * DA, S), BF16),
        compiler_params=pltpu.CompilerParams(
            dimension_semantics=("parallel", "parallel"), vmem_limit_bytes=VMEM_LIMIT),
        name="swa",
    )(sink, qaT, ka, ka, ka, vaT, vaT, vaT)


def _mla_kernel(qT_ref, k_ref, vT_ref, oT_ref, s_ref):
    nt, _, tw = qT_ref.shape[2:]
    nvb, nv, tv = vT_ref.shape[2:]
    tk, tq = s_ref.shape[1:]
    nkb = nvb * tv // tk
    vper = tk // tv
    sub = tw // tq
    nq = nt * sub

    def scores(slot, tile, half, j):
        q = qT_ref[0, 0, tile, :, half * tq:(half + 1) * tq]
        cmax = None
        for r in range(0, tk, MLA_SCORE_ROWS):
            s = _dot(k_ref[0, j * tk + r:j * tk + r + MLA_SCORE_ROWS, :], q)
            s_ref[slot, r:r + MLA_SCORE_ROWS] = s
            cm = jnp.max(s, axis=0, keepdims=True)
            cmax = cm if cmax is None else jnp.maximum(cmax, cm)
        return cmax

    def softmax_pv(slot, j, cmax, m, acc):
        m_new = jnp.maximum(m, cmax)
        alpha = jnp.exp2(m - m_new)
        p = jnp.exp2(s_ref[slot] - m_new).astype(BF16)
        pv = _dot(vT_ref[0, 0, j * vper], p[:tv])
        for c in range(1, vper):
            pv += _dot(vT_ref[0, 0, j * vper + c], p[c * tv:(c + 1) * tv])
        return m_new, alpha * acc + pv

    unroll = min(nq, MLA_Q_TILES_PER_TRIP)
    assert nq % unroll == 0 and unroll % sub == 0
    tiles_per_trip = unroll // sub

    def q_tiles(t, cmax):
        for u in range(unroll):
            tile, half = t * tiles_per_trip + u // sub, u % sub
            m, acc = jnp.full((1, tq), -jnp.inf, F32), jnp.zeros((nv, tq), F32)
            for j in range(nkb):
                if j + 1 < nkb:
                    nxt = scores((j + 1) % 2, tile, half, j + 1)
                elif u + 1 < unroll:
                    nxt = scores(0, t * tiles_per_trip + (u + 1) // sub, (u + 1) % sub, 0)
                else:
                    nxt = scores(0, jnp.minimum((t + 1) * tiles_per_trip, nt - 1), 0, 0)
                m, acc = softmax_pv(j % 2, j, cmax, m, acc)
                cmax = nxt
            oT_ref[0, 0, tile, :, half * tq:(half + 1) * tq] = (acc[:DV] / acc[DV:DV + 1]).astype(BF16)
        return cmax

    lax.fori_loop(0, nq // unroll, q_tiles, scores(0, 0, 0, 0))


def _mla(qT, k, vT):
    B, _, nq, _, tq = qT.shape
    nvb, nv, tv = vT.shape[2:]
    S = nq * tq
    tk = MLA_KEY_BLOCK
    assert tk % tv == 0 and S % (2 * tk) == 0
    return pl.pallas_call(
        _mla_kernel,
        grid=(B, HB),
        in_specs=[
            pl.BlockSpec((1, 1, nq, LANE, tq), lambda b, h: (b, h, 0, 0, 0)),
            pl.BlockSpec((1, S, LANE), lambda b, h: (b, 0, h)),
            pl.BlockSpec((1, 1, nvb, nv, tv), lambda b, h: (b, h, 0, 0, 0)),
        ],
        out_specs=pl.BlockSpec((1, 1, nq, DV, tq), lambda b, h: (b, h, 0, 0, 0)),
        out_shape=jax.ShapeDtypeStruct((B, HB, nq, DV, tq), BF16),
        scratch_shapes=[pltpu.VMEM((2, tk, MLA_Q_TILE), F32)],
        compiler_params=pltpu.CompilerParams(
            dimension_semantics=("parallel", "parallel"), vmem_limit_bytes=VMEM_LIMIT),
        name="mla",
    )(qT, k, vT)


def _ffn_kernel(x_ref, oaT_ref, obT_ref, woa_ref, wob_ref, gffn_ref, wg_ref, wu_ref, wd_ref, gfin_ref, y_ref):
    tm = x_ref.shape[1]
    half = tm // 2
    rows = (slice(0, half), slice(half, tm))

    def norm(x, g_ref):
        return x * lax.rsqrt(jnp.mean(x * x, axis=-1, keepdims=True) + EPS) * g_ref[...]

    def attn_out(r):
        obT = obT_ref[0, :, 0, :, r].reshape(HB * DV, half)
        return x_ref[0, r] + _dot_tn(oaT_ref[0, :, r], woa_ref[...]) + _dot_tn(obT, wob_ref[...])

    def gate_up(x):
        h = norm(x, gffn_ref).astype(BF16)
        return _dot(h, wg_ref[...]), _dot(h, wu_ref[...])

    def act(g, u):
        return (g / (1.0 + jnp.exp(-g)) * u).astype(BF16)

    xa = attn_out(rows[0])
    xb = attn_out(rows[1])
    ga, ua = gate_up(xa)
    gb, ub = gate_up(xb)
    xa = xa + _dot(act(ga, ua), wd_ref[...])
    xb = xb + _dot(act(gb, ub), wd_ref[...])
    y_ref[0, rows[0]] = norm(xa, gfin_ref)
    y_ref[0, rows[1]] = norm(xb, gfin_ref)


def _ffn(x, oaT, obT, p):
    B, S, _ = x.shape
    tm = TOK_TILE
    full = lambda a: pl.BlockSpec(a.shape, lambda b, i: (0,) * a.ndim, pipeline_mode=pl.Buffered(1))
    weights = (p["w_oa"], p["w_ob"], p["g_ffn"], p["w_gate"], p["w_up"], p["w_down"], p["g_final"])
    return pl.pallas_call(
        _ffn_kernel,
        grid=(B, S // tm),
        in_specs=[
            pl.BlockSpec((1, tm, D_MODEL), lambda b, i: (b, i, 0)),
            pl.BlockSpec((1, HA * DA, tm), lambda b, i: (b, 0, i)),
            pl.BlockSpec((1, HB, 1, DV, tm), lambda b, i: (b, 0, i, 0, 0)),
        ] + [full(w) for w in weights],
        out_specs=pl.BlockSpec((1, tm, D_MODEL), lambda b, i: (b, i, 0)),
        out_shape=jax.ShapeDtypeStruct(x.shape, F32),
        compiler_params=pltpu.CompilerParams(
            dimension_semantics=("parallel", "parallel"), vmem_limit_bytes=VMEM_LIMIT),
        name="ffn",
    )(x, oaT, obT, *weights)


def _rope_tables_t(seq, dim):
    inv = 1.0 / (ROPE_THETA ** (jnp.arange(0, dim, 2, dtype=F32) / dim))
    ang = jnp.arange(seq, dtype=F32)[:, None] * inv[None, :]
    return jnp.cos(ang).T, jnp.sin(ang).T


def _prep_params(g_mix, w_in, sink, cq_g, w_uq, ckv_g, w_ukv, w_o, g_ffn, w_gate, w_up, w_down, g_final):
    w_uq3 = w_uq.reshape(Q_RANK, HB, DN + DR)
    w_uq_pad = jnp.pad(w_uq3, ((0, 0), (0, 0), (0, LANE - DN - DR))).reshape(Q_RANK, HB * LANE)
    w_ukv3 = w_ukv.reshape(KV_RANK, HB, DN + DV)
    w_uk, w_uv = w_ukv3[..., :DN], w_ukv3[..., DN:]
    top = jnp.pad(w_uk, ((0, 0), (0, 0), (0, LANE - DN)))
    eye = jnp.pad(jnp.eye(DR, dtype=F32)[:, None, :], ((0, 0), (0, 0), (DN, LANE - DN - DR)))
    bot = jnp.broadcast_to(eye, (DR, HB, LANE))
    w_k = jnp.concatenate([top, bot], axis=0).reshape(KV_RANK + DR, HB * LANE)
    return {
        "g_mix": g_mix.reshape(1, D_MODEL),
        "w_inT": w_in.T.astype(BF16),
        "cq_g": cq_g.reshape(Q_RANK, 1),
        "w_uqT": w_uq_pad.T.astype(BF16),
        "ckv_g": ckv_g.reshape(KV_RANK, 1),
        "w_k": w_k.astype(BF16),
        "w_vT": w_uv.reshape(KV_RANK, HB * DV).T.astype(BF16),
        "sink": sink,
        "w_oa": w_o[:HA * DA].astype(BF16),
        "w_ob": w_o[HA * DA:].astype(BF16),
        "g_ffn": g_ffn.reshape(1, D_MODEL),
        "w_gate": w_gate.astype(BF16),
        "w_up": w_up.astype(BF16),
        "w_down": w_down.astype(BF16),
        "g_final": g_final.reshape(1, D_MODEL),
    }


def _trunk(x, p):
    S = x.shape[1]
    tabs = _rope_tables_t(S, DA) + _rope_tables_t(S, DR)
    qaT, ka, vaT, qT, k, vT = _proj(x, p, tabs)
    oaT = _swa(p["sink"], qaT, ka, vaT)
    obT = _mla(qT, k, vT)
    return _ffn(x, oaT, obT, p)


def kernel(x_prompt, x_sample, g_mix, w_in, sink, cq_g, w_uq, ckv_g, w_ukv, w_o, g_ffn,
           w_gate, w_up, w_down, g_final):
    p = _prep_params(g_mix[0], w_in[0], sink[0], cq_g[0], w_uq[0], ckv_g[0], w_ukv[0], w_o[0],
                     g_ffn[0], w_gate[0], w_up[0], w_down[0], g_final)
    return _trunk(x_prompt, p), _trunk(x_sample, p)
```

```python
import functools
import math

import jax
import jax.numpy as jnp
from jax import lax
from jax.experimental import pallas as pl
from jax.experimental.pallas import tpu as pltpu

D_MODEL = 1024
EPS = 1e-6
ROPE_THETA = 10000.0
WINDOW = 128
HA, KVA, DA = 8, 2, 64
GA = HA // KVA
HB, Q_RANK, KV_RANK, DN, DR, DV = 8, 384, 256, 64, 32, 64
D_IN = HA * DA + 2 * KVA * DA + Q_RANK + KV_RANK + DR
D_FF = int(math.ceil(8 * D_MODEL / 3 / 256) * 256)
NEG = -1e30

LANE = 128
V_ROWS = 128
LOG2E = math.log2(math.e)
SWA_Q_SCALE = DA ** -0.5 * LOG2E
MLA_Q_SCALE = (DN + DR) ** -0.5 * LOG2E
OFF_KA = HA * DA
OFF_VA = OFF_KA + KVA * DA
OFF_CQ = OFF_VA + KVA * DA
OFF_CKV = OFF_CQ + Q_RANK
OFF_KR = OFF_CKV + KV_RANK

TOK_TILE = 512
SWA_TILE = 2048
MLA_Q_TILE = 256
MLA_KEY_BLOCK = 2048
MLA_SCORE_ROWS = 1024
MLA_Q_TILES_PER_TRIP = 8
VMEM_LIMIT = 56 * 1024 * 1024

F32 = jnp.float32
BF16 = jnp.bfloat16


def _dot(a, b):
    return jnp.dot(a, b, preferred_element_type=F32)


def _dot_tn(a, b):
    return lax.dot_general(a, b, (((0,), (0,)), ((), ())), preferred_element_type=F32)


def _dot_nt(a, b):
    return lax.dot_general(a, b, (((1,), (1,)), ((), ())), preferred_element_type=F32)


def _rope_rows(x1, x2, c, s):
    return x1 * c - x2 * s, x2 * c + x1 * s


def _proj_kernel(x_ref, gmix_ref, winT_ref, cqg_ref, wuqT_ref, ckvg_ref, wk_ref, wvT_ref,
                 cosa_ref, sina_ref, cosr_ref, sinr_ref,
                 qaT_ref, ka_ref, vaT_ref, qT_ref, k_ref, vT_ref):
    tm = x_ref.shape[1]
    parts = []
    for r in (slice(0, tm // 2), slice(tm // 2, tm)):
        x = x_ref[0, r]
        ms = jnp.mean(x * x, axis=-1, keepdims=True)
        h = (x * lax.rsqrt(ms + EPS) * gmix_ref[...]).astype(BF16)
        z_lat = _dot_nt(winT_ref[OFF_CQ:], h)
        parts.append(jnp.concatenate([_dot_nt(winT_ref[:OFF_CQ], h), z_lat], axis=0))
    zT = jnp.concatenate(parts, axis=1)

    ca, sa = cosa_ref[...], sina_ref[...]
    cr, sr = cosr_ref[...], sinr_ref[...]
    half = DA // 2

    qaT_ref[0] = jnp.zeros(qaT_ref.shape[1:], BF16)
    for hq in range(HA):
        r = hq * DA
        o1, o2 = _rope_rows(zT[r:r + half], zT[r + half:r + DA], ca, sa)
        base = hq * LANE + (hq // GA) * DA
        qaT_ref[0, base:base + half, :] = (o1 * SWA_Q_SCALE).astype(BF16)
        qaT_ref[0, base + half:base + DA, :] = (o2 * SWA_Q_SCALE).astype(BF16)

    kaT = []
    for g in range(KVA):
        r = OFF_KA + g * DA
        kaT.extend(_rope_rows(zT[r:r + half], zT[r + half:r + DA], ca, sa))
    ka_ref[0] = jnp.concatenate(kaT, axis=0).T.astype(BF16)
    ones_row = lax.broadcasted_iota(jnp.int32, (V_ROWS - DV, tm), 0) == 0
    for g in range(KVA):
        r = OFF_VA + g * DA
        vaT_ref[0, g * V_ROWS:g * V_ROWS + DA, :] = zT[r:r + DA].astype(BF16)
        vaT_ref[0, g * V_ROWS + DA:(g + 1) * V_ROWS, :] = ones_row.astype(BF16)

    cq = zT[OFF_CQ:OFF_CKV]
    cqn = (cq * lax.rsqrt(jnp.mean(cq * cq, axis=0, keepdims=True) + EPS) * cqg_ref[...]).astype(BF16)
    qT = _dot(wuqT_ref[...], cqn) * MLA_Q_SCALE
    qT_ref[0, :, 0] = qT.astype(BF16).reshape(HB, LANE, tm)
    hr = DR // 2
    for hh in range(HB):
        r = hh * LANE + DN
        o1, o2 = _rope_rows(qT[r:r + hr], qT[r + hr:r + DR], cr, sr)
        qT_ref[0, hh, 0, DN:DN + hr, :] = o1.astype(BF16)
        qT_ref[0, hh, 0, DN + hr:DN + DR, :] = o2.astype(BF16)

    ckv = zT[OFF_CKV:OFF_KR]
    ckvn = (ckv * lax.rsqrt(jnp.mean(ckv * ckv, axis=0, keepdims=True) + EPS) * ckvg_ref[...]).astype(BF16)
    krT = zT[OFF_KR:D_IN]
    k1, k2 = _rope_rows(krT[:hr], krT[hr:], cr, sr)
    kin = jnp.concatenate([ckvn, k1.astype(BF16), k2.astype(BF16)], axis=0)
    k_ref[0] = _dot_tn(kin, wk_ref[...]).astype(BF16)
    vT_ref[0, :, 0, :DV] = _dot(wvT_ref[...], ckvn).astype(BF16).reshape(HB, DV, tm)
    vT_ref[0, :, 0, DV:] = jnp.broadcast_to(ones_row.astype(BF16), (HB, V_ROWS - DV, tm))


def _proj(x, p, tabs):
    B, S, _ = x.shape
    tm = TOK_TILE
    nt = S // tm
    full = lambda a: pl.BlockSpec(a.shape, lambda b, i: (0,) * a.ndim)
    tab = lambda a: pl.BlockSpec((a.shape[0], tm), lambda b, i: (0, i))
    weights = (p["g_mix"], p["w_inT"], p["cq_g"], p["w_uqT"], p["ckv_g"], p["w_k"], p["w_vT"])
    out_shape = (
        jax.ShapeDtypeStruct((B, HA * LANE, S), BF16),
        jax.ShapeDtypeStruct((B, S, KVA * DA), BF16),
        jax.ShapeDtypeStruct((B, KVA * V_ROWS, S), BF16),
        jax.ShapeDtypeStruct((B, HB, nt, LANE, tm), BF16),
        jax.ShapeDtypeStruct((B, S, HB * LANE), BF16),
        jax.ShapeDtypeStruct((B, HB, nt, V_ROWS, tm), BF16),
    )
    out_specs = (
        pl.BlockSpec((1, HA * LANE, tm), lambda b, i: (b, 0, i)),
        pl.BlockSpec((1, tm, KVA * DA), lambda b, i: (b, i, 0)),
        pl.BlockSpec((1, KVA * V_ROWS, tm), lambda b, i: (b, 0, i)),
        pl.BlockSpec((1, HB, 1, LANE, tm), lambda b, i: (b, 0, i, 0, 0)),
        pl.BlockSpec((1, tm, HB * LANE), lambda b, i: (b, i, 0)),
        pl.BlockSpec((1, HB, 1, V_ROWS, tm), lambda b, i: (b, 0, i, 0, 0)),
    )
    return pl.pallas_call(
        _proj_kernel,
        grid=(B, nt),
        in_specs=[pl.BlockSpec((1, tm, D_MODEL), lambda b, i: (b, i, 0))]
        + [full(w) for w in weights] + [tab(t) for t in tabs],
        out_specs=out_specs,
        out_shape=out_shape,
        compiler_params=pltpu.CompilerParams(
            dimension_semantics=("parallel", "parallel"), vmem_limit_bytes=VMEM_LIMIT),
        name="proj",
    )(x, *weights, *tabs)


def _swa_kernel(sink_ref, qaT_ref, kp_ref, kc_ref, kn_ref, vp_ref, vc_ref, vn_ref, oT_ref, s_ref, *, seq):
    tq = qaT_ref.shape[2]
    q0 = pl.program_id(1) * tq
    kcat = jnp.concatenate([kp_ref[0], kc_ref[0], kn_ref[0]], axis=0)
    vcat = jnp.concatenate([vp_ref[0], vc_ref[0], vn_ref[0]], axis=1)
    nk = 3 * WINDOW
    wide = GA * WINDOW
    r = lax.broadcasted_iota(jnp.int32, (nk, wide), 0)
    lane = lax.broadcasted_iota(jnp.int32, (nk, wide), 1)
    d = r - (lane & (WINDOW - 1))
    band = (d >= 0) & (d <= 2 * WINDOW)
    head = lax.broadcasted_iota(jnp.int32, (1, wide), 1) // WINDOW
    units = [(c, g) for c in range(tq // WINDOW) for g in range(KVA)]

    def scores(slot, c, g):
        q4 = jnp.concatenate(
            [qaT_ref[0, (g * GA + h) * LANE:(g * GA + h + 1) * LANE, c * WINDOW:(c + 1) * WINDOW]
             for h in range(GA)], axis=1)
        valid = band
        if c in (0, tq // WINDOW - 1):
            kpos = q0 + (c - 1) * WINDOW + r
            valid = band & (kpos >= 0) & (kpos < seq)
        s = jnp.where(valid, _dot(kcat[c * WINDOW:c * WINDOW + nk], q4), NEG)
        s_ref[slot] = s
        return jnp.max(s, axis=0, keepdims=True)

    def finish(slot, c, g, cmax):
        sink = jnp.zeros((1, wide), F32)
        for h in range(GA):
            sink = jnp.where(head == h, sink_ref[g * GA + h] * LOG2E, sink)
        m = jnp.maximum(cmax, sink)
        p = jnp.exp2(s_ref[slot] - m).astype(BF16)
        pv = _dot(vcat[g * V_ROWS:(g + 1) * V_ROWS, c * WINDOW:c * WINDOW + nk], p)
        o = (pv[:DA] / (pv[DA:DA + 1] + jnp.exp2(sink - m))).astype(BF16)
        for h in range(GA):
            hq = g * GA + h
            oT_ref[0, hq * DA:(hq + 1) * DA, c * WINDOW:(c + 1) * WINDOW] = o[:, h * WINDOW:(h + 1) * WINDOW]

    cmax = scores(0, *units[0])
    for u, (c, g) in enumerate(units):
        nxt = scores((u + 1) % 2, *units[u + 1]) if u + 1 < len(units) else None
        finish(u % 2, c, g, cmax)
        cmax = nxt


def _swa(sink, qaT, ka, vaT):
    B, _, S = qaT.shape
    tq = SWA_TILE
    r = tq // WINDOW
    nb = S // WINDOW
    prev = lambda b, i, *_: (b, jnp.maximum(i * r - 1, 0), 0)
    nxt = lambda b, i, *_: (b, jnp.minimum(i * r + r, nb - 1), 0)
    prev_t = lambda b, i, *_: (b, 0, jnp.maximum(i * r - 1, 0))
    nxt_t = lambda b, i, *_: (b, 0, jnp.minimum(i * r + r, nb - 1))
    grid_spec = pltpu.PrefetchScalarGridSpec(
        num_scalar_prefetch=1,
        grid=(B, S // tq),
        in_specs=[
            pl.BlockSpec((1, HA * LANE, tq), lambda b, i, *_: (b, 0, i)),
            pl.BlockSpec((1, WINDOW, KVA * DA), prev),
            pl.BlockSpec((1, tq, KVA * DA), lambda b, i, *_: (b, i, 0)),
            pl.BlockSpec((1, WINDOW, KVA * DA), nxt),
            pl.BlockSpec((1, KVA * V_ROWS, WINDOW), prev_t),
            pl.BlockSpec((1, KVA * V_ROWS, tq), lambda b, i, *_: (b, 0, i)),
            pl.BlockSpec((1, KVA * V_ROWS, WINDOW), nxt_t),
        ],
        out_specs=pl.BlockSpec((1, HA * DA, tq), lambda b, i, *_: (b, 0, i)),
        scratch_shapes=[pltpu.VMEM((2, 3 * WINDOW, GA * WINDOW), F32)],
    )
    return pl.pallas_call(
        functools.partial(_swa_kernel, seq=S),
        grid_spec=grid_spec,
        out_shape=jax.ShapeDtypeStruct((B, HA * DA, S), BF16),
        compiler_params=pltpu.CompilerParams(
            dimension_semantics=("parallel", "parallel"), vmem_limit_bytes=VMEM_LIMIT),
        name="swa",
    )(sink, qaT, ka, ka, ka, vaT, vaT, vaT)


def _mla_kernel(qT_ref, k_ref, vT_ref, oT_ref, s_ref):
    nt, _, tw = qT_ref.shape[2:]
    nvb, nv, tv = vT_ref.shape[2:]
    tk, tq = s_ref.shape[1:]
    nkb = nvb * tv // tk
    vper = tk // tv
    sub = tw // tq
    nq = nt * sub

    def scores(slot, tile, half, j):
        q = qT_ref[0, 0, tile, :, half * tq:(half + 1) * tq]
        cmax = None
        for r in range(0, tk, MLA_SCORE_ROWS):
            s = _dot(k_ref[0, j * tk + r:j * tk + r + MLA_SCORE_ROWS, :], q)
            s_ref[slot, r:r + MLA_SCORE_ROWS] = s
            cm = jnp.max(s, axis=0, keepdims=True)
            cmax = cm if cmax is None else jnp.maximum(cmax, cm)
        return cmax

    def softmax_pv(slot, j, cmax, m, acc):
        m_new = jnp.maximum(m, cmax)
        alpha = jnp.exp2(m - m_new)
        p = jnp.exp2(s_ref[slot] - m_new).astype(BF16)
        pv = _dot(vT_ref[0, 0, j * vper], p[:tv])
        for c in range(1, vper):
            pv += _dot(vT_ref[0, 0, j * vper + c], p[c * tv:(c + 1) * tv])
        return m_new, alpha * acc + pv

    unroll = min(nq, MLA_Q_TILES_PER_TRIP)
    assert nq % unroll == 0 and unroll % sub == 0
    tiles_per_trip = unroll // sub

    def q_tiles(t, cmax):
        for u in range(unroll):
            tile, half = t * tiles_per_trip + u // sub, u % sub
            m, acc = jnp.full((1, tq), -jnp.inf, F32), jnp.zeros((nv, tq), F32)
            for j in range(nkb):
                if j + 1 < nkb:
                    nxt = scores((j + 1) % 2, tile, half, j + 1)
                elif u + 1 < unroll:
                    nxt = scores(0, t * tiles_per_trip + (u + 1) // sub, (u + 1) % sub, 0)
                else:
                    nxt = scores(0, jnp.minimum((t + 1) * tiles_per_trip, nt - 1), 0, 0)
                m, acc = softmax_pv(j % 2, j, cmax, m, acc)
                cmax = nxt
            oT_ref[0, 0, tile, :, half * tq:(half + 1) * tq] = (acc[:DV] / acc[DV:DV + 1]).astype(BF16)
        return cmax

    lax.fori_loop(0, nq // unroll, q_tiles, scores(0, 0, 0, 0))


def _mla(qT, k, vT):
    B, _, nq, _, tq = qT.shape
    nvb, nv, tv = vT.shape[2:]
    S = nq * tq
    tk = MLA_KEY_BLOCK
    assert tk % tv == 0 and S % (2 * tk) == 0
    return pl.pallas_call(
        _mla_kernel,
        grid=(B, HB),
        in_specs=[
            pl.BlockSpec((1, 1, nq, LANE, tq), lambda b, h: (b, h, 0, 0, 0)),
            pl.BlockSpec((1, S, LANE), lambda b, h: (b, 0, h)),
            pl.BlockSpec((1, 1, nvb, nv, tv), lambda b, h: (b, h, 0, 0, 0)),
        ],
        out_specs=pl.BlockSpec((1, 1, nq, DV, tq), lambda b, h: (b, h, 0, 0, 0)),
        out_shape=jax.ShapeDtypeStruct((B, HB, nq, DV, tq), BF16),
        scratch_shapes=[pltpu.VMEM((2, tk, MLA_Q_TILE), F32)],
        compiler_params=pltpu.CompilerParams(
            dimension_semantics=("parallel", "parallel"), vmem_limit_bytes=VMEM_LIMIT),
        name="mla",
    )(qT, k, vT)


def _ffn_kernel(x_ref, oaT_ref, obT_ref, woa_ref, wob_ref, gffn_ref, wgu_ref, wd_ref, gfin_ref, y_ref):
    tm = x_ref.shape[1]
    half = tm // 2
    rows = (slice(0, half), slice(half, tm))

    def norm(x, g_ref):
        return x * lax.rsqrt(jnp.mean(x * x, axis=-1, keepdims=True) + EPS) * g_ref[...]

    def attn_out(r):
        obT = obT_ref[0, :, 0, :, r].reshape(HB * DV, half)
        return x_ref[0, r] + _dot_tn(oaT_ref[0, :, r], woa_ref[...]) + _dot_tn(obT, wob_ref[...])

    def gate_up(x):
        h = norm(x, gffn_ref).astype(BF16)
        gu = _dot(h, wgu_ref[...])
        return gu[:, :D_FF], gu[:, D_FF:]

    def act(g, u):
        return (g / (1.0 + jnp.exp(-g)) * u).astype(BF16)

    xa = attn_out(rows[0])
    xb = attn_out(rows[1])
    ga, ua = gate_up(xa)
    gb, ub = gate_up(xb)
    xa = xa + _dot(act(ga, ua), wd_ref[...])
    xb = xb + _dot(act(gb, ub), wd_ref[...])
    y_ref[0, rows[0]] = norm(xa, gfin_ref)
    y_ref[0, rows[1]] = norm(xb, gfin_ref)


def _ffn(x, oaT, obT, p):
    B, S, _ = x.shape
    tm = TOK_TILE
    full = lambda a: pl.BlockSpec(a.shape, lambda b, i: (0,) * a.ndim, pipeline_mode=pl.Buffered(1))
    weights = (p["w_oa"], p["w_ob"], p["g_ffn"], p["w_gu"], p["w_down"], p["g_final"])
    return pl.pallas_call(
        _ffn_kernel,
        grid=(B, S // tm),
        in_specs=[
            pl.BlockSpec((1, tm, D_MODEL), lambda b, i: (b, i, 0)),
            pl.BlockSpec((1, HA * DA, tm), lambda b, i: (b, 0, i)),
            pl.BlockSpec((1, HB, 1, DV, tm), lambda b, i: (b, 0, i, 0, 0)),
        ] + [full(w) for w in weights],
        out_specs=pl.BlockSpec((1, tm, D_MODEL), lambda b, i: (b, i, 0)),
        out_shape=jax.ShapeDtypeStruct(x.shape, F32),
        compiler_params=pltpu.CompilerParams(
            dimension_semantics=("parallel", "parallel"), vmem_limit_bytes=VMEM_LIMIT),
        name="ffn",
    )(x, oaT, obT, *weights)


def _rope_tables_t(seq, dim):
    inv = 1.0 / (ROPE_THETA ** (jnp.arange(0, dim, 2, dtype=F32) / dim))
    ang = jnp.arange(seq, dtype=F32)[:, None] * inv[None, :]
    return jnp.cos(ang).T, jnp.sin(ang).T


def _prep_params(g_mix, w_in, sink, cq_g, w_uq, ckv_g, w_ukv, w_o, g_ffn, w_gate, w_up, w_down, g_final):
    w_uq3 = w_uq.reshape(Q_RANK, HB, DN + DR)
    w_uq_pad = jnp.pad(w_uq3, ((0, 0), (0, 0), (0, LANE - DN - DR))).reshape(Q_RANK, HB * LANE)
    w_ukv3 = w_ukv.reshape(KV_RANK, HB, DN + DV)
    w_uk, w_uv = w_ukv3[..., :DN], w_ukv3[..., DN:]
    top = jnp.pad(w_uk, ((0, 0), (0, 0), (0, LANE - DN)))
    eye = jnp.pad(jnp.eye(DR, dtype=F32)[:, None, :], ((0, 0), (0, 0), (DN, LANE - DN - DR)))
    bot = jnp.broadcast_to(eye, (DR, HB, LANE))
    w_k = jnp.concatenate([top, bot], axis=0).reshape(KV_RANK + DR, HB * LANE)
    return {
        "g_mix": g_mix.reshape(1, D_MODEL),
        "w_inT": w_in.T.astype(BF16),
        "cq_g": cq_g.reshape(Q_RANK, 1),
        "w_uqT": w_uq_pad.T.astype(BF16),
        "ckv_g": ckv_g.reshape(KV_RANK, 1),
        "w_k": w_k.astype(BF16),
        "w_vT": w_uv.reshape(KV_RANK, HB * DV).T.astype(BF16),
        "sink": sink,
        "w_oa": w_o[:HA * DA].astype(BF16),
        "w_ob": w_o[HA * DA:].astype(BF16),
        "g_ffn": g_ffn.reshape(1, D_MODEL),
        "w_gu": jnp.concatenate([w_gate.astype(BF16), w_up.astype(BF16)], axis=1),
        "w_down": w_down.astype(BF16),
        "g_final": g_final.reshape(1, D_MODEL),
    }


def _trunk(x, p):
    S = x.shape[1]
    tabs = _rope_tables_t(S, DA) + _rope_tables_t(S, DR)
    qaT, ka, vaT, qT, k, vT = _proj(x, p, tabs)
    oaT = _swa(p["sink"], qaT, ka, vaT)
    obT = _mla(qT, k, vT)
    return _ffn(x, oaT, obT, p)


def kernel(x_prompt, x_sample, g_mix, w_in, sink, cq_g, w_uq, ckv_g, w_ukv, w_o, g_ffn,
           w_gate, w_up, w_down, g_final):
    p = _prep_params(g_mix[0], w_in[0], sink[0], cq_g[0], w_uq[0], ckv_g[0], w_ukv[0], w_o[0],
                     g_ffn[0], w_gate[0], w_up[0], w_down[0], g_final)
    return _trunk(x_prompt, p), _trunk(x_sample, p)
```
